```python
import math
import jax, jax.numpy as jnp
from jax import lax
import numpy as np

D_MODEL = 1024
BATCH = 8
SEQ = 4096
DEPTH = 1

GRID_W = 64
CTX_LEN = 256
N_MLA_HEADS = 8
QK_NOPE_DIM = 64
QK_ROPE_DIM = 32
V_HEAD_DIM = 64
Q_LORA_RANK = 384
KV_LORA_RANK = 256
MLA_WIDTH = N_MLA_HEADS * V_HEAD_DIM
S5_WIDTH = D_MODEL - MLA_WIDTH
S5_GROUP = 16
S5_GROUPS = S5_WIDTH // S5_GROUP
S5_STATE = 64
DT_MIN = 1e-3
DT_MAX = 1e-1
KV_END = Q_LORA_RANK + KV_LORA_RANK
ROPE_END = KV_END + QK_ROPE_DIM
D_IN = ROPE_END + S5_WIDTH
D_FF = ((8 * D_MODEL + 3 * 256 - 1) // (3 * 256)) * 256
ROPE_THETA = 10000.0
Q_BLOCK = 128
NORM_EPS = 1e-6
DN_ALPHA = (2.0 * DEPTH) ** 0.25
DN_BETA = (8.0 * DEPTH) ** -0.25

kernel_name = 'hymba_mla_s5_deepnorm_adaln_prefix_ctx'


def layer_norm(x, g=None, b=None):
    xf = x.astype(jnp.float32)
    mu = xf.mean(-1, keepdims=True)
    var = jnp.square(xf - mu).mean(-1, keepdims=True)
    y = (xf - mu) * lax.rsqrt(var + NORM_EPS)
    if g is not None:
        y = y * g.astype(jnp.float32) + b.astype(jnp.float32)
    return y.astype(x.dtype)


def rms_norm(x, g):
    xf = x.astype(jnp.float32)
    y = xf * lax.rsqrt(jnp.mean(xf * xf, -1, keepdims=True) + NORM_EPS)
    return (y * g.astype(jnp.float32)).astype(x.dtype)


def adaln(cond, w_ada, b_ada):
    return jnp.split(jax.nn.silu(cond) @ w_ada + b_ada, 6, axis=-1)


def modulate(x, shift, scale):
    return layer_norm(x) * (1 + scale) + shift


def axial_rope(rows):
    row = jnp.repeat(jnp.arange(rows), GRID_W).astype(jnp.float32)
    col = jnp.tile(jnp.arange(GRID_W), rows).astype(jnp.float32)
    n_freq = QK_ROPE_DIM // 4
    freqs = ROPE_THETA ** (-jnp.arange(n_freq, dtype=jnp.float32) / n_freq)
    ang = jnp.concatenate([row[:, None] * freqs, col[:, None] * freqs], axis=-1)
    return jnp.cos(ang), jnp.sin(ang)


def rope2d(x, cos, sin):
    xr = x.astype(jnp.float32).reshape(x.shape[:-1] + (QK_ROPE_DIM // 2, 2))
    x0, x1 = xr[..., 0], xr[..., 1]
    out = jnp.stack([x0 * cos - x1 * sin, x0 * sin + x1 * cos], axis=-1)
    return out.reshape(x.shape).astype(x.dtype)


def mla_queries(p, q_norm_g, w_uq):
    q_c = rms_norm(p[..., :Q_LORA_RANK], q_norm_g)
    q = jnp.einsum('blr,rhe->blhe', q_c, w_uq)
    return q[..., :QK_NOPE_DIM], q[..., QK_NOPE_DIM:]


def mla_keys(p, kv_norm_g, w_uk, w_uv):
    kv_c = rms_norm(p[..., Q_LORA_RANK:KV_END], kv_norm_g)
    k_rope = p[..., KV_END:ROPE_END]
    k_nope = jnp.einsum('blr,rhe->blhe', kv_c, w_uk)
    v = jnp.einsum('blr,rhe->blhe', kv_c, w_uv)
    return k_nope, k_rope, v


def block_attention(q_nope, q_rope, k_nope, k_rope, v):
    b, lq = q_nope.shape[:2]
    nb = lq // Q_BLOCK
    scale = (QK_NOPE_DIM + QK_ROPE_DIM) ** -0.5

    def blocks(t):
        return jnp.moveaxis(t.reshape((b, nb, Q_BLOCK) + t.shape[2:]), 1, 0)

    def one_block(qs):
        qn, qr = qs
        s = jnp.einsum('bqhe,bkhe->bhqk', qn, k_nope) + jnp.einsum('bqhe,bke->bhqk', qr, k_rope)
        p = jax.nn.softmax(s.astype(jnp.float32) * scale, axis=-1).astype(v.dtype)
        return jnp.einsum('bhqk,bkhe->bqhe', p, v)

    o = lax.map(one_block, (blocks(q_nope), blocks(q_rope)))
    return jnp.moveaxis(o, 0, 1).reshape(b, lq, MLA_WIDTH)


def s5_discretise(lam_re, lam_im, log_dt, b_re, b_im):
    lam_re, lam_im = lam_re.astype(jnp.float32), lam_im.astype(jnp.float32)
    b_re, b_im = b_re.astype(jnp.float32), b_im.astype(jnp.float32)
    dt = jnp.exp(log_dt.astype(jnp.float32))[:, None]
    mag = jnp.exp(lam_re * dt)
    a_re, a_im = mag * jnp.cos(lam_im * dt), mag * jnp.sin(lam_im * dt)
    den = lam_re * lam_re + lam_im * lam_im
    f_re = ((a_re - 1) * lam_re + a_im * lam_im) / den
    f_im = (a_im * lam_re - (a_re - 1) * lam_im) / den
    bb_re = f_re[..., None] * b_re - f_im[..., None] * b_im
    bb_im = f_re[..., None] * b_im + f_im[..., None] * b_re
    return a_re, a_im, bb_re, bb_im


def complex_affine_combine(e1, e2):
    a1r, a1i, b1r, b1i = e1
    a2r, a2i, b2r, b2i = e2
    return (a2r * a1r - a2i * a1i,
            a2r * a1i + a2i * a1r,
            a2r * b1r - a2i * b1i + b2r,
            a2r * b1i + a2i * b1r + b2i)


def s5_states(u, a_re, a_im, bb_re, bb_im, reverse, h0=None):
    if reverse:
        u = jnp.flip(u, 0)
    bu_re = jnp.einsum('lbgc,gpc->lbgp', u, bb_re)
    bu_im = jnp.einsum('lbgc,gpc->lbgp', u, bb_im)
    if h0 is not None:
        bu_re = bu_re.at[0].add(a_re * h0[0] - a_im * h0[1])
        bu_im = bu_im.at[0].add(a_re * h0[1] + a_im * h0[0])
    n = u.shape[0]
    a_seq_re = jnp.broadcast_to(a_re, (n, 1) + a_re.shape)
    a_seq_im = jnp.broadcast_to(a_im, (n, 1) + a_im.shape)
    _, _, h_re, h_im = lax.associative_scan(complex_affine_combine, (a_seq_re, a_seq_im, bu_re, bu_im), axis=0)
    if reverse:
        h_re, h_im = jnp.flip(h_re, 0), jnp.flip(h_im, 0)
    return h_re, h_im


def s5_readout(h, c_re, c_im):
    h_re, h_im = h
    return (jnp.einsum('lbgp,gcp->lbgc', h_re, c_re.astype(jnp.float32))
            - jnp.einsum('lbgp,gcp->lbgc', h_im, c_im.astype(jnp.float32)))


def s5_glu(y, w_glu, b_glu):
    l, b = y.shape[:2]
    y = jnp.swapaxes(y.reshape(l, b, S5_WIDTH), 0, 1)
    z = jax.nn.gelu(y)
    return z * jax.nn.sigmoid(z @ w_glu.astype(jnp.float32) + b_glu.astype(jnp.float32))


def to_groups(u):
    return jnp.swapaxes(u, 0, 1).reshape(u.shape[1], u.shape[0], S5_GROUPS, S5_GROUP).astype(jnp.float32)


def s5_mixer(u_lat, u_ctx, lp, with_ctx_out):
    ul, uc = to_groups(u_lat), to_groups(u_ctx)
    d = lp['s5_d'].astype(jnp.float32).reshape(S5_GROUPS, S5_GROUP)
    y_lat = ul * d
    y_ctx = uc * d if with_ctx_out else None
    for direction, reverse in ((0, False), (1, True)):
        a_re, a_im, bb_re, bb_im = s5_discretise(lp['s5_lambda_re'][direction], lp['s5_lambda_im'][direction],
                                                 lp['s5_log_dt'][direction], lp['s5_b_re'][direction],
                                                 lp['s5_b_im'][direction])
        h_ctx = s5_states(uc, a_re, a_im, bb_re, bb_im, reverse)
        edge = 0 if reverse else -1
        h_lat = s5_states(ul, a_re, a_im, bb_re, bb_im, reverse, h0=(h_ctx[0][edge], h_ctx[1][edge]))
        y_lat = y_lat + s5_readout(h_lat, lp['s5_c_re'][direction], lp['s5_c_im'][direction])
        if with_ctx_out:
            y_ctx = y_ctx + s5_readout(h_ctx, lp['s5_c_re'][direction], lp['s5_c_im'][direction])
    out_lat = s5_glu(y_lat, lp['s5_w_glu'], lp['s5_b_glu']).astype(u_lat.dtype)
    out_ctx = s5_glu(y_ctx, lp['s5_w_glu'], lp['s5_b_glu']).astype(u_ctx.dtype) if with_ctx_out else None
    return out_lat, out_ctx


def mixer_sublayer(u_lat, u_ctx, lp, cos, sin, with_ctx_out):
    p_lat = u_lat @ lp['w_in']
    p_ctx = u_ctx @ lp['w_in']
    qn, qr = mla_queries(p_lat, lp['q_norm_g'], lp['w_uq'])
    qr = rope2d(qr, cos[:, None, :], sin[:, None, :])
    kn, kr, v = mla_keys(p_lat, lp['kv_norm_g'], lp['w_uk'], lp['w_uv'])
    kr = rope2d(kr, cos, sin)
    kn_c, kr_c, v_c = mla_keys(p_ctx, lp['kv_norm_g'], lp['w_uk'], lp['w_uv'])
    att_lat = block_attention(qn, qr, jnp.concatenate([kn_c, kn], 1), jnp.concatenate([kr_c, kr], 1),
                              jnp.concatenate([v_c, v], 1))
    s5_lat, s5_ctx = s5_mixer(p_lat[..., ROPE_END:], p_ctx[..., ROPE_END:], lp, with_ctx_out)
    y_lat = jnp.concatenate([att_lat, s5_lat], axis=-1) @ lp['w_out']
    y_ctx = None
    if with_ctx_out:
        qn_c, qr_c = mla_queries(p_ctx, lp['q_norm_g'], lp['w_uq'])
        att_ctx = block_attention(qn_c, qr_c, kn_c, kr_c, v_c)
        y_ctx = jnp.concatenate([att_ctx, s5_ctx], axis=-1) @ lp['w_out']
    return y_lat, y_ctx


def swiglu(u, w_gate_up, w_down):
    gate, up = jnp.split(u @ w_gate_up, 2, axis=-1)
    return (jax.nn.silu(gate) * up) @ w_down


def trunk_layer(x, ctx, mod_lat, mod_ctx, lp, cos, sin, with_ctx_out):
    sh1, sc1, g1, sh2, sc2, g2 = mod_lat
    csh1, csc1, cg1, csh2, csc2, cg2 = mod_ctx
    y_lat, y_ctx = mixer_sublayer(modulate(x, sh1, sc1), modulate(ctx, csh1, csc1), lp, cos, sin, with_ctx_out)
    x = layer_norm(DN_ALPHA * x + g1 * y_lat, lp['ln1_g'], lp['ln1_b'])
    x = layer_norm(DN_ALPHA * x + g2 * swiglu(modulate(x, sh2, sc2), lp['w_gate_up'], lp['w_down']),
                   lp['ln2_g'], lp['ln2_b'])
    if with_ctx_out:
        ctx = layer_norm(DN_ALPHA * ctx + cg1 * y_ctx, lp['ln1_g'], lp['ln1_b'])
        ctx = layer_norm(DN_ALPHA * ctx + cg2 * swiglu(modulate(ctx, csh2, csc2), lp['w_gate_up'], lp['w_down']),
                         lp['ln2_g'], lp['ln2_b'])
    return x, ctx


def setup_inputs(seed: int = 0) -> dict:
    key = jax.random.key(seed)
    ks = iter(jax.random.split(key, 29))
    f32 = jnp.float32

    def nrm(shape, scale):
        return jax.random.normal(next(ks), shape, f32) * scale

    lam_re = -0.5 + nrm((DEPTH, 2, S5_GROUPS, S5_STATE), 0.01)
    lam_im = jnp.pi * jnp.arange(S5_STATE, dtype=f32) + nrm((DEPTH, 2, S5_GROUPS, S5_STATE), 0.01)
    return {
        'x': nrm((BATCH, SEQ, D_MODEL), 1.0),
        'c': nrm((BATCH, D_MODEL), 1.0),
        'ctx': nrm((BATCH, CTX_LEN, D_MODEL), 1.0),
        'c_ctx': nrm((D_MODEL,), 1.0),
        'w_ada': nrm((DEPTH, D_MODEL, 6 * D_MODEL), D_MODEL ** -0.5),
        'b_ada': nrm((DEPTH, 6 * D_MODEL), 0.02),
        'w_in': nrm((DEPTH, D_MODEL, D_IN), D_MODEL ** -0.5),
        'q_norm_g': 1.0 + nrm((DEPTH, Q_LORA_RANK), 0.02),
        'kv_norm_g': 1.0 + nrm((DEPTH, KV_LORA_RANK), 0.02),
        'w_uq': nrm((DEPTH, Q_LORA_RANK, N_MLA_HEADS, QK_NOPE_DIM + QK_ROPE_DIM), Q_LORA_RANK ** -0.5),
        'w_uk': nrm((DEPTH, KV_LORA_RANK, N_MLA_HEADS, QK_NOPE_DIM), KV_LORA_RANK ** -0.5),
        'w_uv': nrm((DEPTH, KV_LORA_RANK, N_MLA_HEADS, V_HEAD_DIM), KV_LORA_RANK ** -0.5 * DN_BETA),
        's5_lambda_re': lam_re,
        's5_lambda_im': lam_im,
        's5_log_dt': jax.random.uniform(next(ks), (DEPTH, 2, S5_GROUPS), f32,
                                        minval=math.log(DT_MIN), maxval=math.log(DT_MAX)),
        's5_b_re': nrm((DEPTH, 2, S5_GROUPS, S5_STATE, S5_GROUP), (2 * S5_GROUP) ** -0.5),
        's5_b_im': nrm((DEPTH, 2, S5_GROUPS, S5_STATE, S5_GROUP), (2 * S5_GROUP) ** -0.5),
        's5_c_re': nrm((DEPTH, 2, S5_GROUPS, S5_GROUP, S5_STATE), (2 * S5_STATE) ** -0.5),
        's5_c_im': nrm((DEPTH, 2, S5_GROUPS, S5_GROUP, S5_STATE), (2 * S5_STATE) ** -0.5),
        's5_d': nrm((DEPTH, S5_WIDTH), 1.0),
        's5_w_glu': nrm((DEPTH, S5_WIDTH, S5_WIDTH), S5_WIDTH ** -0.5),
        's5_b_glu': nrm((DEPTH, S5_WIDTH), 0.02),
        'w_out': nrm((DEPTH, D_MODEL, D_MODEL), D_MODEL ** -0.5 * DN_BETA),
        'ln1_g': 1.0 + nrm((DEPTH, D_MODEL), 0.02),
        'ln1_b': nrm((DEPTH, D_MODEL), 0.02),
        'w_gate_up': nrm((DEPTH, D_MODEL, 2 * D_FF), D_MODEL ** -0.5),
        'w_down': nrm((DEPTH, D_FF, D_MODEL), D_FF ** -0.5 * DN_BETA),
        'ln2_g': 1.0 + nrm((DEPTH, D_MODEL), 0.02),
        'ln2_b': nrm((DEPTH, D_MODEL), 0.02),
    }


def reference(x, c, ctx, c_ctx, w_ada, b_ada, w_in, q_norm_g, kv_norm_g, w_uq, w_uk, w_uv,
              s5_lambda_re, s5_lambda_im, s5_log_dt, s5_b_re, s5_b_im, s5_c_re, s5_c_im, s5_d,
              s5_w_glu, s5_b_glu, w_out, ln1_g, ln1_b, w_gate_up, w_down, ln2_g, ln2_b):
    ROWS = x.shape[1] // GRID_W
    cos, sin = axial_rope(ROWS)
    for i in range(DEPTH):
        lp = dict(w_in=w_in[i], q_norm_g=q_norm_g[i], kv_norm_g=kv_norm_g[i], w_uq=w_uq[i], w_uk=w_uk[i],
                  w_uv=w_uv[i], s5_lambda_re=s5_lambda_re[i], s5_lambda_im=s5_lambda_im[i],
                  s5_log_dt=s5_log_dt[i], s5_b_re=s5_b_re[i], s5_b_im=s5_b_im[i], s5_c_re=s5_c_re[i],
                  s5_c_im=s5_c_im[i], s5_d=s5_d[i], s5_w_glu=s5_w_glu[i], s5_b_glu=s5_b_glu[i],
                  w_out=w_out[i], ln1_g=ln1_g[i], ln1_b=ln1_b[i], w_gate_up=w_gate_up[i],
                  w_down=w_down[i], ln2_g=ln2_g[i], ln2_b=ln2_b[i])
        mod_lat = adaln(c[:, None, :], w_ada[i], b_ada[i])
        mod_ctx = adaln(c_ctx, w_ada[i], b_ada[i])
        x, ctx = trunk_layer(x, ctx, mod_lat, mod_ctx, lp, cos, sin, with_ctx_out=(i < DEPTH - 1))
    return x
```

```python
import functools
import math

import jax
import jax.numpy as jnp
from jax import lax
from jax.experimental import pallas as pl
from jax.experimental.pallas import tpu as pltpu

D_MODEL = 1024
GRID_W = 64
N_HEADS = 8
NOPE = 64
ROPE = 32
V_DIM = 64
Q_RANK = 384
KV_RANK = 256
MLA_WIDTH = N_HEADS * V_DIM
S5_WIDTH = D_MODEL - MLA_WIDTH
S5_GROUP = 16
S5_GROUPS = S5_WIDTH // S5_GROUP
S5_STATE = 64
S5_COLS = S5_GROUPS * S5_STATE
KV_END = Q_RANK + KV_RANK
ROPE_END = KV_END + ROPE
D_FF = 2816
ROPE_THETA = 10000.0
NORM_EPS = 1e-6
DEPTH = 1
DN_ALPHA = (2.0 * DEPTH) ** 0.25

LANES = 128
SUBLANES = 8
HEAD_PAD = LANES
ROPE_HALF = ROPE // 2
P_COLS = Q_RANK + KV_RANK + S5_WIDTH + LANES
VMEM_LIMIT = 56 * 1024 * 1024

TM = 512
BQ = 256
HEADS_PER_STEP = 2
TC = 64
SLAB = 512
FF_CHUNK = 256

F32 = jnp.float32
BF16 = jnp.bfloat16


def _params(*sem):
    return pltpu.CompilerParams(dimension_semantics=sem, vmem_limit_bytes=VMEM_LIMIT)


def _const_spec(shape):
    nd = len(shape)
    return pl.BlockSpec(shape, lambda *_: (0,) * nd, pipeline_mode=pl.Buffered(1))


def _layer_norm(x):
    mu = jnp.mean(x, axis=-1, keepdims=True)
    xc = x - mu
    var = jnp.mean(xc * xc, axis=-1, keepdims=True)
    return xc * lax.rsqrt(var + NORM_EPS)


def _rms_norm(x, g):
    return x * lax.rsqrt(jnp.mean(x * x, axis=-1, keepdims=True) + NORM_EPS) * g


def _dot(a, b):
    return jnp.dot(a, b, preferred_element_type=F32)


def _adaln_kernel(cond_ref, w_ref, b_ref, o_ref):
    cnd = cond_ref[...]
    act = cnd * jax.nn.sigmoid(cnd)
    o_ref[...] = _dot(act.astype(BF16), w_ref[...]) + b_ref[...]


def _adaln(cond, w_ada, b_ada):
    rows, n = cond.shape[0], w_ada.shape[1]
    tn = D_MODEL
    return pl.pallas_call(
        _adaln_kernel,
        grid=(n // tn,),
        in_specs=[pl.BlockSpec((rows, D_MODEL), lambda j: (0, 0)),
                  pl.BlockSpec((D_MODEL, tn), lambda j: (0, j)),
                  pl.BlockSpec((1, tn), lambda j: (0, j))],
        out_specs=pl.BlockSpec((rows, tn), lambda j: (0, j)),
        out_shape=jax.ShapeDtypeStruct((rows, n), F32),
        compiler_params=_params("arbitrary"),
        name="adaln",
    )(cond, w_ada, b_ada)


def _s5prep_kernel(lre_ref, lim_ref, ldt_ref, bre_ref, bim_ref, are_ref, aim_ref, bbre_ref, bbim_ref):
    lre, lim = lre_ref[...], lim_ref[...]
    dt = jnp.exp(ldt_ref[...])
    mag = jnp.exp(lre * dt)
    a_re, a_im = mag * jnp.cos(lim * dt), mag * jnp.sin(lim * dt)
    den = lre * lre + lim * lim
    f_re = ((a_re - 1) * lre + a_im * lim) / den
    f_im = (a_im * lre - (a_re - 1) * lim) / den
    b_re, b_im = bre_ref[...], bim_ref[...]
    are_ref[...] = a_re
    aim_ref[...] = a_im
    bbre_ref[...] = f_re * b_re - f_im * b_im
    bbim_ref[...] = f_re * b_im + f_im * b_re


def _s5prep(lam_re, lam_im, log_dt, b_re, b_im):
    rows = 2 * S5_GROUPS * S5_GROUP

    def per_row(t):
        return jnp.broadcast_to(t[:, :, None, :], (2, S5_GROUPS, S5_GROUP, S5_STATE)).reshape(rows, S5_STATE)

    ldt = jnp.broadcast_to(log_dt[:, :, None, None], (2, S5_GROUPS, S5_GROUP, S5_STATE)).reshape(rows, S5_STATE)
    bt_re = jnp.swapaxes(b_re, -1, -2).reshape(rows, S5_STATE)
    bt_im = jnp.swapaxes(b_im, -1, -2).reshape(rows, S5_STATE)
    shp = jax.ShapeDtypeStruct((rows, S5_STATE), F32)
    a_re, a_im, bb_re, bb_im = pl.pallas_call(
        _s5prep_kernel, out_shape=(shp, shp, shp, shp), name="s5prep",
    )(per_row(lam_re), per_row(lam_im), ldt, bt_re, bt_im)
    g4 = (2, S5_GROUPS, S5_GROUP, S5_STATE)
    return (a_re.reshape(g4)[:, :, 0], a_im.reshape(g4)[:, :, 0], bb_re.reshape(g4), bb_im.reshape(g4))


def _block_diag(t):
    eye = jnp.eye(S5_GROUPS, dtype=t.dtype)
    out = t[:, :, :, None, :] * eye[None, :, None, :, None]
    return out.reshape(2, S5_GROUPS * t.shape[2], S5_GROUPS * t.shape[3])


def _rope_tile(t, c, sa, sb):
    return t * c + pltpu.roll(t, LANES - ROPE_HALF, 1) * sa + pltpu.roll(t, ROPE_HALF, 1) * sb


def _inproj_kernel(with_q, x_ref, sh_ref, sc_ref, win_ref, qg_ref, kvg_ref, wuq_ref, wuk_ref, wuv_ref,
                   c_ref, sa_ref, sb_ref, *out_refs):
    if with_q:
        q_ref, k_ref, v_ref, u_ref = out_refs
    else:
        k_ref, v_ref, u_ref = out_refs
    x = x_ref[0]
    xm = _layer_norm(x) * (1.0 + sc_ref[0]) + sh_ref[0]
    p = _dot(xm.astype(BF16), win_ref[...])
    c, sa, sb = c_ref[...], sa_ref[...], sb_ref[...]

    kv_c = _rms_norm(p[:, Q_RANK:KV_END], kvg_ref[...]).astype(BF16)
    k = _dot(kv_c, wuk_ref[...])
    kr = _rope_tile(p[:, KV_END + S5_WIDTH:], c, sa, sb)
    for h in range(N_HEADS):
        k_ref[0, :, h * HEAD_PAD:(h + 1) * HEAD_PAD] = (k[:, h * HEAD_PAD:(h + 1) * HEAD_PAD] + kr).astype(BF16)
    v_ref[0] = _dot(kv_c, wuv_ref[...]).astype(BF16)
    u_ref[0] = p[:, KV_END:KV_END + S5_WIDTH]

    if with_q:
        q_c = _rms_norm(p[:, :Q_RANK], qg_ref[...]).astype(BF16)
        q = _dot(q_c, wuq_ref[...])
        for h in range(N_HEADS):
            sl = slice(h * HEAD_PAD, (h + 1) * HEAD_PAD)
            q_ref[0, :, sl] = _rope_tile(q[:, sl], c, sa, sb).astype(BF16)


def _inproj(x, mod3, mod_row, w_in_r, qg, kvg, wuq, wuk, wuv, rope_c, rope_sa, rope_sb, with_q):
    b, l, _ = x.shape
    tm = min(TM, l)
    row = (lambda bi: bi) if mod_row is None else (lambda bi: mod_row)
    tok = lambda w: pl.BlockSpec((1, tm, w), lambda bi, i: (bi, i, 0))
    in_specs = [
        tok(D_MODEL),
        pl.BlockSpec((1, 1, D_MODEL), lambda bi, i: (row(bi), 0, 0)),
        pl.BlockSpec((1, 1, D_MODEL), lambda bi, i: (row(bi), 0, 1)),
        _const_spec(w_in_r.shape), _const_spec(qg.shape), _const_spec(kvg.shape),
        _const_spec(wuq.shape), _const_spec(wuk.shape), _const_spec(wuv.shape),
        pl.BlockSpec((tm, LANES), lambda bi, i: (i, 0)),
        pl.BlockSpec((tm, LANES), lambda bi, i: (i, 0)),
        pl.BlockSpec((tm, LANES), lambda bi, i: (i, 0)),
    ]
    out_specs = [tok(N_HEADS * HEAD_PAD), tok(MLA_WIDTH), tok(S5_WIDTH)]
    out_shape = [jax.ShapeDtypeStruct((b, l, N_HEADS * HEAD_PAD), BF16),
                 jax.ShapeDtypeStruct((b, l, MLA_WIDTH), BF16),
                 jax.ShapeDtypeStruct((b, l, S5_WIDTH), F32)]
    if with_q:
        out_specs = [tok(N_HEADS * HEAD_PAD)] + out_specs
        out_shape = [jax.ShapeDtypeStruct((b, l, N_HEADS * HEAD_PAD), BF16)] + out_shape
    return pl.pallas_call(
        functools.partial(_inproj_kernel, with_q),
        grid=(b, l // tm),
        in_specs=in_specs, out_specs=out_specs, out_shape=out_shape,
        compiler_params=_params("parallel", "parallel"),
        name="inproj_lat" if with_q else "inproj_ctx",
    )(x, mod3, mod3, w_in_r, qg, kvg, wuq, wuk, wuv, rope_c, rope_sa, rope_sb)


def _attn_kernel(q_ref, kc_ref, kl_ref, vc_ref, vl_ref, o_ref):
    scale = (NOPE + ROPE) ** -0.5
    nt = (((1,), (1,)), ((), ()))
    for h in range(HEADS_PER_STEP):
        ks = slice(h * HEAD_PAD, (h + 1) * HEAD_PAD)
        vs = slice(h * V_DIM, (h + 1) * V_DIM)
        q = q_ref[0, :, ks]
        s_c = lax.dot_general(q, kc_ref[0, :, ks], nt, preferred_element_type=F32)
        s_l = lax.dot_general(q, kl_ref[0, :, ks], nt, preferred_element_type=F32)
        m = jnp.maximum(jnp.max(s_c, axis=-1, keepdims=True), jnp.max(s_l, axis=-1, keepdims=True))
        e_c = jnp.exp((s_c - m) * scale)
        e_l = jnp.exp((s_l - m) * scale)
        den = jnp.sum(e_c, axis=-1, keepdims=True) + jnp.sum(e_l, axis=-1, keepdims=True)
        o = _dot(e_c.astype(BF16), vc_ref[0, :, vs]) + _dot(e_l.astype(BF16), vl_ref[0, :, vs])
        o_ref[0, :, vs] = (o / den).astype(o_ref.dtype)


def _attention(q, k_c, k_l, v_c, v_l):
    b, l, _ = q.shape
    lc = k_c.shape[1]
    kw, vw = HEADS_PER_STEP * HEAD_PAD, HEADS_PER_STEP * V_DIM
    return pl.pallas_call(
        _attn_kernel,
        grid=(b, N_HEADS // HEADS_PER_STEP, l // BQ),
        in_specs=[pl.BlockSpec((1, BQ, kw), lambda bi, hp, i: (bi, i, hp)),
                  pl.BlockSpec((1, lc, kw), lambda bi, hp, i: (bi, 0, hp)),
                  pl.BlockSpec((1, l, kw), lambda bi, hp, i: (bi, 0, hp)),
                  pl.BlockSpec((1, lc, vw), lambda bi, hp, i: (bi, 0, hp)),
                  pl.BlockSpec((1, l, vw), lambda bi, hp, i: (bi, 0, hp))],
        out_specs=pl.BlockSpec((1, BQ, vw), lambda bi, hp, i: (bi, i, hp)),
        out_shape=jax.ShapeDtypeStruct((b, l, MLA_WIDTH), BF16),
        compiler_params=_params("parallel", "parallel", "arbitrary"),
        name="attn",
    )(q, k_c, k_l, v_c, v_l)


def _s5scan_kernel(u_ref, bmat_ref, a_ref, cmat_ref, dskip_ref, y_ref, hs_ref, st_ref):
    d = pl.program_id(0)

    @pl.when(pl.program_id(1) == 0)
    def _():
        st_ref[...] = jnp.zeros_like(st_ref)

    u = u_ref[...].reshape(TC * SUBLANES, S5_WIDTH)
    hs_ref[...] = _dot(u.astype(BF16), bmat_ref[0])

    for cs in range(0, S5_COLS, SLAB):
        re = slice(cs, cs + SLAB)
        im = slice(S5_COLS + cs, S5_COLS + cs + SLAB)
        a_re, a_im = a_ref[0, :, re], a_ref[0, :, im]

        def step(tt, carry, re=re, im=im, a_re=a_re, a_im=a_im):
            h_re, h_im = carry
            t = jnp.where(d == 0, tt, TC - 1 - tt)
            rows = pl.ds(pl.multiple_of(t * SUBLANES, SUBLANES), SUBLANES)
            n_re = a_re * h_re - a_im * h_im + hs_ref[rows, re]
            n_im = a_re * h_im + a_im * h_re + hs_ref[rows, im]
            hs_ref[rows, re] = n_re
            hs_ref[rows, im] = n_im
            return n_re, n_im

        h_re, h_im = lax.fori_loop(0, TC, step, (st_ref[:, re], st_ref[:, im]), unroll=4)
        st_ref[:, re] = h_re
        st_ref[:, im] = h_im

    y = _dot(hs_ref[...].astype(BF16), cmat_ref[0]) + u * dskip_ref[0]
    y_ref[0] = y.reshape(TC, SUBLANES, S5_WIDTH)


def _s5scan(u_all, bmat, a_b, cmat, dskip, n_ctx_chunks):
    lk = u_all.shape[0]
    n = lk // TC

    def blk(d, i):
        rev = jnp.where(i < n_ctx_chunks, n_ctx_chunks - 1 - i, n + n_ctx_chunks - 1 - i)
        return jnp.where(d == 0, i, rev)

    return pl.pallas_call(
        _s5scan_kernel,
        grid=(2, n),
        in_specs=[pl.BlockSpec((TC, SUBLANES, S5_WIDTH), lambda d, i: (blk(d, i), 0, 0)),
                  pl.BlockSpec((1, S5_WIDTH, 2 * S5_COLS), lambda d, i: (d, 0, 0)),
                  pl.BlockSpec((1, SUBLANES, 2 * S5_COLS), lambda d, i: (d, 0, 0)),
                  pl.BlockSpec((1, 2 * S5_COLS, S5_WIDTH), lambda d, i: (d, 0, 0)),
                  pl.BlockSpec((1, 1, S5_WIDTH), lambda d, i: (d, 0, 0))],
        out_specs=pl.BlockSpec((1, TC, SUBLANES, S5_WIDTH), lambda d, i: (d, blk(d, i), 0, 0)),
        out_shape=jax.ShapeDtypeStruct((2, lk, SUBLANES, S5_WIDTH), F32),
        scratch_shapes=[pltpu.VMEM((TC * SUBLANES, 2 * S5_COLS), F32),
                        pltpu.VMEM((SUBLANES, 2 * S5_COLS), F32)],
        compiler_params=_params("arbitrary", "arbitrary"),
        name="s5scan",
    )(u_all, bmat, a_b, cmat, dskip)


def _tail_kernel(x_ref, att_ref, y_ref, g1_ref, sh2_ref, sc2_ref, g2_ref, wglu_ref, bglu_ref, wout_ref,
                 ln1g_ref, ln1b_ref, wg_ref, wu_ref, wd_ref, ln2g_ref, ln2b_ref, o_ref, acc_ref):
    z = jax.nn.gelu(y_ref[0])
    s5o = z * jax.nn.sigmoid(_dot(z.astype(BF16), wglu_ref[...]) + bglu_ref[...])
    mix = _dot(att_ref[0], wout_ref[:MLA_WIDTH, :]) + _dot(s5o.astype(BF16), wout_ref[MLA_WIDTH:, :])
    x1 = _layer_norm(DN_ALPHA * x_ref[0] + g1_ref[0] * mix) * ln1g_ref[...] + ln1b_ref[...]

    xm = (_layer_norm(x1) * (1.0 + sc2_ref[0]) + sh2_ref[0]).astype(BF16)
    for j in range(D_FF // FF_CHUNK):
        cols = slice(j * FF_CHUNK, (j + 1) * FF_CHUNK)
        gate = _dot(xm, wg_ref[:, cols])
        up = _dot(xm, wu_ref[:, cols])
        hidden = (gate * jax.nn.sigmoid(gate) * up).astype(BF16)
        part = _dot(hidden, wd_ref[cols, :])
        if j == 0:
            acc_ref[...] = part
        else:
            acc_ref[...] += part
    o_ref[0] = _layer_norm(DN_ALPHA * x1 + g2_ref[0] * acc_ref[...]) * ln2g_ref[...] + ln2b_ref[...]


def _tail(x, att, y_s5, mod3, w_glu, b_glu, w_out, ln1_g, ln1_b, w_gate, w_up, w_down, ln2_g, ln2_b):
    b, l, _ = x.shape
    tok = lambda w: pl.BlockSpec((1, TM, w), lambda bi, i: (bi, i, 0))
    mod = lambda j: pl.BlockSpec((1, 1, D_MODEL), lambda bi, i: (bi, 0, j))
    consts = (w_glu, b_glu, w_out, ln1_g, ln1_b, w_gate, w_up, w_down, ln2_g, ln2_b)
    return pl.pallas_call(
        _tail_kernel,
        grid=(b, l // TM),
        in_specs=[tok(D_MODEL), tok(MLA_WIDTH), tok(S5_WIDTH), mod(2), mod(3), mod(4), mod(5)]
                 + [_const_spec(w.shape) for w in consts],
        out_specs=tok(D_MODEL),
        out_shape=jax.ShapeDtypeStruct((b, l, D_MODEL), F32),
        scratch_shapes=[pltpu.VMEM((TM, D_MODEL), F32)],
        compiler_params=_params("parallel", "parallel"),
        name="tail",
    )(x, att, y_s5, mod3, mod3, mod3, mod3, *consts)


def _rope_tables(seq):
    pos = jnp.arange(seq)
    row = (pos // GRID_W).astype(F32)
    col = (pos % GRID_W).astype(F32)
    n_freq = ROPE // 4
    freqs = ROPE_THETA ** (-jnp.arange(n_freq, dtype=F32) / n_freq)
    ang = jnp.concatenate([row[:, None] * freqs, col[:, None] * freqs], axis=-1)
    cos, sin = jnp.cos(ang), jnp.sin(ang)
    zeros = lambda w: jnp.zeros((seq, w), F32)
    tail = LANES - NOPE - ROPE
    c = jnp.concatenate([jnp.ones((seq, NOPE), F32), cos, cos, zeros(tail)], axis=-1)
    sa = jnp.concatenate([zeros(NOPE), -sin, zeros(ROPE_HALF), zeros(tail)], axis=-1)
    sb = jnp.concatenate([zeros(NOPE), zeros(ROPE_HALF), sin, zeros(tail)], axis=-1)
    return c, sa, sb


def _pair_split(w):
    return jnp.concatenate([w[..., 0::2], w[..., 1::2]], axis=-1)


def kernel(x, c, ctx, c_ctx, w_ada, b_ada, w_in, q_norm_g, kv_norm_g, w_uq, w_uk, w_uv, s5_lambda_re, s5_lambda_im, s5_log_dt, s5_b_re, s5_b_im, s5_c_re, s5_c_im, s5_d, s5_w_glu, s5_b_glu, w_out, ln1_g, ln1_b, w_gate_up, w_down, ln2_g, ln2_b):
    assert w_ada.shape[0] == DEPTH == 1
    b, l, _ = x.shape
    lc = ctx.shape[1]
    row2 = lambda t: t.reshape(1, -1)

    cond = jnp.concatenate([c, c_ctx[None, :], jnp.zeros((2 * SUBLANES - b - 1, D_MODEL), F32)], axis=0)
    mod = _adaln(cond, w_ada[0].astype(BF16), row2(b_ada[0]))
    mod3 = mod.reshape(mod.shape[0], 1, 6 * D_MODEL)

    wi = w_in[0]
    rope_tile = jnp.zeros((D_MODEL, LANES), F32).at[:, NOPE:NOPE + ROPE].set(_pair_split(wi[:, KV_END:ROPE_END]))
    w_in_r = jnp.concatenate([wi[:, :KV_END], wi[:, ROPE_END:], rope_tile], axis=1).astype(BF16)
    uq = w_uq[0]
    uq = jnp.concatenate([uq[..., :NOPE], _pair_split(uq[..., NOPE:]),
                          jnp.zeros((Q_RANK, N_HEADS, HEAD_PAD - NOPE - ROPE), F32)], axis=-1)
    wuq = uq.reshape(Q_RANK, N_HEADS * HEAD_PAD).astype(BF16)
    uk = jnp.concatenate([w_uk[0], jnp.zeros((KV_RANK, N_HEADS, HEAD_PAD - NOPE), F32)], axis=-1)
    wuk = uk.reshape(KV_RANK, N_HEADS * HEAD_PAD).astype(BF16)
    wuv = w_uv[0].reshape(KV_RANK, MLA_WIDTH).astype(BF16)
    qg, kvg = row2(q_norm_g[0]), row2(kv_norm_g[0])

    rope_c, rope_sa, rope_sb = _rope_tables(l)
    flat_c = jnp.concatenate([jnp.ones((lc, NOPE + ROPE), F32), jnp.zeros((lc, LANES - NOPE - ROPE), F32)], axis=-1)
    flat_s = jnp.zeros((lc, LANES), F32)

    q, k_l, v_l, u_l = _inproj(x, mod3, None, w_in_r, qg, kvg, wuq, wuk, wuv, rope_c, rope_sa, rope_sb, True)
    k_c, v_c, u_c = _inproj(ctx, mod3, b, w_in_r, qg, kvg, wuq, wuk, wuv, flat_c, flat_s, flat_s, False)

    att = _attention(q, k_c, k_l, v_c, v_l)

    a_re, a_im, bb_re, bb_im = _s5prep(s5_lambda_re[0], s5_lambda_im[0], s5_log_dt[0], s5_b_re[0], s5_b_im[0])
    bmat = jnp.concatenate([_block_diag(bb_re), _block_diag(bb_im)], axis=-1).astype(BF16)
    c_t = lambda t: jnp.swapaxes(t, -1, -2)
    cmat = jnp.concatenate([_block_diag(c_t(s5_c_re[0])), -_block_diag(c_t(s5_c_im[0]))], axis=-2).astype(BF16)
    a_cat = jnp.concatenate([a_re.reshape(2, S5_COLS), a_im.reshape(2, S5_COLS)], axis=-1)
    a_b = jnp.broadcast_to(a_cat[:, None, :], (2, SUBLANES, 2 * S5_COLS))
    dskip = jnp.stack([s5_d[0], jnp.zeros_like(s5_d[0])])[:, None, :]
    u_all = jnp.concatenate([jnp.swapaxes(u_c, 0, 1), jnp.swapaxes(u_l, 0, 1)], axis=0)
    y_dir = _s5scan(u_all, bmat, a_b, cmat, dskip, lc // TC)
    y_s5 = jnp.swapaxes(y_dir[0, lc:] + y_dir[1, lc:], 0, 1)

    return _tail(x, att, y_s5, mod3, s5_w_glu[0].astype(BF16), row2(s5_b_glu[0]), w_out[0].astype(BF16),
                 row2(ln1_g[0]), row2(ln1_b[0]), w_gate_up[0, :, :D_FF].astype(BF16),
                 w_gate_up[0, :, D_FF:].astype(BF16), w_down[0].astype(BF16), row2(ln2_g[0]), row2(ln2_b[0]))
```

```python
import functools
import math

import jax
import jax.numpy as jnp
from jax import lax
from jax.experimental import pallas as pl
from jax.experimental.pallas import tpu as pltpu

D_MODEL = 1024
GRID_W = 64
N_HEADS = 8
NOPE = 64
ROPE = 32
V_DIM = 64
Q_RANK = 384
KV_RANK = 256
MLA_WIDTH = N_HEADS * V_DIM
S5_WIDTH = D_MODEL - MLA_WIDTH
S5_GROUP = 16
S5_GROUPS = S5_WIDTH // S5_GROUP
S5_STATE = 64
S5_COLS = S5_GROUPS * S5_STATE
KV_END = Q_RANK + KV_RANK
ROPE_END = KV_END + ROPE
D_FF = 2816
ROPE_THETA = 10000.0
NORM_EPS = 1e-6
DEPTH = 1
DN_ALPHA = (2.0 * DEPTH) ** 0.25

LANES = 128
SUBLANES = 8
HEAD_PAD = LANES
ROPE_HALF = ROPE // 2
P_COLS = Q_RANK + KV_RANK + S5_WIDTH + LANES
VMEM_LIMIT = 56 * 1024 * 1024

TM = 512
BQ = 512
DEN_ROWS = 16
HEADS_PER_STEP = 4
Q_SCALE = (NOPE + ROPE) ** -0.5 * math.log2(math.e)
TC = 64
N_LANE_TILES = S5_WIDTH // LANES
GROUPS_PER_TILE = LANES // S5_GROUP
TILE_COLS = GROUPS_PER_TILE * S5_STATE
TILES_PER_SLAB = 2
SLAB = TILES_PER_SLAB * TILE_COLS
FF_CHUNK = 256

F32 = jnp.float32
BF16 = jnp.bfloat16


def _params(*sem):
    return pltpu.CompilerParams(dimension_semantics=sem, vmem_limit_bytes=VMEM_LIMIT)


def _const_spec(shape):
    nd = len(shape)
    return pl.BlockSpec(shape, lambda *_: (0,) * nd, pipeline_mode=pl.Buffered(1))


def _layer_norm(x):
    mu = jnp.mean(x, axis=-1, keepdims=True)
    xc = x - mu
    var = jnp.mean(xc * xc, axis=-1, keepdims=True)
    return xc * lax.rsqrt(var + NORM_EPS)


def _rms_norm(x, g):
    return x * lax.rsqrt(jnp.mean(x * x, axis=-1, keepdims=True) + NORM_EPS) * g


def _dot(a, b):
    return jnp.dot(a, b, preferred_element_type=F32)


_NT = (((1,), (1,)), ((), ()))


def _adaln_kernel(cond_ref, w_ref, b_ref, o_ref):
    cnd = cond_ref[...]
    act = cnd * jax.nn.sigmoid(cnd)
    o_ref[...] = _dot(act.astype(BF16), w_ref[...]) + b_ref[...]


def _adaln(cond, w_ada, b_ada):
    rows, n = cond.shape[0], w_ada.shape[1]
    tn = D_MODEL
    return pl.pallas_call(
        _adaln_kernel,
        grid=(n // tn,),
        in_specs=[pl.BlockSpec((rows, D_MODEL), lambda j: (0, 0)),
                  pl.BlockSpec((D_MODEL, tn), lambda j: (0, j)),
                  pl.BlockSpec((1, tn), lambda j: (0, j))],
        out_specs=pl.BlockSpec((rows, tn), lambda j: (0, j)),
        out_shape=jax.ShapeDtypeStruct((rows, n), F32),
        compiler_params=_params("arbitrary"),
        name="adaln",
    )(cond, w_ada, b_ada)


def _s5prep_kernel(lre_ref, lim_ref, ldt_ref, bre_ref, bim_ref, are_ref, aim_ref, bbre_ref, bbim_ref):
    lre, lim = lre_ref[...], lim_ref[...]
    dt = jnp.exp(ldt_ref[...])
    mag = jnp.exp(lre * dt)
    a_re, a_im = mag * jnp.cos(lim * dt), mag * jnp.sin(lim * dt)
    den = lre * lre + lim * lim
    f_re = ((a_re - 1) * lre + a_im * lim) / den
    f_im = (a_im * lre - (a_re - 1) * lim) / den
    b_re, b_im = bre_ref[...], bim_ref[...]
    are_ref[...] = a_re
    aim_ref[...] = a_im
    bbre_ref[...] = f_re * b_re - f_im * b_im
    bbim_ref[...] = f_re * b_im + f_im * b_re


def _s5prep(lam_re, lam_im, log_dt, b_re, b_im):
    rows = 2 * S5_GROUPS * S5_GROUP

    def per_row(t):
        return jnp.broadcast_to(t[:, :, None, :], (2, S5_GROUPS, S5_GROUP, S5_STATE)).reshape(rows, S5_STATE)

    ldt = jnp.broadcast_to(log_dt[:, :, None, None], (2, S5_GROUPS, S5_GROUP, S5_STATE)).reshape(rows, S5_STATE)
    bt_re = jnp.swapaxes(b_re, -1, -2).reshape(rows, S5_STATE)
    bt_im = jnp.swapaxes(b_im, -1, -2).reshape(rows, S5_STATE)
    shp = jax.ShapeDtypeStruct((rows, S5_STATE), F32)
    a_re, a_im, bb_re, bb_im = pl.pallas_call(
        _s5prep_kernel, out_shape=(shp, shp, shp, shp), name="s5prep",
    )(per_row(lam_re), per_row(lam_im), ldt, bt_re, bt_im)
    g4 = (2, S5_GROUPS, S5_GROUP, S5_STATE)
    return (a_re.reshape(g4)[:, :, 0], a_im.reshape(g4)[:, :, 0], bb_re.reshape(g4), bb_im.reshape(g4))


def _tile_block_diag(t):
    t = t.reshape(2, N_LANE_TILES, GROUPS_PER_TILE, t.shape[2], t.shape[3])
    eye = jnp.eye(GROUPS_PER_TILE, dtype=t.dtype)
    out = t[:, :, :, :, None, :] * eye[None, None, :, None, :, None]
    return out.reshape(2, N_LANE_TILES, GROUPS_PER_TILE * t.shape[3], GROUPS_PER_TILE * t.shape[4])


def _rope_tile(t, c, sa, sb):
    return t * c + pltpu.roll(t, LANES - ROPE_HALF, 1) * sa + pltpu.roll(t, ROPE_HALF, 1) * sb


def _inproj_kernel(with_q, x_ref, sh_ref, sc_ref, win_ref, qg_ref, kvg_ref, wuq_ref, wuk_ref, wuv_ref,
                   c_ref, sa_ref, sb_ref, *out_refs):
    if with_q:
        q_ref, k_ref, v_ref, u_ref = out_refs
    else:
        k_ref, v_ref, u_ref = out_refs
    x = x_ref[0]
    xm = _layer_norm(x) * (1.0 + sc_ref[0]) + sh_ref[0]
    p = _dot(xm.astype(BF16), win_ref[...])
    c, sa, sb = c_ref[...], sa_ref[...], sb_ref[...]

    kv_c = _rms_norm(p[:, Q_RANK:KV_END], kvg_ref[...]).astype(BF16)
    k = _dot(kv_c, wuk_ref[...])
    kr = _rope_tile(p[:, KV_END + S5_WIDTH:], c, sa, sb)
    for h in range(N_HEADS):
        k_ref[0, :, h * HEAD_PAD:(h + 1) * HEAD_PAD] = (k[:, h * HEAD_PAD:(h + 1) * HEAD_PAD] + kr).astype(BF16)
    v_ref[0] = lax.dot_general(wuv_ref[...], kv_c, _NT, preferred_element_type=F32).astype(BF16)
    u_ref[0] = p[:, KV_END:KV_END + S5_WIDTH]

    if with_q:
        q_c = _rms_norm(p[:, :Q_RANK], qg_ref[...]).astype(BF16)
        q = _dot(q_c, wuq_ref[...])
        for h in range(N_HEADS):
            sl = slice(h * HEAD_PAD, (h + 1) * HEAD_PAD)
            q_ref[0, :, sl] = (_rope_tile(q[:, sl], c, sa, sb) * Q_SCALE).astype(BF16)


def _inproj(x, mod3, mod_row, w_in_r, qg, kvg, wuq, wuk, wuv, rope_c, rope_sa, rope_sb, with_q):
    b, l, _ = x.shape
    tm = min(TM, l)
    row = (lambda bi: bi) if mod_row is None else (lambda bi: mod_row)
    tok = lambda w: pl.BlockSpec((1, tm, w), lambda bi, i: (bi, i, 0))
    in_specs = [
        tok(D_MODEL),
        pl.BlockSpec((1, 1, D_MODEL), lambda bi, i: (row(bi), 0, 0)),
        pl.BlockSpec((1, 1, D_MODEL), lambda bi, i: (row(bi), 0, 1)),
        _const_spec(w_in_r.shape), _const_spec(qg.shape), _const_spec(kvg.shape),
        _const_spec(wuq.shape), _const_spec(wuk.shape), _const_spec(wuv.shape),
        pl.BlockSpec((tm, LANES), lambda bi, i: (i, 0)),
        pl.BlockSpec((tm, LANES), lambda bi, i: (i, 0)),
        pl.BlockSpec((tm, LANES), lambda bi, i: (i, 0)),
    ]
    out_specs = [tok(N_HEADS * HEAD_PAD), pl.BlockSpec((1, MLA_WIDTH, tm), lambda bi, i: (bi, 0, i)), tok(S5_WIDTH)]
    out_shape = [jax.ShapeDtypeStruct((b, l, N_HEADS * HEAD_PAD), BF16),
                 jax.ShapeDtypeStruct((b, MLA_WIDTH, l), BF16),
                 jax.ShapeDtypeStruct((b, l, S5_WIDTH), F32)]
    if with_q:
        out_specs = [tok(N_HEADS * HEAD_PAD)] + out_specs
        out_shape = [jax.ShapeDtypeStruct((b, l, N_HEADS * HEAD_PAD), BF16)] + out_shape
    return pl.pallas_call(
        functools.partial(_inproj_kernel, with_q),
        grid=(b, l // tm),
        in_specs=in_specs, out_specs=out_specs, out_shape=out_shape,
        compiler_params=_params("parallel", "parallel"),
        name="inproj_lat" if with_q else "inproj_ctx",
    )(x, mod3, mod3, w_in_r, qg, kvg, wuq, wuk, wuv, rope_c, rope_sa, rope_sb)


def _attn_kernel(q_ref, kc_ref, kl_ref, vc_ref, vl_ref, o_ref):
    def scores(h):
        ks = slice(h * HEAD_PAD, (h + 1) * HEAD_PAD)
        q = q_ref[0, :, ks]
        return (lax.dot_general(kc_ref[0, :, ks], q, _NT, preferred_element_type=F32),
                lax.dot_general(kl_ref[0, :, ks], q, _NT, preferred_element_type=F32))

    def with_ones(vt):
        return jnp.concatenate([vt, jnp.ones((DEN_ROWS, vt.shape[1]), BF16)], axis=0)

    outs = []
    s_next = scores(0)
    for h in range(HEADS_PER_STEP):
        s_c, s_l = s_next
        if h + 1 < HEADS_PER_STEP:
            s_next = scores(h + 1)
        vs = slice(h * V_DIM, (h + 1) * V_DIM)
        m = jnp.maximum(jnp.max(s_c, axis=0, keepdims=True), jnp.max(s_l, axis=0, keepdims=True))
        e_c = jnp.exp2(s_c - m).astype(BF16)
        e_l = jnp.exp2(s_l - m).astype(BF16)
        o_t = _dot(with_ones(vc_ref[0, vs, :]), e_c) + _dot(with_ones(vl_ref[0, vs, :]), e_l)
        outs.append(o_t[:V_DIM] / o_t[V_DIM:V_DIM + 1])
    o_ref[0] = jnp.concatenate(outs, axis=0).T.astype(o_ref.dtype)


def _attention(q, k_c, k_l, v_c, v_l):
    b, l, _ = q.shape
    lc = k_c.shape[1]
    kw, vw = HEADS_PER_STEP * HEAD_PAD, HEADS_PER_STEP * V_DIM
    return pl.pallas_call(
        _attn_kernel,
        grid=(b, N_HEADS // HEADS_PER_STEP, l // BQ),
        in_specs=[pl.BlockSpec((1, BQ, kw), lambda bi, hp, i: (bi, i, hp)),
                  pl.BlockSpec((1, lc, kw), lambda bi, hp, i: (bi, 0, hp)),
                  pl.BlockSpec((1, l, kw), lambda bi, hp, i: (bi, 0, hp)),
                  pl.BlockSpec((1, vw, lc), lambda bi, hp, i: (bi, hp, 0)),
                  pl.BlockSpec((1, vw, l), lambda bi, hp, i: (bi, hp, 0))],
        out_specs=pl.BlockSpec((1, BQ, vw), lambda bi, hp, i: (bi, i, hp)),
        out_shape=jax.ShapeDtypeStruct((b, l, MLA_WIDTH), BF16),
        compiler_params=_params("parallel", "parallel", "arbitrary"),
        name="attn",
    )(q, k_c, k_l, v_c, v_l)


def _s5scan_kernel(u_ref, bmat_ref, a_ref, cmat_ref, dskip_ref, y_ref, hs_ref, st_ref):
    d = pl.program_id(0)

    @pl.when(pl.program_id(1) == 0)
    def _():
        st_ref[...] = jnp.zeros_like(st_ref)

    u = u_ref[...].reshape(TC * SUBLANES, S5_WIDTH)
    ub = u.astype(BF16)

    def tile_cols(lt):
        base = (lt // TILES_PER_SLAB) * 2 * SLAB + (lt % TILES_PER_SLAB) * TILE_COLS
        return slice(base, base + TILE_COLS), slice(base + SLAB, base + SLAB + TILE_COLS)

    for lt in range(N_LANE_TILES):
        re, im = tile_cols(lt)
        bu = _dot(ub[:, lt * LANES:(lt + 1) * LANES], bmat_ref[0, lt])
        hs_ref[:, re] = bu[:, :TILE_COLS]
        hs_ref[:, im] = bu[:, TILE_COLS:]

    for cs in range(0, 2 * S5_COLS, 2 * SLAB):
        re = slice(cs, cs + SLAB)
        im = slice(cs + SLAB, cs + 2 * SLAB)
        a_re, a_im = a_ref[0, :, re], a_ref[0, :, im]

        def step(tt, carry, re=re, im=im, a_re=a_re, a_im=a_im):
            h_re, h_im = carry
            t = jnp.where(d == 0, tt, TC - 1 - tt)
            rows = pl.ds(pl.multiple_of(t * SUBLANES, SUBLANES), SUBLANES)
            n_re = a_re * h_re - a_im * h_im + hs_ref[rows, re]
            n_im = a_re * h_im + a_im * h_re + hs_ref[rows, im]
            hs_ref[rows, re] = n_re
            hs_ref[rows, im] = n_im
            return n_re, n_im

        h_re, h_im = lax.fori_loop(0, TC, step, (st_ref[:, re], st_ref[:, im]), unroll=4)
        st_ref[:, re] = h_re
        st_ref[:, im] = h_im

    for lt in range(N_LANE_TILES):
        re, im = tile_cols(lt)
        lanes = slice(lt * LANES, (lt + 1) * LANES)
        y = (_dot(hs_ref[:, re].astype(BF16), cmat_ref[0, lt, :TILE_COLS])
             + _dot(hs_ref[:, im].astype(BF16), cmat_ref[0, lt, TILE_COLS:])
             + u[:, lanes] * dskip_ref[0, :, lanes])
        y_ref[0, :, :, lanes] = y.reshape(TC, SUBLANES, LANES)


def _s5scan(u_all, bmat, a_b, cmat, dskip, n_ctx_chunks):
    lk = u_all.shape[0]
    n = lk // TC

    def blk(d, i):
        rev = jnp.where(i < n_ctx_chunks, n_ctx_chunks - 1 - i, n + n_ctx_chunks - 1 - i)
        return jnp.where(d == 0, i, rev)

    return pl.pallas_call(
        _s5scan_kernel,
        grid=(2, n),
        in_specs=[pl.BlockSpec((TC, SUBLANES, S5_WIDTH), lambda d, i: (blk(d, i), 0, 0)),
                  pl.BlockSpec((1, N_LANE_TILES, LANES, 2 * TILE_COLS), lambda d, i: (d, 0, 0, 0)),
                  pl.BlockSpec((1, SUBLANES, 2 * S5_COLS), lambda d, i: (d, 0, 0)),
                  pl.BlockSpec((1, N_LANE_TILES, 2 * TILE_COLS, LANES), lambda d, i: (d, 0, 0, 0)),
                  pl.BlockSpec((1, 1, S5_WIDTH), lambda d, i: (d, 0, 0))],
        out_specs=pl.BlockSpec((1, TC, SUBLANES, S5_WIDTH), lambda d, i: (d, blk(d, i), 0, 0)),
        out_shape=jax.ShapeDtypeStruct((2, lk, SUBLANES, S5_WIDTH), F32),
        scratch_shapes=[pltpu.VMEM((TC * SUBLANES, 2 * S5_COLS), F32),
                        pltpu.VMEM((SUBLANES, 2 * S5_COLS), F32)],
        compiler_params=_params("arbitrary", "arbitrary"),
        name="s5scan",
    )(u_all, bmat, a_b, cmat, dskip)


def _tail_kernel(x_ref, att_ref, y_ref, g1_ref, sh2_ref, sc2_ref, g2_ref, wglu_ref, bglu_ref, wout_ref,
                 ln1g_ref, ln1b_ref, wg_ref, wu_ref, wd_ref, ln2g_ref, ln2b_ref, o_ref, acc_ref):
    z = jax.nn.gelu(y_ref[0])
    s5o = z * jax.nn.sigmoid(_dot(z.astype(BF16), wglu_ref[...]) + bglu_ref[...])
    mix = _dot(att_ref[0], wout_ref[:MLA_WIDTH, :]) + _dot(s5o.astype(BF16), wout_ref[MLA_WIDTH:, :])
    x1 = _layer_norm(DN_ALPHA * x_ref[0] + g1_ref[0] * mix) * ln1g_ref[...] + ln1b_ref[...]

    xm = (_layer_norm(x1) * (1.0 + sc2_ref[0]) + sh2_ref[0]).astype(BF16)
    for j in range(D_FF // FF_CHUNK):
        cols = slice(j * FF_CHUNK, (j + 1) * FF_CHUNK)
        gate = _dot(xm, wg_ref[:, cols])
        up = _dot(xm, wu_ref[:, cols])
        hidden = (gate * jax.nn.sigmoid(gate) * up).astype(BF16)
        part = _dot(hidden, wd_ref[cols, :])
        if j == 0:
            acc_ref[...] = part
        else:
            acc_ref[...] += part
    o_ref[0] = _layer_norm(DN_ALPHA * x1 + g2_ref[0] * acc_ref[...]) * ln2g_ref[...] + ln2b_ref[...]


def _tail(x, att, y_s5, mod3, w_glu, b_glu, w_out, ln1_g, ln1_b, w_gate, w_up, w_down, ln2_g, ln2_b):
    b, l, _ = x.shape
    tok = lambda w: pl.BlockSpec((1, TM, w), lambda bi, i: (bi, i, 0))
    mod = lambda j: pl.BlockSpec((1, 1, D_MODEL), lambda bi, i: (bi, 0, j))
    consts = (w_glu, b_glu, w_out, ln1_g, ln1_b, w_gate, w_up, w_down, ln2_g, ln2_b)
    return pl.pallas_call(
        _tail_kernel,
        grid=(b, l // TM),
        in_specs=[tok(D_MODEL), tok(MLA_WIDTH), tok(S5_WIDTH), mod(2), mod(3), mod(4), mod(5)]
                 + [_const_spec(w.shape) for w in consts],
        out_specs=tok(D_MODEL),
        out_shape=jax.ShapeDtypeStruct((b, l, D_MODEL), F32),
        scratch_shapes=[pltpu.VMEM((TM, D_MODEL), F32)],
        compiler_params=_params("parallel", "parallel"),
        name="tail",
    )(x, att, y_s5, mod3, mod3, mod3, mod3, *consts)


def _rope_tables(seq):
    pos = jnp.arange(seq)
    row = (pos // GRID_W).astype(F32)
    col = (pos % GRID_W).astype(F32)
    n_freq = ROPE // 4
    freqs = ROPE_THETA ** (-jnp.arange(n_freq, dtype=F32) / n_freq)
    ang = jnp.concatenate([row[:, None] * freqs, col[:, None] * freqs], axis=-1)
    cos, sin = jnp.cos(ang), jnp.sin(ang)
    zeros = lambda w: jnp.zeros((seq, w), F32)
    tail = LANES - NOPE - ROPE
    c = jnp.concatenate([jnp.ones((seq, NOPE), F32), cos, cos, zeros(tail)], axis=-1)
    sa = jnp.concatenate([zeros(NOPE), -sin, zeros(ROPE_HALF), zeros(tail)], axis=-1)
    sb = jnp.concatenate([zeros(NOPE), zeros(ROPE_HALF), sin, zeros(tail)], axis=-1)
    return c, sa, sb


def _pair_split(w):
    return jnp.concatenate([w[..., 0::2], w[..., 1::2]], axis=-1)


def kernel(x, c, ctx, c_ctx, w_ada, b_ada, w_in, q_norm_g, kv_norm_g, w_uq, w_uk, w_uv, s5_lambda_re, s5_lambda_im, s5_log_dt, s5_b_re, s5_b_im, s5_c_re, s5_c_im, s5_d, s5_w_glu, s5_b_glu, w_out, ln1_g, ln1_b, w_gate_up, w_down, ln2_g, ln2_b):
    assert w_ada.shape[0] == DEPTH == 1
    b, l, _ = x.shape
    lc = ctx.shape[1]
    row2 = lambda t: t.reshape(1, -1)

    cond = jnp.concatenate([c, c_ctx[None, :], jnp.zeros((2 * SUBLANES - b - 1, D_MODEL), F32)], axis=0)
    mod = _adaln(cond, w_ada[0].astype(BF16), row2(b_ada[0]))
    mod3 = mod.reshape(mod.shape[0], 1, 6 * D_MODEL)

    wi = w_in[0]
    rope_tile = jnp.zeros((D_MODEL, LANES), F32).at[:, NOPE:NOPE + ROPE].set(_pair_split(wi[:, KV_END:ROPE_END]))
    w_in_r = jnp.concatenate([wi[:, :KV_END], wi[:, ROPE_END:], rope_tile], axis=1).astype(BF16)
    uq = w_uq[0]
    uq = jnp.concatenate([uq[..., :NOPE], _pair_split(uq[..., NOPE:]),
                          jnp.zeros((Q_RANK, N_HEADS, HEAD_PAD - NOPE - ROPE), F32)], axis=-1)
    wuq = uq.reshape(Q_RANK, N_HEADS * HEAD_PAD).astype(BF16)
    uk = jnp.concatenate([w_uk[0], jnp.zeros((KV_RANK, N_HEADS, HEAD_PAD - NOPE), F32)], axis=-1)
    wuk = uk.reshape(KV_RANK, N_HEADS * HEAD_PAD).astype(BF16)
    wuv = w_uv[0].reshape(KV_RANK, MLA_WIDTH).T.astype(BF16)
    qg, kvg = row2(q_norm_g[0]), row2(kv_norm_g[0])

    rope_c, rope_sa, rope_sb = _rope_tables(l)
    flat_c = jnp.concatenate([jnp.ones((lc, NOPE + ROPE), F32), jnp.zeros((lc, LANES - NOPE - ROPE), F32)], axis=-1)
    flat_s = jnp.zeros((lc, LANES), F32)

    q, k_l, v_l, u_l = _inproj(x, mod3, None, w_in_r, qg, kvg, wuq, wuk, wuv, rope_c, rope_sa, rope_sb, True)
    k_c, v_c, u_c = _inproj(ctx, mod3, b, w_in_r, qg, kvg, wuq, wuk, wuv, flat_c, flat_s, flat_s, False)

    att = _attention(q, k_c, k_l, v_c, v_l)

    a_re, a_im, bb_re, bb_im = _s5prep(s5_lambda_re[0], s5_lambda_im[0], s5_log_dt[0], s5_b_re[0], s5_b_im[0])
    bmat = jnp.concatenate([_tile_block_diag(bb_re), _tile_block_diag(bb_im)], axis=-1).astype(BF16)
    c_t = lambda t: jnp.swapaxes(t, -1, -2)
    cmat = jnp.concatenate([_tile_block_diag(c_t(s5_c_re[0])), -_tile_block_diag(c_t(s5_c_im[0]))],
                           axis=-2).astype(BF16)
    slabs = lambda t: t.reshape(2, S5_COLS // SLAB, 1, SLAB)
    a_cat = jnp.concatenate([slabs(a_re), slabs(a_im)], axis=2).reshape(2, 2 * S5_COLS)
    a_b = jnp.broadcast_to(a_cat[:, None, :], (2, SUBLANES, 2 * S5_COLS))
    dskip = jnp.stack([s5_d[0], jnp.zeros_like(s5_d[0])])[:, None, :]
    u_all = jnp.concatenate([jnp.swapaxes(u_c, 0, 1), jnp.swapaxes(u_l, 0, 1)], axis=0)
    y_dir = _s5scan(u_all, bmat, a_b, cmat, dskip, lc // TC)
    y_s5 = jnp.swapaxes(y_dir[0, lc:] + y_dir[1, lc:], 0, 1)

    return _tail(x, att, y_s5, mod3, s5_w_glu[0].astype(BF16), row2(s5_b_glu[0]), w_out[0].astype(BF16),
                 row2(ln1_g[0]), row2(ln1_b[0]), w_gate_up[0, :, :D_FF].astype(BF16),
                 w_gate_up[0, :, D_FF:].astype(BF16), w_down[0].astype(BF16), row2(ln2_g[0]), row2(ln2_b[0]))
```

```python
import functools
import math

import jax
import jax.numpy as jnp
from jax import lax
from jax.experimental import pallas as pl
from jax.experimental.pallas import tpu as pltpu

D_MODEL = 1024
GRID_W = 64
N_HEADS = 8
NOPE = 64
ROPE = 32
V_DIM = 64
Q_RANK = 384
KV_RANK = 256
MLA_WIDTH = N_HEADS * V_DIM
S5_WIDTH = D_MODEL - MLA_WIDTH
S5_GROUP = 16
S5_GROUPS = S5_WIDTH // S5_GROUP
S5_STATE = 64
S5_COLS = S5_GROUPS * S5_STATE
KV_END = Q_RANK + KV_RANK
ROPE_END = KV_END + ROPE
D_FF = 2816
ROPE_THETA = 10000.0
NORM_EPS = 1e-6
DEPTH = 1
DN_ALPHA = (2.0 * DEPTH) ** 0.25

LANES = 128
SUBLANES = 8
HEAD_PAD = LANES
ROPE_HALF = ROPE // 2
P_COLS = Q_RANK + KV_RANK + S5_WIDTH + LANES
VMEM_LIMIT = 56 * 1024 * 1024

TM = 512
BQ = 512
DEN_ROWS = 16
HEADS_PER_STEP = 4
Q_SCALE = (NOPE + ROPE) ** -0.5 * math.log2(math.e)
TC = 64
N_LANE_TILES = S5_WIDTH // LANES
GROUPS_PER_TILE = LANES // S5_GROUP
TILE_COLS = GROUPS_PER_TILE * S5_STATE
TILES_PER_SLAB = 2
SLAB = TILES_PER_SLAB * TILE_COLS
FF_CHUNK = 256

F32 = jnp.float32
BF16 = jnp.bfloat16


def _params(*sem):
    return pltpu.CompilerParams(dimension_semantics=sem, vmem_limit_bytes=VMEM_LIMIT)


def _const_spec(shape):
    nd = len(shape)
    return pl.BlockSpec(shape, lambda *_: (0,) * nd, pipeline_mode=pl.Buffered(1))


def _layer_norm(x):
    mu = jnp.mean(x, axis=-1, keepdims=True)
    xc = x - mu
    var = jnp.mean(xc * xc, axis=-1, keepdims=True)
    return xc * lax.rsqrt(var + NORM_EPS)


def _rms_norm(x, g):
    return x * lax.rsqrt(jnp.mean(x * x, axis=-1, keepdims=True) + NORM_EPS) * g


def _dot(a, b):
    return jnp.dot(a, b, preferred_element_type=F32)


_NT = (((1,), (1,)), ((), ()))


def _adaln_kernel(cond_ref, w_ref, b_ref, o_ref):
    cnd = cond_ref[...]
    act = cnd * jax.nn.sigmoid(cnd)
    o_ref[...] = _dot(act.astype(BF16), w_ref[...]) + b_ref[...]


def _adaln(cond, w_ada, b_ada):
    rows, n = cond.shape[0], w_ada.shape[1]
    tn = D_MODEL
    return pl.pallas_call(
        _adaln_kernel,
        grid=(n // tn,),
        in_specs=[pl.BlockSpec((rows, D_MODEL), lambda j: (0, 0)),
                  pl.BlockSpec((D_MODEL, tn), lambda j: (0, j)),
                  pl.BlockSpec((1, tn), lambda j: (0, j))],
        out_specs=pl.BlockSpec((rows, tn), lambda j: (0, j)),
        out_shape=jax.ShapeDtypeStruct((rows, n), F32),
        compiler_params=_params("arbitrary"),
        name="adaln",
    )(cond, w_ada, b_ada)


def _s5prep_kernel(lre_ref, lim_ref, ldt_ref, bre_ref, bim_ref, are_ref, aim_ref, bbre_ref, bbim_ref):
    lre, lim = lre_ref[...], lim_ref[...]
    dt = jnp.exp(ldt_ref[...])
    mag = jnp.exp(lre * dt)
    a_re, a_im = mag * jnp.cos(lim * dt), mag * jnp.sin(lim * dt)
    den = lre * lre + lim * lim
    f_re = ((a_re - 1) * lre + a_im * lim) / den
    f_im = (a_im * lre - (a_re - 1) * lim) / den
    b_re, b_im = bre_ref[...], bim_ref[...]
    are_ref[...] = a_re
    aim_ref[...] = a_im
    bbre_ref[...] = f_re * b_re - f_im * b_im
    bbim_ref[...] = f_re * b_im + f_im * b_re


def _s5prep(lam_re, lam_im, log_dt, b_re, b_im):
    rows = 2 * S5_GROUPS * S5_GROUP

    def per_row(t):
        return jnp.broadcast_to(t[:, :, None, :], (2, S5_GROUPS, S5_GROUP, S5_STATE)).reshape(rows, S5_STATE)

    ldt = jnp.broadcast_to(log_dt[:, :, None, None], (2, S5_GROUPS, S5_GROUP, S5_STATE)).reshape(rows, S5_STATE)
    bt_re = jnp.swapaxes(b_re, -1, -2).reshape(rows, S5_STATE)
    bt_im = jnp.swapaxes(b_im, -1, -2).reshape(rows, S5_STATE)
    shp = jax.ShapeDtypeStruct((rows, S5_STATE), F32)
    a_re, a_im, bb_re, bb_im = pl.pallas_call(
        _s5prep_kernel, out_shape=(shp, shp, shp, shp), name="s5prep",
    )(per_row(lam_re), per_row(lam_im), ldt, bt_re, bt_im)
    g4 = (2, S5_GROUPS, S5_GROUP, S5_STATE)
    return (a_re.reshape(g4)[:, :, 0], a_im.reshape(g4)[:, :, 0], bb_re.reshape(g4), bb_im.reshape(g4))


def _tile_block_diag(t):
    t = t.reshape(2, N_LANE_TILES, GROUPS_PER_TILE, t.shape[2], t.shape[3])
    eye = jnp.eye(GROUPS_PER_TILE, dtype=t.dtype)
    out = t[:, :, :, :, None, :] * eye[None, None, :, None, :, None]
    return out.reshape(2, N_LANE_TILES, GROUPS_PER_TILE * t.shape[3], GROUPS_PER_TILE * t.shape[4])


def _rope_tile(t, c, sa, sb):
    return t * c + pltpu.roll(t, LANES - ROPE_HALF, 1) * sa + pltpu.roll(t, ROPE_HALF, 1) * sb


def _inproj_kernel(with_q, x_ref, sh_ref, sc_ref, win_ref, qg_ref, kvg_ref, wuq_ref, wuk_ref, wuv_ref,
                   c_ref, sa_ref, sb_ref, *out_refs):
    if with_q:
        q_ref, k_ref, v_ref, u_ref = out_refs
    else:
        k_ref, v_ref, u_ref = out_refs
    x = x_ref[0]
    xm = _layer_norm(x) * (1.0 + sc_ref[0]) + sh_ref[0]
    p = _dot(xm.astype(BF16), win_ref[...])
    c, sa, sb = c_ref[...], sa_ref[...], sb_ref[...]

    kv_c = _rms_norm(p[:, Q_RANK:KV_END], kvg_ref[...]).astype(BF16)
    k = _dot(kv_c, wuk_ref[...])
    kr = _rope_tile(p[:, KV_END + S5_WIDTH:], c, sa, sb)
    for h in range(N_HEADS):
        k_ref[0, :, h * HEAD_PAD:(h + 1) * HEAD_PAD] = (k[:, h * HEAD_PAD:(h + 1) * HEAD_PAD] + kr).astype(BF16)
    v_ref[0] = lax.dot_general(wuv_ref[...], kv_c, _NT, preferred_element_type=F32).astype(BF16)
    u_ref[0] = p[:, KV_END:KV_END + S5_WIDTH]

    if with_q:
        q_c = _rms_norm(p[:, :Q_RANK], qg_ref[...]).astype(BF16)
        q = _dot(q_c, wuq_ref[...])
        for h in range(N_HEADS):
            sl = slice(h * HEAD_PAD, (h + 1) * HEAD_PAD)
            q_ref[0, :, sl] = (_rope_tile(q[:, sl], c, sa, sb) * Q_SCALE).astype(BF16)


def _inproj(x, mod3, mod_row, w_in_r, qg, kvg, wuq, wuk, wuv, rope_c, rope_sa, rope_sb, with_q):
    b, l, _ = x.shape
    tm = min(TM, l)
    row = (lambda bi: bi) if mod_row is None else (lambda bi: mod_row)
    tok = lambda w: pl.BlockSpec((1, tm, w), lambda bi, i: (bi, i, 0))
    in_specs = [
        tok(D_MODEL),
        pl.BlockSpec((1, 1, D_MODEL), lambda bi, i: (row(bi), 0, 0)),
        pl.BlockSpec((1, 1, D_MODEL), lambda bi, i: (row(bi), 0, 1)),
        _const_spec(w_in_r.shape), _const_spec(qg.shape), _const_spec(kvg.shape),
        _const_spec(wuq.shape), _const_spec(wuk.shape), _const_spec(wuv.shape),
        pl.BlockSpec((tm, LANES), lambda bi, i: (i, 0)),
        pl.BlockSpec((tm, LANES), lambda bi, i: (i, 0)),
        pl.BlockSpec((tm, LANES), lambda bi, i: (i, 0)),
    ]
    out_specs = [tok(N_HEADS * HEAD_PAD), pl.BlockSpec((1, MLA_WIDTH, tm), lambda bi, i: (bi, 0, i)), tok(S5_WIDTH)]
    out_shape = [jax.ShapeDtypeStruct((b, l, N_HEADS * HEAD_PAD), BF16),
                 jax.ShapeDtypeStruct((b, MLA_WIDTH, l), BF16),
                 jax.ShapeDtypeStruct((b, l, S5_WIDTH), F32)]
    if with_q:
        out_specs = [tok(N_HEADS * HEAD_PAD)] + out_specs
        out_shape = [jax.ShapeDtypeStruct((b, l, N_HEADS * HEAD_PAD), BF16)] + out_shape
    return pl.pallas_call(
        functools.partial(_inproj_kernel, with_q),
        grid=(b, l // tm),
        in_specs=in_specs, out_specs=out_specs, out_shape=out_shape,
        compiler_params=_params("parallel", "parallel"),
        name="inproj_lat" if with_q else "inproj_ctx",
    )(x, mod3, mod3, w_in_r, qg, kvg, wuq, wuk, wuv, rope_c, rope_sa, rope_sb)


def _attn_kernel(q_ref, kc_ref, kl_ref, vc_ref, vl_ref, o_ref):
    def scores(h):
        ks = slice(h * HEAD_PAD, (h + 1) * HEAD_PAD)
        q = q_ref[0, :, ks]
        return (lax.dot_general(kc_ref[0, :, ks], q, _NT, preferred_element_type=F32),
                lax.dot_general(kl_ref[0, :, ks], q, _NT, preferred_element_type=F32))

    def with_ones(vt):
        return jnp.concatenate([vt, jnp.ones((DEN_ROWS, vt.shape[1]), BF16)], axis=0)

    outs = []
    s_next = scores(0)
    for h in range(HEADS_PER_STEP):
        s_c, s_l = s_next
        if h + 1 < HEADS_PER_STEP:
            s_next = scores(h + 1)
        vs = slice(h * V_DIM, (h + 1) * V_DIM)
        m = jnp.maximum(jnp.max(s_c, axis=0, keepdims=True), jnp.max(s_l, axis=0, keepdims=True))
        e_c = jnp.exp2(s_c - m).astype(BF16)
        e_l = jnp.exp2(s_l - m).astype(BF16)
        o_t = _dot(with_ones(vc_ref[0, vs, :]), e_c) + _dot(with_ones(vl_ref[0, vs, :]), e_l)
        outs.append(o_t[:V_DIM] / o_t[V_DIM:V_DIM + 1])
    o_ref[0] = jnp.concatenate(outs, axis=0).T.astype(o_ref.dtype)


def _attention(q, k_c, k_l, v_c, v_l):
    b, l, _ = q.shape
    lc = k_c.shape[1]
    kw, vw = HEADS_PER_STEP * HEAD_PAD, HEADS_PER_STEP * V_DIM
    return pl.pallas_call(
        _attn_kernel,
        grid=(b, N_HEADS // HEADS_PER_STEP, l // BQ),
        in_specs=[pl.BlockSpec((1, BQ, kw), lambda bi, hp, i: (bi, i, hp)),
                  pl.BlockSpec((1, lc, kw), lambda bi, hp, i: (bi, 0, hp)),
                  pl.BlockSpec((1, l, kw), lambda bi, hp, i: (bi, 0, hp)),
                  pl.BlockSpec((1, vw, lc), lambda bi, hp, i: (bi, hp, 0)),
                  pl.BlockSpec((1, vw, l), lambda bi, hp, i: (bi, hp, 0))],
        out_specs=pl.BlockSpec((1, BQ, vw), lambda bi, hp, i: (bi, i, hp)),
        out_shape=jax.ShapeDtypeStruct((b, l, MLA_WIDTH), BF16),
        compiler_params=_params("parallel", "parallel", "arbitrary"),
        name="attn",
    )(q, k_c, k_l, v_c, v_l)


def _s5scan_kernel(n_ctx_chunks, uc_ref, ul_ref, bmat_ref, a_ref, cmat_ref, dskip_ref, y_ref, hs_ref, st_ref):
    d = pl.program_id(0)
    i = pl.program_id(1)

    @pl.when(i == 0)
    def _():
        st_ref[...] = jnp.zeros_like(st_ref)

    u_bt = jnp.where(i < n_ctx_chunks, uc_ref[...], ul_ref[...])
    u = pltpu.einshape("btc->tbc", u_bt).reshape(TC * SUBLANES, S5_WIDTH)
    ub = u.astype(BF16)

    def tile_cols(lt):
        base = (lt // TILES_PER_SLAB) * 2 * SLAB + (lt % TILES_PER_SLAB) * TILE_COLS
        return slice(base, base + TILE_COLS), slice(base + SLAB, base + SLAB + TILE_COLS)

    for lt in range(N_LANE_TILES):
        re, im = tile_cols(lt)
        bu = _dot(ub[:, lt * LANES:(lt + 1) * LANES], bmat_ref[0, lt])
        hs_ref[:, re] = bu[:, :TILE_COLS]
        hs_ref[:, im] = bu[:, TILE_COLS:]

    for cs in range(0, 2 * S5_COLS, 2 * SLAB):
        re = slice(cs, cs + SLAB)
        im = slice(cs + SLAB, cs + 2 * SLAB)
        a_re, a_im = a_ref[0, :, re], a_ref[0, :, im]

        def step(tt, carry, re=re, im=im, a_re=a_re, a_im=a_im):
            h_re, h_im = carry
            t = jnp.where(d == 0, tt, TC - 1 - tt)
            rows = pl.ds(pl.multiple_of(t * SUBLANES, SUBLANES), SUBLANES)
            n_re = a_re * h_re - a_im * h_im + hs_ref[rows, re]
            n_im = a_re * h_im + a_im * h_re + hs_ref[rows, im]
            hs_ref[rows, re] = n_re
            hs_ref[rows, im] = n_im
            return n_re, n_im

        h_re, h_im = lax.fori_loop(0, TC, step, (st_ref[:, re], st_ref[:, im]), unroll=4)
        st_ref[:, re] = h_re
        st_ref[:, im] = h_im

    for lt in range(N_LANE_TILES):
        re, im = tile_cols(lt)
        lanes = slice(lt * LANES, (lt + 1) * LANES)
        y = (_dot(hs_ref[:, re].astype(BF16), cmat_ref[0, lt, :TILE_COLS])
             + _dot(hs_ref[:, im].astype(BF16), cmat_ref[0, lt, TILE_COLS:])
             + u[:, lanes] * dskip_ref[0, :, lanes])
        y_ref[0, :, :, lanes] = pltpu.einshape("tbc->btc", y.reshape(TC, SUBLANES, LANES))


def _s5scan(u_c, u_l, bmat, a_b, cmat, dskip):
    b, l, _ = u_l.shape
    nc, nl = u_c.shape[1] // TC, l // TC

    def ctx_blk(d, i):
        return jnp.clip(jnp.where(d == 0, i, nc - 1 - i), 0, nc - 1)

    def lat_blk(d, i):
        return jnp.clip(jnp.where(d == 0, i - nc, nl - 1 - (i - nc)), 0, nl - 1)

    def out_blk(d, i):
        return jnp.where(i < nc, nl + ctx_blk(d, i), lat_blk(d, i))

    return pl.pallas_call(
        functools.partial(_s5scan_kernel, nc),
        grid=(2, nc + nl),
        in_specs=[pl.BlockSpec((b, TC, S5_WIDTH), lambda d, i: (0, ctx_blk(d, i), 0)),
                  pl.BlockSpec((b, TC, S5_WIDTH), lambda d, i: (0, lat_blk(d, i), 0)),
                  pl.BlockSpec((1, N_LANE_TILES, LANES, 2 * TILE_COLS), lambda d, i: (d, 0, 0, 0)),
                  pl.BlockSpec((1, SUBLANES, 2 * S5_COLS), lambda d, i: (d, 0, 0)),
                  pl.BlockSpec((1, N_LANE_TILES, 2 * TILE_COLS, LANES), lambda d, i: (d, 0, 0, 0)),
                  pl.BlockSpec((1, 1, S5_WIDTH), lambda d, i: (d, 0, 0))],
        out_specs=pl.BlockSpec((1, b, TC, S5_WIDTH), lambda d, i: (d, 0, out_blk(d, i), 0)),
        out_shape=jax.ShapeDtypeStruct((2, b, (nc + nl) * TC, S5_WIDTH), F32),
        scratch_shapes=[pltpu.VMEM((TC * SUBLANES, 2 * S5_COLS), F32),
                        pltpu.VMEM((SUBLANES, 2 * S5_COLS), F32)],
        compiler_params=_params("arbitrary", "arbitrary"),
        name="s5scan",
    )(u_c, u_l, bmat, a_b, cmat, dskip)


def _tail_kernel(x_ref, att_ref, yf_ref, yb_ref, g1_ref, sh2_ref, sc2_ref, g2_ref, wglu_ref, bglu_ref, wout_ref,
                 ln1g_ref, ln1b_ref, wg_ref, wu_ref, wd_ref, ln2g_ref, ln2b_ref, o_ref, acc_ref):
    z = jax.nn.gelu(yf_ref[0, 0] + yb_ref[0, 0])
    s5o = z * jax.nn.sigmoid(_dot(z.astype(BF16), wglu_ref[...]) + bglu_ref[...])
    mix = _dot(att_ref[0], wout_ref[:MLA_WIDTH, :]) + _dot(s5o.astype(BF16), wout_ref[MLA_WIDTH:, :])
    x1 = _layer_norm(DN_ALPHA * x_ref[0] + g1_ref[0] * mix) * ln1g_ref[...] + ln1b_ref[...]

    xm = (_layer_norm(x1) * (1.0 + sc2_ref[0]) + sh2_ref[0]).astype(BF16)
    for j in range(D_FF // FF_CHUNK):
        cols = slice(j * FF_CHUNK, (j + 1) * FF_CHUNK)
        gate = _dot(xm, wg_ref[:, cols])
        up = _dot(xm, wu_ref[:, cols])
        hidden = (gate * jax.nn.sigmoid(gate) * up).astype(BF16)
        part = _dot(hidden, wd_ref[cols, :])
        if j == 0:
            acc_ref[...] = part
        else:
            acc_ref[...] += part
    o_ref[0] = _layer_norm(DN_ALPHA * x1 + g2_ref[0] * acc_ref[...]) * ln2g_ref[...] + ln2b_ref[...]


def _tail(x, att, y_dir, mod3, w_glu, b_glu, w_out, ln1_g, ln1_b, w_gate, w_up, w_down, ln2_g, ln2_b):
    b, l, _ = x.shape
    tok = lambda w: pl.BlockSpec((1, TM, w), lambda bi, i: (bi, i, 0))
    mod = lambda j: pl.BlockSpec((1, 1, D_MODEL), lambda bi, i: (bi, 0, j))
    y_of = lambda d: pl.BlockSpec((1, 1, TM, S5_WIDTH), lambda bi, i: (d, bi, i, 0))
    consts = (w_glu, b_glu, w_out, ln1_g, ln1_b, w_gate, w_up, w_down, ln2_g, ln2_b)
    return pl.pallas_call(
        _tail_kernel,
        grid=(b, l // TM),
        in_specs=[tok(D_MODEL), tok(MLA_WIDTH), y_of(0), y_of(1), mod(2), mod(3), mod(4), mod(5)]
                 + [_const_spec(w.shape) for w in consts],
        out_specs=tok(D_MODEL),
        out_shape=jax.ShapeDtypeStruct((b, l, D_MODEL), F32),
        scratch_shapes=[pltpu.VMEM((TM, D_MODEL), F32)],
        compiler_params=_params("parallel", "parallel"),
        name="tail",
    )(x, att, y_dir, y_dir, mod3, mod3, mod3, mod3, *consts)


def _rope_tables(seq):
    pos = jnp.arange(seq)
    row = (pos // GRID_W).astype(F32)
    col = (pos % GRID_W).astype(F32)
    n_freq = ROPE // 4
    freqs = ROPE_THETA ** (-jnp.arange(n_freq, dtype=F32) / n_freq)
    ang = jnp.concatenate([row[:, None] * freqs, col[:, None] * freqs], axis=-1)
    cos, sin = jnp.cos(ang), jnp.sin(ang)
    zeros = lambda w: jnp.zeros((seq, w), F32)
    tail = LANES - NOPE - ROPE
    c = jnp.concatenate([jnp.ones((seq, NOPE), F32), cos, cos, zeros(tail)], axis=-1)
    sa = jnp.concatenate([zeros(NOPE), -sin, zeros(ROPE_HALF), zeros(tail)], axis=-1)
    sb = jnp.concatenate([zeros(NOPE), zeros(ROPE_HALF), sin, zeros(tail)], axis=-1)
    return c, sa, sb


def _pair_split(w):
    return jnp.concatenate([w[..., 0::2], w[..., 1::2]], axis=-1)


def kernel(x, c, ctx, c_ctx, w_ada, b_ada, w_in, q_norm_g, kv_norm_g, w_uq, w_uk, w_uv, s5_lambda_re, s5_lambda_im, s5_log_dt, s5_b_re, s5_b_im, s5_c_re, s5_c_im, s5_d, s5_w_glu, s5_b_glu, w_out, ln1_g, ln1_b, w_gate_up, w_down, ln2_g, ln2_b):
    assert w_ada.shape[0] == DEPTH == 1
    b, l, _ = x.shape
    assert b == SUBLANES, "the S5 scan maps the batch onto the sublanes of one vreg row"
    lc = ctx.shape[1]
    row2 = lambda t: t.reshape(1, -1)

    cond = jnp.concatenate([c, c_ctx[None, :], jnp.zeros((2 * SUBLANES - b - 1, D_MODEL), F32)], axis=0)
    mod = _adaln(cond, w_ada[0].astype(BF16), row2(b_ada[0]))
    mod3 = mod.reshape(mod.shape[0], 1, 6 * D_MODEL)

    wi = w_in[0]
    rope_tile = jnp.zeros((D_MODEL, LANES), F32).at[:, NOPE:NOPE + ROPE].set(_pair_split(wi[:, KV_END:ROPE_END]))
    w_in_r = jnp.concatenate([wi[:, :KV_END], wi[:, ROPE_END:], rope_tile], axis=1).astype(BF16)
    uq = w_uq[0]
    uq = jnp.concatenate([uq[..., :NOPE], _pair_split(uq[..., NOPE:]),
                          jnp.zeros((Q_RANK, N_HEADS, HEAD_PAD - NOPE - ROPE), F32)], axis=-1)
    wuq = uq.reshape(Q_RANK, N_HEADS * HEAD_PAD).astype(BF16)
    uk = jnp.concatenate([w_uk[0], jnp.zeros((KV_RANK, N_HEADS, HEAD_PAD - NOPE), F32)], axis=-1)
    wuk = uk.reshape(KV_RANK, N_HEADS * HEAD_PAD).astype(BF16)
    wuv = w_uv[0].reshape(KV_RANK, MLA_WIDTH).T.astype(BF16)
    qg, kvg = row2(q_norm_g[0]), row2(kv_norm_g[0])

    rope_c, rope_sa, rope_sb = _rope_tables(l)
    flat_c = jnp.concatenate([jnp.ones((lc, NOPE + ROPE), F32), jnp.zeros((lc, LANES - NOPE - ROPE), F32)], axis=-1)
    flat_s = jnp.zeros((lc, LANES), F32)

    q, k_l, v_l, u_l = _inproj(x, mod3, None, w_in_r, qg, kvg, wuq, wuk, wuv, rope_c, rope_sa, rope_sb, True)
    k_c, v_c, u_c = _inproj(ctx, mod3, b, w_in_r, qg, kvg, wuq, wuk, wuv, flat_c, flat_s, flat_s, False)

    att = _attention(q, k_c, k_l, v_c, v_l)

    a_re, a_im, bb_re, bb_im = _s5prep(s5_lambda_re[0], s5_lambda_im[0], s5_log_dt[0], s5_b_re[0], s5_b_im[0])
    bmat = jnp.concatenate([_tile_block_diag(bb_re), _tile_block_diag(bb_im)], axis=-1).astype(BF16)
    c_t = lambda t: jnp.swapaxes(t, -1, -2)
    cmat = jnp.concatenate([_tile_block_diag(c_t(s5_c_re[0])), -_tile_block_diag(c_t(s5_c_im[0]))],
                           axis=-2).astype(BF16)
    slabs = lambda t: t.reshape(2, S5_COLS // SLAB, 1, SLAB)
    a_cat = jnp.concatenate([slabs(a_re), slabs(a_im)], axis=2).reshape(2, 2 * S5_COLS)
    a_b = jnp.broadcast_to(a_cat[:, None, :], (2, SUBLANES, 2 * S5_COLS))
    dskip = jnp.stack([s5_d[0], jnp.zeros_like(s5_d[0])])[:, None, :]
    y_dir = _s5scan(u_c, u_l, bmat, a_b, cmat, dskip)

    return _tail(x, att, y_dir, mod3, s5_w_glu[0].astype(BF16), row2(s5_b_glu[0]), w_out[0].astype(BF16),
                 row2(ln1_g[0]), row2(ln1_b[0]), w_gate_up[0, :, :D_FF].astype(BF16),
                 w_gate_up[0, :, D_FF:].astype(BF16), w_down[0].astype(BF16), row2(ln2_g[0]), row2(ln2_b[0]))
```

```python
import functools
import math

import jax
import jax.numpy as jnp
from jax import lax
from jax.experimental import pallas as pl
from jax.experimental.pallas import tpu as pltpu

D_MODEL = 1024
GRID_W = 64
N_HEADS = 8
NOPE = 64
ROPE = 32
V_DIM = 64
Q_RANK = 384
KV_RANK = 256
MLA_WIDTH = N_HEADS * V_DIM
S5_WIDTH = D_MODEL - MLA_WIDTH
S5_GROUP = 16
S5_GROUPS = S5_WIDTH // S5_GROUP
S5_STATE = 64
S5_COLS = S5_GROUPS * S5_STATE
KV_END = Q_RANK + KV_RANK
ROPE_END = KV_END + ROPE
D_FF = 2816
ROPE_THETA = 10000.0
NORM_EPS = 1e-6
DEPTH = 1
DN_ALPHA = (2.0 * DEPTH) ** 0.25

LANES = 128
SUBLANES = 8
HEAD_PAD = LANES
ROPE_HALF = ROPE // 2
P_COLS = Q_RANK + KV_RANK + S5_WIDTH + LANES
VMEM_LIMIT = 56 * 1024 * 1024

TM = 512
BQ = 512
DEN_ROWS = 16
HEADS_PER_STEP = 8
Q_SCALE = (NOPE + ROPE) ** -0.5 * math.log2(math.e)
TC = 128
N_LANE_TILES = S5_WIDTH // LANES
GROUPS_PER_TILE = LANES // S5_GROUP
TILE_COLS = GROUPS_PER_TILE * S5_STATE
TILES_PER_SLAB = 2
SLAB = TILES_PER_SLAB * TILE_COLS
FF_CHUNK = 256

F32 = jnp.float32
BF16 = jnp.bfloat16


def _params(*sem):
    return pltpu.CompilerParams(dimension_semantics=sem, vmem_limit_bytes=VMEM_LIMIT)


def _const_spec(shape):
    nd = len(shape)
    return pl.BlockSpec(shape, lambda *_: (0,) * nd, pipeline_mode=pl.Buffered(1))


def _layer_norm(x):
    mu = jnp.mean(x, axis=-1, keepdims=True)
    xc = x - mu
    var = jnp.mean(xc * xc, axis=-1, keepdims=True)
    return xc * lax.rsqrt(var + NORM_EPS)


def _rms_norm(x, g):
    return x * lax.rsqrt(jnp.mean(x * x, axis=-1, keepdims=True) + NORM_EPS) * g


def _dot(a, b):
    return jnp.dot(a, b, preferred_element_type=F32)


_NT = (((1,), (1,)), ((), ()))


def _adaln_kernel(cond_ref, w_ref, b_ref, o_ref):
    cnd = cond_ref[...]
    act = cnd * jax.nn.sigmoid(cnd)
    o_ref[...] = _dot(act.astype(BF16), w_ref[...].astype(BF16)) + b_ref[...]


def _adaln(cond, w_ada, b_ada):
    rows, n = cond.shape[0], w_ada.shape[1]
    tn = D_MODEL
    return pl.pallas_call(
        _adaln_kernel,
        grid=(n // tn,),
        in_specs=[pl.BlockSpec((rows, D_MODEL), lambda j: (0, 0)),
                  pl.BlockSpec((D_MODEL, tn), lambda j: (0, j)),
                  pl.BlockSpec((1, tn), lambda j: (0, j))],
        out_specs=pl.BlockSpec((rows, tn), lambda j: (0, j)),
        out_shape=jax.ShapeDtypeStruct((rows, n), F32),
        compiler_params=_params("arbitrary"),
        name="adaln",
    )(cond, w_ada, b_ada)


def _s5prep_kernel(lre_ref, lim_ref, ldt_ref, bre_ref, bim_ref, are_ref, aim_ref, bbre_ref, bbim_ref):
    lre, lim = lre_ref[...], lim_ref[...]
    dt = jnp.exp(ldt_ref[...])
    mag = jnp.exp(lre * dt)
    a_re, a_im = mag * jnp.cos(lim * dt), mag * jnp.sin(lim * dt)
    den = lre * lre + lim * lim
    f_re = ((a_re - 1) * lre + a_im * lim) / den
    f_im = (a_im * lre - (a_re - 1) * lim) / den
    b_re, b_im = bre_ref[...], bim_ref[...]
    are_ref[...] = a_re
    aim_ref[...] = a_im
    bbre_ref[...] = f_re * b_re - f_im * b_im
    bbim_ref[...] = f_re * b_im + f_im * b_re


def _s5prep(lam_re, lam_im, log_dt, b_re, b_im):
    rows = 2 * S5_GROUPS * S5_GROUP

    def per_row(t):
        return jnp.broadcast_to(t[:, :, None, :], (2, S5_GROUPS, S5_GROUP, S5_STATE)).reshape(rows, S5_STATE)

    ldt = jnp.broadcast_to(log_dt[:, :, None, None], (2, S5_GROUPS, S5_GROUP, S5_STATE)).reshape(rows, S5_STATE)
    bt_re = jnp.swapaxes(b_re, -1, -2).reshape(rows, S5_STATE)
    bt_im = jnp.swapaxes(b_im, -1, -2).reshape(rows, S5_STATE)
    shp = jax.ShapeDtypeStruct((rows, S5_STATE), F32)
    a_re, a_im, bb_re, bb_im = pl.pallas_call(
        _s5prep_kernel, out_shape=(shp, shp, shp, shp), name="s5prep",
    )(per_row(lam_re), per_row(lam_im), ldt, bt_re, bt_im)
    g4 = (2, S5_GROUPS, S5_GROUP, S5_STATE)
    return (a_re.reshape(g4)[:, :, 0], a_im.reshape(g4)[:, :, 0], bb_re.reshape(g4), bb_im.reshape(g4))


def _tile_block_diag(t):
    t = t.reshape(2, N_LANE_TILES, GROUPS_PER_TILE, t.shape[2], t.shape[3])
    eye = jnp.eye(GROUPS_PER_TILE, dtype=t.dtype)
    out = t[:, :, :, :, None, :] * eye[None, None, :, None, :, None]
    return out.reshape(2, N_LANE_TILES, GROUPS_PER_TILE * t.shape[3], GROUPS_PER_TILE * t.shape[4])


def _rope_tile(t, c, sa, sb):
    return t * c + pltpu.roll(t, LANES - ROPE_HALF, 1) * sa + pltpu.roll(t, ROPE_HALF, 1) * sb


def _inproj_kernel(with_q, x_ref, sh_ref, sc_ref, win_ref, qg_ref, kvg_ref, wuq_ref, wuk_ref, wuv_ref,
                   c_ref, sa_ref, sb_ref, *out_refs):
    if with_q:
        q_ref, k_ref, v_ref, u_ref = out_refs
    else:
        k_ref, v_ref, u_ref = out_refs
    x = x_ref[0]
    xm = _layer_norm(x) * (1.0 + sc_ref[0]) + sh_ref[0]
    p = _dot(xm.astype(BF16), win_ref[...])
    c, sa, sb = c_ref[...], sa_ref[...], sb_ref[...]

    kv_c = _rms_norm(p[:, Q_RANK:KV_END], kvg_ref[...]).astype(BF16)
    k = _dot(kv_c, wuk_ref[...])
    kr = _rope_tile(p[:, KV_END + S5_WIDTH:], c, sa, sb)
    for h in range(N_HEADS):
        k_ref[0, :, h * HEAD_PAD:(h + 1) * HEAD_PAD] = (k[:, h * HEAD_PAD:(h + 1) * HEAD_PAD] + kr).astype(BF16)
    v_ref[0] = lax.dot_general(wuv_ref[...], kv_c, _NT, preferred_element_type=F32).astype(BF16)
    u_ref[0] = p[:, KV_END:KV_END + S5_WIDTH]

    if with_q:
        q_c = _rms_norm(p[:, :Q_RANK], qg_ref[...]).astype(BF16)
        q = _dot(q_c, wuq_ref[...])
        cq, saq, sbq = c * Q_SCALE, sa * Q_SCALE, sb * Q_SCALE
        for h in range(N_HEADS):
            sl = slice(h * HEAD_PAD, (h + 1) * HEAD_PAD)
            q_ref[0, :, sl] = _rope_tile(q[:, sl], cq, saq, sbq).astype(BF16)


def _inproj(x, mod3, mod_row, w_in_r, qg, kvg, wuq, wuk, wuv, rope_c, rope_sa, rope_sb, with_q):
    b, l, _ = x.shape
    tm = min(TM, l)
    row = (lambda bi: bi) if mod_row is None else (lambda bi: mod_row)
    tok = lambda w: pl.BlockSpec((1, tm, w), lambda bi, i: (bi, i, 0))
    in_specs = [
        tok(D_MODEL),
        pl.BlockSpec((1, 1, D_MODEL), lambda bi, i: (row(bi), 0, 0)),
        pl.BlockSpec((1, 1, D_MODEL), lambda bi, i: (row(bi), 0, 1)),
        _const_spec(w_in_r.shape), _const_spec(qg.shape), _const_spec(kvg.shape),
        _const_spec(wuq.shape), _const_spec(wuk.shape), _const_spec(wuv.shape),
        pl.BlockSpec((tm, LANES), lambda bi, i: (i, 0)),
        pl.BlockSpec((tm, LANES), lambda bi, i: (i, 0)),
        pl.BlockSpec((tm, LANES), lambda bi, i: (i, 0)),
    ]
    out_specs = [tok(N_HEADS * HEAD_PAD), pl.BlockSpec((1, MLA_WIDTH, tm), lambda bi, i: (bi, 0, i)), tok(S5_WIDTH)]
    out_shape = [jax.ShapeDtypeStruct((b, l, N_HEADS * HEAD_PAD), BF16),
                 jax.ShapeDtypeStruct((b, MLA_WIDTH, l), BF16),
                 jax.ShapeDtypeStruct((b, l, S5_WIDTH), F32)]
    if with_q:
        out_specs = [tok(N_HEADS * HEAD_PAD)] + out_specs
        out_shape = [jax.ShapeDtypeStruct((b, l, N_HEADS * HEAD_PAD), BF16)] + out_shape
    return pl.pallas_call(
        functools.partial(_inproj_kernel, with_q),
        grid=(b, l // tm),
        in_specs=in_specs, out_specs=out_specs, out_shape=out_shape,
        compiler_params=_params("parallel", "parallel"),
        name="inproj_lat" if with_q else "inproj_ctx",
    )(x, mod3, mod3, w_in_r, qg, kvg, wuq, wuk, wuv, rope_c, rope_sa, rope_sb)


def _attn_kernel(q_ref, kc_ref, kl_ref, vc_ref, vl_ref, o_ref):
    def scores(h):
        ks = slice(h * HEAD_PAD, (h + 1) * HEAD_PAD)
        q = q_ref[0, :, ks]
        return (lax.dot_general(kc_ref[0, :, ks], q, _NT, preferred_element_type=F32),
                lax.dot_general(kl_ref[0, :, ks], q, _NT, preferred_element_type=F32))

    def with_ones(vt):
        return jnp.concatenate([vt, jnp.ones((DEN_ROWS, vt.shape[1]), BF16)], axis=0)

    outs = []
    s_next = scores(0)
    for h in range(HEADS_PER_STEP):
        s_c, s_l = s_next
        if h + 1 < HEADS_PER_STEP:
            s_next = scores(h + 1)
        vs = slice(h * V_DIM, (h + 1) * V_DIM)
        m = jnp.maximum(jnp.max(s_c, axis=0, keepdims=True), jnp.max(s_l, axis=0, keepdims=True))
        e_c = jnp.exp2(s_c - m).astype(BF16)
        e_l = jnp.exp2(s_l - m).astype(BF16)
        o_t = _dot(with_ones(vc_ref[0, vs, :]), e_c) + _dot(with_ones(vl_ref[0, vs, :]), e_l)
        outs.append(o_t[:V_DIM] / o_t[V_DIM:V_DIM + 1])
    o_ref[0] = jnp.concatenate(outs, axis=0).T.astype(o_ref.dtype)


def _attention(q, k_c, k_l, v_c, v_l):
    b, l, _ = q.shape
    lc = k_c.shape[1]
    kw, vw = HEADS_PER_STEP * HEAD_PAD, HEADS_PER_STEP * V_DIM
    return pl.pallas_call(
        _attn_kernel,
        grid=(b, N_HEADS // HEADS_PER_STEP, l // BQ),
        in_specs=[pl.BlockSpec((1, BQ, kw), lambda bi, hp, i: (bi, i, hp)),
                  pl.BlockSpec((1, lc, kw), lambda bi, hp, i: (bi, 0, hp), pipeline_mode=pl.Buffered(1)),
                  pl.BlockSpec((1, l, kw), lambda bi, hp, i: (bi, 0, hp), pipeline_mode=pl.Buffered(1)),
                  pl.BlockSpec((1, vw, lc), lambda bi, hp, i: (bi, hp, 0), pipeline_mode=pl.Buffered(1)),
                  pl.BlockSpec((1, vw, l), lambda bi, hp, i: (bi, hp, 0), pipeline_mode=pl.Buffered(1))],
        out_specs=pl.BlockSpec((1, BQ, vw), lambda bi, hp, i: (bi, i, hp)),
        out_shape=jax.ShapeDtypeStruct((b, l, MLA_WIDTH), BF16),
        compiler_params=_params("parallel", "parallel", "arbitrary"),
        name="attn",
    )(q, k_c, k_l, v_c, v_l)


def _s5scan_kernel(n_ctx_chunks, uc_ref, ul_ref, bmat_ref, a_ref, cmat_ref, dskip_ref, y_ref, hs_ref, st_ref):
    d = pl.program_id(0)
    i = pl.program_id(1)

    @pl.when(i == 0)
    def _():
        st_ref[...] = jnp.zeros_like(st_ref)

    u_bt = jnp.where(i < n_ctx_chunks, uc_ref[...], ul_ref[...])
    u = jnp.swapaxes(u_bt, 0, 1).reshape(TC * SUBLANES, S5_WIDTH)
    ub = u.astype(BF16)

    def tile_cols(lt):
        base = (lt // TILES_PER_SLAB) * 2 * SLAB + (lt % TILES_PER_SLAB) * TILE_COLS
        return slice(base, base + TILE_COLS), slice(base + SLAB, base + SLAB + TILE_COLS)

    for lt in range(N_LANE_TILES):
        re, im = tile_cols(lt)
        bu = _dot(ub[:, lt * LANES:(lt + 1) * LANES], bmat_ref[0, lt])
        hs_ref[:, re] = bu[:, :TILE_COLS]
        hs_ref[:, im] = bu[:, TILE_COLS:]

    for cs in range(0, 2 * S5_COLS, 2 * SLAB):
        re = slice(cs, cs + SLAB)
        im = slice(cs + SLAB, cs + 2 * SLAB)
        a_re, a_im = a_ref[0, :, re], a_ref[0, :, im]

        def step(tt, carry, re=re, im=im, a_re=a_re, a_im=a_im):
            h_re, h_im = carry
            t = jnp.where(d == 0, tt, TC - 1 - tt)
            rows = pl.ds(pl.multiple_of(t * SUBLANES, SUBLANES), SUBLANES)
            n_re = a_re * h_re - a_im * h_im + hs_ref[rows, re]
            n_im = a_re * h_im + a_im * h_re + hs_ref[rows, im]
            hs_ref[rows, re] = n_re
            hs_ref[rows, im] = n_im
            return n_re, n_im

        h_re, h_im = lax.fori_loop(0, TC, step, (st_ref[:, re], st_ref[:, im]), unroll=4)
        st_ref[:, re] = h_re
        st_ref[:, im] = h_im

    for lt in range(N_LANE_TILES):
        re, im = tile_cols(lt)
        lanes = slice(lt * LANES, (lt + 1) * LANES)
        y = (_dot(hs_ref[:, re].astype(BF16), cmat_ref[0, lt, :TILE_COLS])
             + _dot(hs_ref[:, im].astype(BF16), cmat_ref[0, lt, TILE_COLS:])
             + u[:, lanes] * dskip_ref[0, :, lanes])
        y_ref[0, :, :, lanes] = jnp.swapaxes(y.reshape(TC, SUBLANES, LANES), 0, 1)


def _s5scan(u_c, u_l, bmat, a_b, cmat, dskip):
    b, l, _ = u_l.shape
    nc, nl = u_c.shape[1] // TC, l // TC

    def ctx_blk(d, i):
        return jnp.clip(jnp.where(d == 0, i, nc - 1 - i), 0, nc - 1)

    def lat_blk(d, i):
        return jnp.clip(jnp.where(d == 0, i - nc, nl - 1 - (i - nc)), 0, nl - 1)

    def out_blk(d, i):
        return jnp.where(i < nc, nl + ctx_blk(d, i), lat_blk(d, i))

    return pl.pallas_call(
        functools.partial(_s5scan_kernel, nc),
        grid=(2, nc + nl),
        in_specs=[pl.BlockSpec((b, TC, S5_WIDTH), lambda d, i: (0, ctx_blk(d, i), 0)),
                  pl.BlockSpec((b, TC, S5_WIDTH), lambda d, i: (0, lat_blk(d, i), 0)),
                  pl.BlockSpec((1, N_LANE_TILES, LANES, 2 * TILE_COLS), lambda d, i: (d, 0, 0, 0)),
                  pl.BlockSpec((1, SUBLANES, 2 * S5_COLS), lambda d, i: (d, 0, 0)),
                  pl.BlockSpec((1, N_LANE_TILES, 2 * TILE_COLS, LANES), lambda d, i: (d, 0, 0, 0)),
                  pl.BlockSpec((1, 1, S5_WIDTH), lambda d, i: (d, 0, 0))],
        out_specs=pl.BlockSpec((1, b, TC, S5_WIDTH), lambda d, i: (d, 0, out_blk(d, i), 0)),
        out_shape=jax.ShapeDtypeStruct((2, b, (nc + nl) * TC, S5_WIDTH), F32),
        scratch_shapes=[pltpu.VMEM((TC * SUBLANES, 2 * S5_COLS), F32),
                        pltpu.VMEM((SUBLANES, 2 * S5_COLS), F32)],
        compiler_params=_params("arbitrary", "arbitrary"),
        name="s5scan",
    )(u_c, u_l, bmat, a_b, cmat, dskip)


def _tail_kernel(x_ref, att_ref, yf_ref, yb_ref, g1_ref, sh2_ref, sc2_ref, g2_ref, wglu_ref, bglu_ref, wout_ref,
                 ln1g_ref, ln1b_ref, wgu_ref, wd_ref, ln2g_ref, ln2b_ref, o_ref, acc_ref):
    z = jax.nn.gelu(yf_ref[0, 0] + yb_ref[0, 0])
    s5o = z * jax.nn.sigmoid(_dot(z.astype(BF16), wglu_ref[...]) + bglu_ref[...])
    mix = _dot(att_ref[0], wout_ref[:MLA_WIDTH, :]) + _dot(s5o.astype(BF16), wout_ref[MLA_WIDTH:, :])
    x1 = _layer_norm(DN_ALPHA * x_ref[0] + g1_ref[0] * mix) * ln1g_ref[...] + ln1b_ref[...]

    xm = (_layer_norm(x1) * (1.0 + sc2_ref[0]) + sh2_ref[0]).astype(BF16)
    for j in range(D_FF // FF_CHUNK):
        gu = _dot(xm, wgu_ref[:, 2 * j * FF_CHUNK:2 * (j + 1) * FF_CHUNK])
        gate, up = gu[:, :FF_CHUNK], gu[:, FF_CHUNK:]
        hidden = (gate * jax.nn.sigmoid(gate) * up).astype(BF16)
        part = _dot(hidden, wd_ref[j * FF_CHUNK:(j + 1) * FF_CHUNK, :])
        if j == 0:
            acc_ref[...] = part
        else:
            acc_ref[...] += part
    o_ref[0] = _layer_norm(DN_ALPHA * x1 + g2_ref[0] * acc_ref[...]) * ln2g_ref[...] + ln2b_ref[...]


def _tail(x, att, y_dir, mod3, w_glu, b_glu, w_out, ln1_g, ln1_b, w_gu, w_down, ln2_g, ln2_b):
    b, l, _ = x.shape
    tok = lambda w: pl.BlockSpec((1, TM, w), lambda bi, i: (bi, i, 0))
    mod = lambda j: pl.BlockSpec((1, 1, D_MODEL), lambda bi, i: (bi, 0, j))
    y_of = lambda d: pl.BlockSpec((1, 1, TM, S5_WIDTH), lambda bi, i: (d, bi, i, 0))
    consts = (w_glu, b_glu, w_out, ln1_g, ln1_b, w_gu, w_down, ln2_g, ln2_b)
    return pl.pallas_call(
        _tail_kernel,
        grid=(b, l // TM),
        in_specs=[tok(D_MODEL), tok(MLA_WIDTH), y_of(0), y_of(1), mod(2), mod(3), mod(4), mod(5)]
                 + [_const_spec(w.shape) for w in consts],
        out_specs=tok(D_MODEL),
        out_shape=jax.ShapeDtypeStruct((b, l, D_MODEL), F32),
        scratch_shapes=[pltpu.VMEM((TM, D_MODEL), F32)],
        compiler_params=_params("parallel", "parallel"),
        name="tail",
    )(x, att, y_dir, y_dir, mod3, mod3, mod3, mod3, *consts)


def _rope_tables(seq):
    pos = jnp.arange(seq)
    row = (pos // GRID_W).astype(F32)
    col = (pos % GRID_W).astype(F32)
    n_freq = ROPE // 4
    freqs = ROPE_THETA ** (-jnp.arange(n_freq, dtype=F32) / n_freq)
    ang = jnp.concatenate([row[:, None] * freqs, col[:, None] * freqs], axis=-1)
    cos, sin = jnp.cos(ang), jnp.sin(ang)
    zeros = lambda w: jnp.zeros((seq, w), F32)
    tail = LANES - NOPE - ROPE
    c = jnp.concatenate([jnp.ones((seq, NOPE), F32), cos, cos, zeros(tail)], axis=-1)
    sa = jnp.concatenate([zeros(NOPE), -sin, zeros(ROPE_HALF), zeros(tail)], axis=-1)
    sb = jnp.concatenate([zeros(NOPE), zeros(ROPE_HALF), sin, zeros(tail)], axis=-1)
    return c, sa, sb


def _pair_split(w):
    return jnp.concatenate([w[..., 0::2], w[..., 1::2]], axis=-1)


def kernel(x, c, ctx, c_ctx, w_ada, b_ada, w_in, q_norm_g, kv_norm_g, w_uq, w_uk, w_uv, s5_lambda_re, s5_lambda_im, s5_log_dt, s5_b_re, s5_b_im, s5_c_re, s5_c_im, s5_d, s5_w_glu, s5_b_glu, w_out, ln1_g, ln1_b, w_gate_up, w_down, ln2_g, ln2_b):
    assert w_ada.shape[0] == DEPTH == 1
    b, l, _ = x.shape
    assert b == SUBLANES, "the S5 scan maps the batch onto the sublanes of one vreg row"
    lc = ctx.shape[1]
    row2 = lambda t: t.reshape(1, -1)

    cond = jnp.concatenate([c, c_ctx[None, :], jnp.zeros((2 * SUBLANES - b - 1, D_MODEL), F32)], axis=0)
    mod = _adaln(cond, w_ada[0], row2(b_ada[0]))
    mod3 = mod.reshape(mod.shape[0], 1, 6 * D_MODEL)

    wi = w_in[0]
    rope_tile = jnp.zeros((D_MODEL, LANES), F32).at[:, NOPE:NOPE + ROPE].set(_pair_split(wi[:, KV_END:ROPE_END]))
    w_in_r = jnp.concatenate([wi[:, :KV_END], wi[:, ROPE_END:], rope_tile], axis=1).astype(BF16)
    uq = w_uq[0]
    uq = jnp.concatenate([uq[..., :NOPE], _pair_split(uq[..., NOPE:]),
                          jnp.zeros((Q_RANK, N_HEADS, HEAD_PAD - NOPE - ROPE), F32)], axis=-1)
    wuq = uq.reshape(Q_RANK, N_HEADS * HEAD_PAD).astype(BF16)
    uk = jnp.concatenate([w_uk[0], jnp.zeros((KV_RANK, N_HEADS, HEAD_PAD - NOPE), F32)], axis=-1)
    wuk = uk.reshape(KV_RANK, N_HEADS * HEAD_PAD).astype(BF16)
    wuv = w_uv[0].reshape(KV_RANK, MLA_WIDTH).T.astype(BF16)
    qg, kvg = row2(q_norm_g[0]), row2(kv_norm_g[0])

    rope_c, rope_sa, rope_sb = _rope_tables(l)
    flat_c = jnp.concatenate([jnp.ones((lc, NOPE + ROPE), F32), jnp.zeros((lc, LANES - NOPE - ROPE), F32)], axis=-1)
    flat_s = jnp.zeros((lc, LANES), F32)

    q, k_l, v_l, u_l = _inproj(x, mod3, None, w_in_r, qg, kvg, wuq, wuk, wuv, rope_c, rope_sa, rope_sb, True)
    k_c, v_c, u_c = _inproj(ctx, mod3, b, w_in_r, qg, kvg, wuq, wuk, wuv, flat_c, flat_s, flat_s, False)

    att = _attention(q, k_c, k_l, v_c, v_l)

    a_re, a_im, bb_re, bb_im = _s5prep(s5_lambda_re[0], s5_lambda_im[0], s5_log_dt[0], s5_b_re[0], s5_b_im[0])
    bmat = jnp.concatenate([_tile_block_diag(bb_re), _tile_block_diag(bb_im)], axis=-1).astype(BF16)
    c_t = lambda t: jnp.swapaxes(t, -1, -2)
    cmat = jnp.concatenate([_tile_block_diag(c_t(s5_c_re[0])), -_tile_block_diag(c_t(s5_c_im[0]))],
                           axis=-2).astype(BF16)
    slabs = lambda t: t.reshape(2, S5_COLS // SLAB, 1, SLAB)
    a_cat = jnp.concatenate([slabs(a_re), slabs(a_im)], axis=2).reshape(2, 2 * S5_COLS)
    a_b = jnp.broadcast_to(a_cat[:, None, :], (2, SUBLANES, 2 * S5_COLS))
    dskip = jnp.stack([s5_d[0], jnp.zeros_like(s5_d[0])])[:, None, :]
    y_dir = _s5scan(u_c, u_l, bmat, a_b, cmat, dskip)

    n_ff = D_FF // FF_CHUNK
    w_gu = jnp.swapaxes(w_gate_up[0].reshape(D_MODEL, 2, n_ff, FF_CHUNK), 1, 2).reshape(D_MODEL, 2 * D_FF)
    return _tail(x, att, y_dir, mod3, s5_w_glu[0].astype(BF16), row2(s5_b_glu[0]), w_out[0].astype(BF16),
                 row2(ln1_g[0]), row2(ln1_b[0]), w_gu.astype(BF16), w_down[0].astype(BF16),
                 row2(ln2_g[0]), row2(ln2_b[0]))
```

```python
import functools
import math

import jax
import jax.numpy as jnp
from jax import lax
from jax.experimental import pallas as pl
from jax.experimental.pallas import tpu as pltpu

D_MODEL = 1024
GRID_W = 64
N_HEADS = 8
NOPE = 64
ROPE = 32
V_DIM = 64
Q_RANK = 384
KV_RANK = 256
MLA_WIDTH = N_HEADS * V_DIM
S5_WIDTH = D_MODEL - MLA_WIDTH
S5_GROUP = 16
S5_GROUPS = S5_WIDTH // S5_GROUP
S5_STATE = 64
S5_COLS = S5_GROUPS * S5_STATE
KV_END = Q_RANK + KV_RANK
ROPE_END = KV_END + ROPE
D_FF = 2816
ROPE_THETA = 10000.0
NORM_EPS = 1e-6
DEPTH = 1
DN_ALPHA = (2.0 * DEPTH) ** 0.25

LANES = 128
SUBLANES = 8
HEAD_PAD = LANES
ROPE_HALF = ROPE // 2
P_COLS = Q_RANK + KV_RANK + S5_WIDTH + LANES
VMEM_LIMIT = 56 * 1024 * 1024

TM = 512
BQ = 512
DEN_ROWS = 16
HEADS_PER_STEP = 8
Q_SCALE = (NOPE + ROPE) ** -0.5 * math.log2(math.e)
TC = 128
S5_T = 4
N_ROWS = TC // S5_T * SUBLANES
N_LANE_TILES = S5_WIDTH // LANES
GROUPS_PER_TILE = LANES // S5_GROUP
TILE_COLS = GROUPS_PER_TILE * S5_STATE
TILES_PER_SLAB = 2
SLAB = TILES_PER_SLAB * TILE_COLS
FF_CHUNK = 256

F32 = jnp.float32
BF16 = jnp.bfloat16


def _params(*sem):
    return pltpu.CompilerParams(dimension_semantics=sem, vmem_limit_bytes=VMEM_LIMIT)


def _const_spec(shape):
    nd = len(shape)
    return pl.BlockSpec(shape, lambda *_: (0,) * nd, pipeline_mode=pl.Buffered(1))


def _layer_norm(x):
    mu = jnp.mean(x, axis=-1, keepdims=True)
    xc = x - mu
    var = jnp.mean(xc * xc, axis=-1, keepdims=True)
    return xc * lax.rsqrt(var + NORM_EPS)


def _rms_norm(x, g):
    return x * lax.rsqrt(jnp.mean(x * x, axis=-1, keepdims=True) + NORM_EPS) * g


def _dot(a, b):
    return jnp.dot(a, b, preferred_element_type=F32)


_NT = (((1,), (1,)), ((), ()))


def _adaln_kernel(cond_ref, w_ref, b_ref, o_ref):
    cnd = cond_ref[...]
    act = cnd * jax.nn.sigmoid(cnd)
    o_ref[...] = _dot(act.astype(BF16), w_ref[...].astype(BF16)) + b_ref[...]


def _adaln(cond, w_ada, b_ada):
    rows, n = cond.shape[0], w_ada.shape[1]
    tn = D_MODEL
    return pl.pallas_call(
        _adaln_kernel,
        grid=(n // tn,),
        in_specs=[pl.BlockSpec((rows, D_MODEL), lambda j: (0, 0)),
                  pl.BlockSpec((D_MODEL, tn), lambda j: (0, j)),
                  pl.BlockSpec((1, tn), lambda j: (0, j))],
        out_specs=pl.BlockSpec((rows, tn), lambda j: (0, j)),
        out_shape=jax.ShapeDtypeStruct((rows, n), F32),
        compiler_params=_params("arbitrary"),
        name="adaln",
    )(cond, w_ada, b_ada)


def _s5prep_kernel(lre_ref, lim_ref, ldt_ref, bre_ref, bim_ref, cre_ref, cim_ref, at_ref, ab_ref, ca_ref, kk_ref):
    lre, lim = lre_ref[...], lim_ref[...]
    dt = jnp.exp(ldt_ref[...])
    mag = jnp.exp(lre * dt)
    a_re, a_im = mag * jnp.cos(lim * dt), mag * jnp.sin(lim * dt)
    den = lre * lre + lim * lim
    f_re = ((a_re - 1) * lre + a_im * lim) / den
    f_im = (a_im * lre - (a_re - 1) * lim) / den
    b_re, b_im = bre_ref[...], bim_ref[...]
    c_re, c_im = cre_ref[...], cim_ref[...]

    def times_a(z_re, z_im):
        return a_re * z_re - a_im * z_im, a_re * z_im + a_im * z_re

    def over_states(x, y):
        return lax.dot_general(x, y, (((2,), (2,)), ((0,), (0,))), precision=lax.Precision.HIGHEST,
                               preferred_element_type=F32)

    p_re, p_im = f_re * b_re - f_im * b_im, f_re * b_im + f_im * b_re
    q_re, q_im = c_re, c_im
    w_re, w_im = jnp.ones_like(a_re), jnp.zeros_like(a_re)
    for k in range(S5_T):
        ab_ref[0, k] = p_re
        ab_ref[1, k] = p_im
        kk_ref[k] = over_states(c_re, p_re) - over_states(c_im, p_im)
        p_re, p_im = times_a(p_re, p_im)
        q_re, q_im = times_a(q_re, q_im)
        w_re, w_im = times_a(w_re, w_im)
        ca_ref[0, k] = q_re
        ca_ref[1, k] = q_im
    at_ref[0] = w_re
    at_ref[1] = w_im


def _s5prep(lam_re, lam_im, log_dt, b_re, b_im, c_re, c_im):
    n = 2 * S5_GROUPS
    full = (n, S5_GROUP, S5_STATE)
    per_state = lambda t: jnp.broadcast_to(t.reshape(n, 1, S5_STATE), full)
    ldt = jnp.broadcast_to(log_dt.reshape(n, 1, 1), full)
    chan_state = lambda t: t.reshape(full)
    shp = lambda *lead: jax.ShapeDtypeStruct(lead + full, F32)
    at, ab, ca, kk = pl.pallas_call(
        _s5prep_kernel,
        out_shape=(shp(2), shp(2, S5_T), shp(2, S5_T), jax.ShapeDtypeStruct((S5_T, n, S5_GROUP, S5_GROUP), F32)),
        compiler_params=pltpu.CompilerParams(vmem_limit_bytes=VMEM_LIMIT),
        name="s5prep",
    )(per_state(lam_re), per_state(lam_im), ldt, chan_state(jnp.swapaxes(b_re, -1, -2)),
      chan_state(jnp.swapaxes(b_im, -1, -2)), chan_state(c_re), chan_state(c_im))
    by_dir = lambda t, lead: t.reshape(lead + (2, S5_GROUPS) + t.shape[-2:])
    return by_dir(at, (2,))[:, :, :, 0], by_dir(ab, (2, S5_T)), by_dir(ca, (2, S5_T)), by_dir(kk, (S5_T,))


def _s5_matrices(ab, ca, kk):
    t_, g8, nt = S5_T, GROUPS_PER_TILE, N_LANE_TILES
    eye = jnp.eye(g8, dtype=F32)
    w_all, v_all, m_all = [], [], []
    for d in range(2):
        walked = (lambda s: s) if d == 0 else (lambda s: t_ - 1 - s)
        ab_d = jnp.stack([ab[:, t_ - 1 - walked(s), d] for s in range(t_)], axis=1)
        ab_d = ab_d.reshape(2, t_, nt, g8, S5_GROUP, S5_STATE).transpose(2, 1, 3, 4, 0, 5)
        w = ab_d[:, :, :, :, :, None, :] * eye[None, None, :, None, None, :, None]
        w_all.append(w.reshape(nt, t_ * LANES, 2 * TILE_COLS))
        ca_d = jnp.stack([ca[:, walked(t), d] for t in range(t_)], axis=1)
        ca_d = ca_d * jnp.array([1.0, -1.0], F32).reshape(2, 1, 1, 1, 1)
        ca_d = ca_d.reshape(2, t_, nt, g8, S5_GROUP, S5_STATE).transpose(2, 0, 3, 5, 1, 4)
        v = ca_d[:, :, :, :, :, None, :] * eye[None, None, :, None, None, :, None]
        v_all.append(v.reshape(nt, 2 * TILE_COLS, t_ * LANES))
        zero = jnp.zeros_like(kk[0, d])
        blocks = jnp.stack([jnp.stack([kk[walked(t) - walked(s), d] if walked(t) >= walked(s) else zero
                                       for t in range(t_)], axis=0) for s in range(t_)], axis=0)
        blocks = blocks.reshape(t_, t_, nt, g8, S5_GROUP, S5_GROUP).transpose(2, 0, 3, 5, 1, 4)
        m = blocks[:, :, :, :, :, None, :] * eye[None, None, :, None, None, :, None]
        m_all.append(m.reshape(nt, t_ * LANES, t_ * LANES))
    return (jnp.stack(w_all).astype(BF16), jnp.stack(v_all).astype(BF16), jnp.stack(m_all).astype(BF16))


def _rope_tile(t, c, sa, sb):
    return t * c + pltpu.roll(t, LANES - ROPE_HALF, 1) * sa + pltpu.roll(t, ROPE_HALF, 1) * sb


def _inproj_kernel(with_q, x_ref, sh_ref, sc_ref, win_ref, qg_ref, kvg_ref, wuq_ref, wuk_ref, wuv_ref,
                   c_ref, sa_ref, sb_ref, *out_refs):
    if with_q:
        q_ref, k_ref, v_ref, u_ref = out_refs
    else:
        k_ref, v_ref, u_ref = out_refs
    x = x_ref[0]
    xm = _layer_norm(x) * (1.0 + sc_ref[0]) + sh_ref[0]
    p = _dot(xm.astype(BF16), win_ref[...])
    c, sa, sb = c_ref[...], sa_ref[...], sb_ref[...]

    kv_c = _rms_norm(p[:, Q_RANK:KV_END], kvg_ref[...]).astype(BF16)
    k = _dot(kv_c, wuk_ref[...])
    kr = _rope_tile(p[:, KV_END + S5_WIDTH:], c, sa, sb)
    for h in range(N_HEADS):
        k_ref[0, :, h * HEAD_PAD:(h + 1) * HEAD_PAD] = (k[:, h * HEAD_PAD:(h + 1) * HEAD_PAD] + kr).astype(BF16)
    v_ref[0] = lax.dot_general(wuv_ref[...], kv_c, _NT, preferred_element_type=F32).astype(BF16)
    u_ref[0] = p[:, KV_END:KV_END + S5_WIDTH]

    if with_q:
        q_c = _rms_norm(p[:, :Q_RANK], qg_ref[...]).astype(BF16)
        q = _dot(q_c, wuq_ref[...])
        cq, saq, sbq = c * Q_SCALE, sa * Q_SCALE, sb * Q_SCALE
        for h in range(N_HEADS):
            sl = slice(h * HEAD_PAD, (h + 1) * HEAD_PAD)
            q_ref[0, :, sl] = _rope_tile(q[:, sl], cq, saq, sbq).astype(BF16)


def _inproj(x, mod3, mod_row, w_in_r, qg, kvg, wuq, wuk, wuv, rope_c, rope_sa, rope_sb, with_q):
    b, l, _ = x.shape
    tm = min(TM, l)
    row = (lambda bi: bi) if mod_row is None else (lambda bi: mod_row)
    tok = lambda w: pl.BlockSpec((1, tm, w), lambda bi, i: (bi, i, 0))
    in_specs = [
        tok(D_MODEL),
        pl.BlockSpec((1, 1, D_MODEL), lambda bi, i: (row(bi), 0, 0)),
        pl.BlockSpec((1, 1, D_MODEL), lambda bi, i: (row(bi), 0, 1)),
        _const_spec(w_in_r.shape), _const_spec(qg.shape), _const_spec(kvg.shape),
        _const_spec(wuq.shape), _const_spec(wuk.shape), _const_spec(wuv.shape),
        pl.BlockSpec((tm, LANES), lambda bi, i: (i, 0)),
        pl.BlockSpec((tm, LANES), lambda bi, i: (i, 0)),
        pl.BlockSpec((tm, LANES), lambda bi, i: (i, 0)),
    ]
    out_specs = [tok(N_HEADS * HEAD_PAD), pl.BlockSpec((1, MLA_WIDTH, tm), lambda bi, i: (bi, 0, i)), tok(S5_WIDTH)]
    out_shape = [jax.ShapeDtypeStruct((b, l, N_HEADS * HEAD_PAD), BF16),
                 jax.ShapeDtypeStruct((b, MLA_WIDTH, l), BF16),
                 jax.ShapeDtypeStruct((b, l, S5_WIDTH), F32)]
    if with_q:
        out_specs = [tok(N_HEADS * HEAD_PAD)] + out_specs
        out_shape = [jax.ShapeDtypeStruct((b, l, N_HEADS * HEAD_PAD), BF16)] + out_shape
    return pl.pallas_call(
        functools.partial(_inproj_kernel, with_q),
        grid=(b, l // tm),
        in_specs=in_specs, out_specs=out_specs, out_shape=out_shape,
        compiler_params=_params("parallel", "parallel"),
        name="inproj_lat" if with_q else "inproj_ctx",
    )(x, mod3, mod3, w_in_r, qg, kvg, wuq, wuk, wuv, rope_c, rope_sa, rope_sb)


def _attn_kernel(q_ref, kc_ref, kl_ref, vc_ref, vl_ref, o_ref):
    def scores(h):
        ks = slice(h * HEAD_PAD, (h + 1) * HEAD_PAD)
        q = q_ref[0, :, ks]
        return (lax.dot_general(kc_ref[0, :, ks], q, _NT, preferred_element_type=F32),
                lax.dot_general(kl_ref[0, :, ks], q, _NT, preferred_element_type=F32))

    def with_ones(vt):
        return jnp.concatenate([vt, jnp.ones((DEN_ROWS, vt.shape[1]), BF16)], axis=0)

    outs = []
    s_next = scores(0)
    for h in range(HEADS_PER_STEP):
        s_c, s_l = s_next
        if h + 1 < HEADS_PER_STEP:
            s_next = scores(h + 1)
        vs = slice(h * V_DIM, (h + 1) * V_DIM)
        m = jnp.maximum(jnp.max(s_c, axis=0, keepdims=True), jnp.max(s_l, axis=0, keepdims=True))
        e_c = jnp.exp2(s_c - m).astype(BF16)
        e_l = jnp.exp2(s_l - m).astype(BF16)
        o_t = _dot(with_ones(vc_ref[0, vs, :]), e_c) + _dot(with_ones(vl_ref[0, vs, :]), e_l)
        outs.append(o_t[:V_DIM] / o_t[V_DIM:V_DIM + 1])
    o_ref[0] = jnp.concatenate(outs, axis=0).T.astype(o_ref.dtype)


def _attention(q, k_c, k_l, v_c, v_l):
    b, l, _ = q.shape
    lc = k_c.shape[1]
    kw, vw = HEADS_PER_STEP * HEAD_PAD, HEADS_PER_STEP * V_DIM
    return pl.pallas_call(
        _attn_kernel,
        grid=(b, N_HEADS // HEADS_PER_STEP, l // BQ),
        in_specs=[pl.BlockSpec((1, BQ, kw), lambda bi, hp, i: (bi, i, hp)),
                  pl.BlockSpec((1, lc, kw), lambda bi, hp, i: (bi, 0, hp), pipeline_mode=pl.Buffered(1)),
                  pl.BlockSpec((1, l, kw), lambda bi, hp, i: (bi, 0, hp), pipeline_mode=pl.Buffered(1)),
                  pl.BlockSpec((1, vw, lc), lambda bi, hp, i: (bi, hp, 0), pipeline_mode=pl.Buffered(1)),
                  pl.BlockSpec((1, vw, l), lambda bi, hp, i: (bi, hp, 0), pipeline_mode=pl.Buffered(1))],
        out_specs=pl.BlockSpec((1, BQ, vw), lambda bi, hp, i: (bi, i, hp)),
        out_shape=jax.ShapeDtypeStruct((b, l, MLA_WIDTH), BF16),
        compiler_params=_params("parallel", "parallel", "arbitrary"),
        name="attn",
    )(q, k_c, k_l, v_c, v_l)


def _s5scan_kernel(n_ctx_chunks, uc_ref, ul_ref, w_ref, a_ref, v_ref, m_ref, dskip_ref, y_ref, hs_ref, st_ref):
    d = pl.program_id(0)
    i = pl.program_id(1)

    @pl.when(i == 0)
    def _():
        st_ref[...] = jnp.zeros_like(st_ref)

    u_bt = jnp.where(i < n_ctx_chunks, uc_ref[...], ul_ref[...])
    u = jnp.swapaxes(u_bt, 0, 1).reshape(N_ROWS // SUBLANES, S5_T, SUBLANES, S5_WIDTH)
    u_pos = [u[:, s].reshape(N_ROWS, S5_WIDTH) for s in range(S5_T)]

    def group_lanes(lt):
        lanes = slice(lt * LANES, (lt + 1) * LANES)
        return jnp.concatenate([p[:, lanes] for p in u_pos], axis=1).astype(BF16)

    def tile_cols(lt):
        base = (lt // TILES_PER_SLAB) * 2 * SLAB + (lt % TILES_PER_SLAB) * TILE_COLS
        return slice(base, base + TILE_COLS), slice(base + SLAB, base + SLAB + TILE_COLS)

    for lt in range(N_LANE_TILES):
        re, im = tile_cols(lt)
        inc = _dot(group_lanes(lt), w_ref[0, lt])
        hs_ref[:, re] = inc[:, :TILE_COLS]
        hs_ref[:, im] = inc[:, TILE_COLS:]

    n_steps = N_ROWS // SUBLANES
    for cs in range(0, 2 * S5_COLS, 2 * SLAB):
        re = slice(cs, cs + SLAB)
        im = slice(cs + SLAB, cs + 2 * SLAB)
        a_re, a_im = a_ref[0, :, re], a_ref[0, :, im]

        def step(tt, carry, re=re, im=im, a_re=a_re, a_im=a_im):
            h_re, h_im = carry
            t = jnp.where(d == 0, tt, n_steps - 1 - tt)
            rows = pl.ds(pl.multiple_of(t * SUBLANES, SUBLANES), SUBLANES)
            n_re = a_re * h_re - a_im * h_im + hs_ref[rows, re]
            n_im = a_re * h_im + a_im * h_re + hs_ref[rows, im]
            hs_ref[rows, re] = h_re
            hs_ref[rows, im] = h_im
            return n_re, n_im

        h_re, h_im = lax.fori_loop(0, n_steps, step, (st_ref[:, re], st_ref[:, im]), unroll=4)
        st_ref[:, re] = h_re
        st_ref[:, im] = h_im

    for lt in range(N_LANE_TILES):
        re, im = tile_cols(lt)
        lanes = slice(lt * LANES, (lt + 1) * LANES)
        y = (_dot(hs_ref[:, re].astype(BF16), v_ref[0, lt, :TILE_COLS])
             + _dot(hs_ref[:, im].astype(BF16), v_ref[0, lt, TILE_COLS:])
             + _dot(group_lanes(lt), m_ref[0, lt]))
        skip = dskip_ref[0, :, lanes]
        y_pos = [(y[:, s * LANES:(s + 1) * LANES] + u_pos[s][:, lanes] * skip)
                 .reshape(N_ROWS // SUBLANES, 1, SUBLANES, LANES) for s in range(S5_T)]
        y_tb = jnp.concatenate(y_pos, axis=1).reshape(TC, SUBLANES, LANES)
        y_ref[0, :, :, lanes] = jnp.swapaxes(y_tb, 0, 1)


def _s5scan(u_c, u_l, w_mat, a_b, v_mat, m_mat, dskip):
    b, l, _ = u_l.shape
    nc, nl = u_c.shape[1] // TC, l // TC

    def ctx_blk(d, i):
        return jnp.clip(jnp.where(d == 0, i, nc - 1 - i), 0, nc - 1)

    def lat_blk(d, i):
        return jnp.clip(jnp.where(d == 0, i - nc, nl - 1 - (i - nc)), 0, nl - 1)

    def out_blk(d, i):
        return jnp.where(i < nc, nl + ctx_blk(d, i), lat_blk(d, i))

    return pl.pallas_call(
        functools.partial(_s5scan_kernel, nc),
        grid=(2, nc + nl),
        in_specs=[pl.BlockSpec((b, TC, S5_WIDTH), lambda d, i: (0, ctx_blk(d, i), 0)),
                  pl.BlockSpec((b, TC, S5_WIDTH), lambda d, i: (0, lat_blk(d, i), 0)),
                  pl.BlockSpec((1, N_LANE_TILES, S5_T * LANES, 2 * TILE_COLS), lambda d, i: (d, 0, 0, 0)),
                  pl.BlockSpec((1, SUBLANES, 2 * S5_COLS), lambda d, i: (d, 0, 0)),
                  pl.BlockSpec((1, N_LANE_TILES, 2 * TILE_COLS, S5_T * LANES), lambda d, i: (d, 0, 0, 0)),
                  pl.BlockSpec((1, N_LANE_TILES, S5_T * LANES, S5_T * LANES), lambda d, i: (d, 0, 0, 0)),
                  pl.BlockSpec((1, 1, S5_WIDTH), lambda d, i: (d, 0, 0))],
        out_specs=pl.BlockSpec((1, b, TC, S5_WIDTH), lambda d, i: (d, 0, out_blk(d, i), 0)),
        out_shape=jax.ShapeDtypeStruct((2, b, (nc + nl) * TC, S5_WIDTH), F32),
        scratch_shapes=[pltpu.VMEM((N_ROWS, 2 * S5_COLS), F32),
                        pltpu.VMEM((SUBLANES, 2 * S5_COLS), F32)],
        compiler_params=_params("arbitrary", "arbitrary"),
        name="s5scan",
    )(u_c, u_l, w_mat, a_b, v_mat, m_mat, dskip)


def _tail_kernel(x_ref, att_ref, yf_ref, yb_ref, g1_ref, sh2_ref, sc2_ref, g2_ref, wglu_ref, bglu_ref, wout_ref,
                 ln1g_ref, ln1b_ref, wgu_ref, wd_ref, ln2g_ref, ln2b_ref, o_ref, acc_ref):
    z = jax.nn.gelu(yf_ref[0, 0] + yb_ref[0, 0])
    s5o = z * jax.nn.sigmoid(_dot(z.astype(BF16), wglu_ref[...]) + bglu_ref[...])
    mix = _dot(att_ref[0], wout_ref[:MLA_WIDTH, :]) + _dot(s5o.astype(BF16), wout_ref[MLA_WIDTH:, :])
    x1 = _layer_norm(DN_ALPHA * x_ref[0] + g1_ref[0] * mix) * ln1g_ref[...] + ln1b_ref[...]

    xm = (_layer_norm(x1) * (1.0 + sc2_ref[0]) + sh2_ref[0]).astype(BF16)
    for j in range(D_FF // FF_CHUNK):
        gate = _dot(xm, wgu_ref[:, j * FF_CHUNK:(j + 1) * FF_CHUNK])
        up = _dot(xm, wgu_ref[:, D_FF + j * FF_CHUNK:D_FF + (j + 1) * FF_CHUNK])
        hidden = (gate * jax.nn.sigmoid(gate) * up).astype(BF16)
        part = _dot(hidden, wd_ref[j * FF_CHUNK:(j + 1) * FF_CHUNK, :])
        if j == 0:
            acc_ref[...] = part
        else:
            acc_ref[...] += part
    o_ref[0] = _layer_norm(DN_ALPHA * x1 + g2_ref[0] * acc_ref[...]) * ln2g_ref[...] + ln2b_ref[...]


def _tail(x, att, y_dir, mod3, w_glu, b_glu, w_out, ln1_g, ln1_b, w_gu, w_down, ln2_g, ln2_b):
    b, l, _ = x.shape
    tok = lambda w: pl.BlockSpec((1, TM, w), lambda bi, i: (bi, i, 0))
    mod = lambda j: pl.BlockSpec((1, 1, D_MODEL), lambda bi, i: (bi, 0, j))
    y_of = lambda d: pl.BlockSpec((1, 1, TM, S5_WIDTH), lambda bi, i: (d, bi, i, 0))
    consts = (w_glu, b_glu, w_out, ln1_g, ln1_b, w_gu, w_down, ln2_g, ln2_b)
    return pl.pallas_call(
        _tail_kernel,
        grid=(b, l // TM),
        in_specs=[tok(D_MODEL), tok(MLA_WIDTH), y_of(0), y_of(1), mod(2), mod(3), mod(4), mod(5)]
                 + [_const_spec(w.shape) for w in consts],
        out_specs=tok(D_MODEL),
        out_shape=jax.ShapeDtypeStruct((b, l, D_MODEL), F32),
        scratch_shapes=[pltpu.VMEM((TM, D_MODEL), F32)],
        compiler_params=_params("parallel", "parallel"),
        name="tail",
    )(x, att, y_dir, y_dir, mod3, mod3, mod3, mod3, *consts)


def _rope_tables(seq):
    pos = jnp.arange(seq)
    row = (pos // GRID_W).astype(F32)
    col = (pos % GRID_W).astype(F32)
    n_freq = ROPE // 4
    freqs = ROPE_THETA ** (-jnp.arange(n_freq, dtype=F32) / n_freq)
    ang = jnp.concatenate([row[:, None] * freqs, col[:, None] * freqs], axis=-1)
    cos, sin = jnp.cos(ang), jnp.sin(ang)
    zeros = lambda w: jnp.zeros((seq, w), F32)
    tail = LANES - NOPE - ROPE
    c = jnp.concatenate([jnp.ones((seq, NOPE), F32), cos, cos, zeros(tail)], axis=-1)
    sa = jnp.concatenate([zeros(NOPE), -sin, zeros(ROPE_HALF), zeros(tail)], axis=-1)
    sb = jnp.concatenate([zeros(NOPE), zeros(ROPE_HALF), sin, zeros(tail)], axis=-1)
    return c, sa, sb


def _pair_split(w):
    return jnp.concatenate([w[..., 0::2], w[..., 1::2]], axis=-1)


def kernel(x, c, ctx, c_ctx, w_ada, b_ada, w_in, q_norm_g, kv_norm_g, w_uq, w_uk, w_uv, s5_lambda_re, s5_lambda_im, s5_log_dt, s5_b_re, s5_b_im, s5_c_re, s5_c_im, s5_d, s5_w_glu, s5_b_glu, w_out, ln1_g, ln1_b, w_gate_up, w_down, ln2_g, ln2_b):
    assert w_ada.shape[0] == DEPTH == 1
    b, l, _ = x.shape
    assert b == SUBLANES, "the S5 scan maps the batch onto the sublanes of one vreg row"
    lc = ctx.shape[1]
    row2 = lambda t: t.reshape(1, -1)

    cond = jnp.concatenate([c, c_ctx[None, :], jnp.zeros((2 * SUBLANES - b - 1, D_MODEL), F32)], axis=0)
    mod = _adaln(cond, w_ada[0], row2(b_ada[0]))
    mod3 = mod.reshape(mod.shape[0], 1, 6 * D_MODEL)

    wi = w_in[0]
    rope_tile = jnp.zeros((D_MODEL, LANES), F32).at[:, NOPE:NOPE + ROPE].set(_pair_split(wi[:, KV_END:ROPE_END]))
    w_in_r = jnp.concatenate([wi[:, :KV_END], wi[:, ROPE_END:], rope_tile], axis=1).astype(BF16)
    uq = w_uq[0]
    uq = jnp.concatenate([uq[..., :NOPE], _pair_split(uq[..., NOPE:]),
                          jnp.zeros((Q_RANK, N_HEADS, HEAD_PAD - NOPE - ROPE), F32)], axis=-1)
    wuq = uq.reshape(Q_RANK, N_HEADS * HEAD_PAD).astype(BF16)
    uk = jnp.concatenate([w_uk[0], jnp.zeros((KV_RANK, N_HEADS, HEAD_PAD - NOPE), F32)], axis=-1)
    wuk = uk.reshape(KV_RANK, N_HEADS * HEAD_PAD).astype(BF16)
    wuv = w_uv[0].reshape(KV_RANK, MLA_WIDTH).T.astype(BF16)
    qg, kvg = row2(q_norm_g[0]), row2(kv_norm_g[0])

    rope_c, rope_sa, rope_sb = _rope_tables(l)
    flat_c = jnp.concatenate([jnp.ones((lc, NOPE + ROPE), F32), jnp.zeros((lc, LANES - NOPE - ROPE), F32)], axis=-1)
    flat_s = jnp.zeros((lc, LANES), F32)

    q, k_l, v_l, u_l = _inproj(x, mod3, None, w_in_r, qg, kvg, wuq, wuk, wuv, rope_c, rope_sa, rope_sb, True)
    k_c, v_c, u_c = _inproj(ctx, mod3, b, w_in_r, qg, kvg, wuq, wuk, wuv, flat_c, flat_s, flat_s, False)

    att = _attention(q, k_c, k_l, v_c, v_l)

    a_t, ab, ca, kk = _s5prep(s5_lambda_re[0], s5_lambda_im[0], s5_log_dt[0], s5_b_re[0], s5_b_im[0],
                              s5_c_re[0], s5_c_im[0])
    w_mat, v_mat, m_mat = _s5_matrices(ab, ca, kk)
    slabs = lambda t: t.reshape(2, S5_COLS // SLAB, 1, SLAB)
    a_cat = jnp.concatenate([slabs(a_t[0]), slabs(a_t[1])], axis=2).reshape(2, 2 * S5_COLS)
    a_b = jnp.broadcast_to(a_cat[:, None, :], (2, SUBLANES, 2 * S5_COLS))
    dskip = jnp.stack([s5_d[0], jnp.zeros_like(s5_d[0])])[:, None, :]
    y_dir = _s5scan(u_c, u_l, w_mat, a_b, v_mat, m_mat, dskip)

    return _tail(x, att, y_dir, mod3, s5_w_glu[0].astype(BF16), row2(s5_b_glu[0]), w_out[0].astype(BF16),
                 row2(ln1_g[0]), row2(ln1_b[0]), w_gate_up[0].astype(BF16), w_down[0].astype(BF16),
                 row2(ln2_g[0]), row2(ln2_b[0]))
```

```python
import functools
import math

import jax
import jax.numpy as jnp
import numpy as np
from jax import lax
from jax.experimental import pallas as pl
from jax.experimental.pallas import tpu as pltpu

D_MODEL = 1024
GRID_W = 64
N_HEADS = 8
NOPE = 64
ROPE = 32
V_DIM = 64
Q_RANK = 384
KV_RANK = 256
MLA_WIDTH = N_HEADS * V_DIM
S5_WIDTH = D_MODEL - MLA_WIDTH
S5_GROUP = 16
S5_GROUPS = S5_WIDTH // S5_GROUP
S5_STATE = 64
S5_COLS = S5_GROUPS * S5_STATE
KV_END = Q_RANK + KV_RANK
ROPE_END = KV_END + ROPE
D_FF = 2816
ROPE_THETA = 10000.0
NORM_EPS = 1e-6
DEPTH = 1
DN_ALPHA = (2.0 * DEPTH) ** 0.25

LANES = 128
SUBLANES = 8
HEAD_PAD = LANES
ROPE_HALF = ROPE // 2
P_COLS = Q_RANK + KV_RANK + S5_WIDTH + LANES
VMEM_LIMIT = 56 * 1024 * 1024

TM = 512
BQ = 512
DEN_ROWS = 16
HEADS_PER_STEP = 8
Q_SCALE = (NOPE + ROPE) ** -0.5 * math.log2(math.e)
TC = 128
S5_T = 4
N_ROWS = TC // S5_T * SUBLANES
N_LANE_TILES = S5_WIDTH // LANES
GROUPS_PER_TILE = LANES // S5_GROUP
TILE_COLS = GROUPS_PER_TILE * S5_STATE
TILES_PER_SLAB = 2
SLAB = TILES_PER_SLAB * TILE_COLS
FF_CHUNK = 256

F32 = jnp.float32
BF16 = jnp.bfloat16


def _params(*sem):
    return pltpu.CompilerParams(dimension_semantics=sem, vmem_limit_bytes=VMEM_LIMIT)


def _const_spec(shape):
    nd = len(shape)
    return pl.BlockSpec(shape, lambda *_: (0,) * nd, pipeline_mode=pl.Buffered(1))


def _layer_norm(x):
    mu = jnp.mean(x, axis=-1, keepdims=True)
    xc = x - mu
    var = jnp.mean(xc * xc, axis=-1, keepdims=True)
    return xc * lax.rsqrt(var + NORM_EPS)


def _rms_norm(x, g):
    return x * lax.rsqrt(jnp.mean(x * x, axis=-1, keepdims=True) + NORM_EPS) * g


def _dot(a, b):
    return jnp.dot(a, b, preferred_element_type=F32)


_NT = (((1,), (1,)), ((), ()))


def _adaln_kernel(cond_ref, w_ref, b_ref, o_ref):
    cnd = cond_ref[...]
    act = cnd * jax.nn.sigmoid(cnd)
    o_ref[...] = _dot(act.astype(BF16), w_ref[...].astype(BF16)) + b_ref[...]


def _adaln(cond, w_ada, b_ada):
    rows, n = cond.shape[0], w_ada.shape[1]
    tn = D_MODEL
    return pl.pallas_call(
        _adaln_kernel,
        grid=(n // tn,),
        in_specs=[pl.BlockSpec((rows, D_MODEL), lambda j: (0, 0)),
                  pl.BlockSpec((D_MODEL, tn), lambda j: (0, j)),
                  pl.BlockSpec((1, tn), lambda j: (0, j))],
        out_specs=pl.BlockSpec((rows, tn), lambda j: (0, j)),
        out_shape=jax.ShapeDtypeStruct((rows, n), F32),
        compiler_params=_params("arbitrary"),
        name="adaln",
    )(cond, w_ada, b_ada)


def _s5prep_kernel(lre_ref, lim_ref, ldt_ref, bre_ref, bim_ref, cre_ref, cim_ref, at_ref, ab_ref, ca_ref, kk_ref):
    lre, lim = lre_ref[...], lim_ref[...]
    dt = jnp.exp(ldt_ref[...])
    mag = jnp.exp(lre * dt)
    a_re, a_im = mag * jnp.cos(lim * dt), mag * jnp.sin(lim * dt)
    den = lre * lre + lim * lim
    f_re = ((a_re - 1) * lre + a_im * lim) / den
    f_im = (a_im * lre - (a_re - 1) * lim) / den
    b_re, b_im = bre_ref[...], bim_ref[...]
    c_re, c_im = cre_ref[...], cim_ref[...]

    def times_a(z_re, z_im):
        return a_re * z_re - a_im * z_im, a_re * z_im + a_im * z_re

    def over_states(x, y):
        return lax.dot_general(x, y, (((2,), (2,)), ((0,), (0,))), precision=lax.Precision.HIGHEST,
                               preferred_element_type=F32)

    p_re, p_im = f_re * b_re - f_im * b_im, f_re * b_im + f_im * b_re
    q_re, q_im = c_re, c_im
    w_re, w_im = jnp.ones_like(a_re), jnp.zeros_like(a_re)
    for k in range(S5_T):
        ab_ref[0, k] = p_re
        ab_ref[1, k] = p_im
        kk_ref[k] = over_states(c_re, p_re) - over_states(c_im, p_im)
        p_re, p_im = times_a(p_re, p_im)
        q_re, q_im = times_a(q_re, q_im)
        w_re, w_im = times_a(w_re, w_im)
        ca_ref[0, k] = q_re
        ca_ref[1, k] = q_im
    at_ref[0] = w_re
    at_ref[1] = w_im


def _s5prep(lam_re, lam_im, log_dt, b_re, b_im, c_re, c_im):
    n = 2 * S5_GROUPS
    full = (n, S5_GROUP, S5_STATE)
    per_state = lambda t: jnp.broadcast_to(t.reshape(n, 1, S5_STATE), full)
    ldt = jnp.broadcast_to(log_dt.reshape(n, 1, 1), full)
    chan_state = lambda t: t.reshape(full)
    shp = lambda *lead: jax.ShapeDtypeStruct(lead + full, F32)
    at, ab, ca, kk = pl.pallas_call(
        _s5prep_kernel,
        out_shape=(shp(2), shp(2, S5_T), shp(2, S5_T), jax.ShapeDtypeStruct((S5_T, n, S5_GROUP, S5_GROUP), F32)),
        compiler_params=pltpu.CompilerParams(vmem_limit_bytes=VMEM_LIMIT),
        name="s5prep",
    )(per_state(lam_re), per_state(lam_im), ldt, chan_state(jnp.swapaxes(b_re, -1, -2)),
      chan_state(jnp.swapaxes(b_im, -1, -2)), chan_state(c_re), chan_state(c_im))
    by_dir = lambda t, lead: t.reshape(lead + (2, S5_GROUPS) + t.shape[-2:])
    return by_dir(at, (2,))[:, :, :, 0], by_dir(ab, (2, S5_T)), by_dir(ca, (2, S5_T)), by_dir(kk, (S5_T,))


def _spread_groups(x, rows_per_group, col_width, n_col_blocks):
    g8 = GROUPS_PER_TILE
    spread = np.kron(np.eye(n_col_blocks), np.kron(np.ones((1, g8)), np.eye(col_width))).astype(np.float32)
    row_group = (np.arange(x.shape[1]) // rows_per_group) % g8
    col_group = (np.arange(spread.shape[1]) // col_width) % g8
    mask = row_group[:, None] == col_group[None, :]
    tiled = jnp.einsum("lrk,kc->lrc", x, spread, precision=lax.Precision.HIGHEST)
    return jnp.where(mask[None], tiled, 0.0).astype(BF16)


def _s5_matrices(ab, ca, kk):
    t_, g8, nt = S5_T, GROUPS_PER_TILE, N_LANE_TILES
    w_all, v_all, m_all = [], [], []
    for d in range(2):
        walked = (lambda s: s) if d == 0 else (lambda s: t_ - 1 - s)
        ab_d = jnp.stack([ab[:, t_ - 1 - walked(s), d] for s in range(t_)], axis=1)
        ab_d = ab_d.reshape(2, t_, nt, g8, S5_GROUP, S5_STATE).transpose(2, 1, 3, 4, 0, 5)
        w_all.append(_spread_groups(ab_d.reshape(nt, t_ * LANES, 2 * S5_STATE), S5_GROUP, S5_STATE, 2))
        ca_d = jnp.stack([ca[:, walked(t), d] for t in range(t_)], axis=1)
        ca_d = ca_d * jnp.array([1.0, -1.0], F32).reshape(2, 1, 1, 1, 1)
        ca_d = ca_d.reshape(2, t_, nt, g8, S5_GROUP, S5_STATE).transpose(2, 0, 3, 5, 1, 4)
        v_all.append(_spread_groups(ca_d.reshape(nt, 2 * TILE_COLS, t_ * S5_GROUP), S5_STATE, S5_GROUP, t_))
        zero = jnp.zeros_like(kk[0, d])
        blocks = jnp.stack([jnp.stack([kk[walked(t) - walked(s), d] if walked(t) >= walked(s) else zero
                                       for t in range(t_)], axis=0) for s in range(t_)], axis=0)
        blocks = blocks.reshape(t_, t_, nt, g8, S5_GROUP, S5_GROUP).transpose(2, 0, 3, 5, 1, 4)
        m_all.append(_spread_groups(blocks.reshape(nt, t_ * LANES, t_ * S5_GROUP), S5_GROUP, S5_GROUP, t_))
    return jnp.stack(w_all), jnp.stack(v_all), jnp.stack(m_all)


def _rope_tile(t, c, sa, sb):
    return t * c + pltpu.roll(t, LANES - ROPE_HALF, 1) * sa + pltpu.roll(t, ROPE_HALF, 1) * sb


def _inproj_kernel(with_q, x_ref, sh_ref, sc_ref, win_ref, qg_ref, kvg_ref, wuq_ref, wuk_ref, wuv_ref,
                   c_ref, sa_ref, sb_ref, *out_refs):
    if with_q:
        q_ref, k_ref, v_ref, u_ref = out_refs
    else:
        k_ref, v_ref, u_ref = out_refs
    x = x_ref[0]
    xm = _layer_norm(x) * (1.0 + sc_ref[0]) + sh_ref[0]
    p = _dot(xm.astype(BF16), win_ref[...])
    c, sa, sb = c_ref[...], sa_ref[...], sb_ref[...]

    kv_c = _rms_norm(p[:, Q_RANK:KV_END], kvg_ref[...]).astype(BF16)
    k = _dot(kv_c, wuk_ref[...])
    kr = _rope_tile(p[:, KV_END + S5_WIDTH:], c, sa, sb)
    for h in range(N_HEADS):
        k_ref[0, :, h * HEAD_PAD:(h + 1) * HEAD_PAD] = (k[:, h * HEAD_PAD:(h + 1) * HEAD_PAD] + kr).astype(BF16)
    v_ref[0] = lax.dot_general(wuv_ref[...], kv_c, _NT, preferred_element_type=F32).astype(BF16)
    u_ref[0] = p[:, KV_END:KV_END + S5_WIDTH]

    if with_q:
        q_c = _rms_norm(p[:, :Q_RANK], qg_ref[...]).astype(BF16)
        q = _dot(q_c, wuq_ref[...])
        cq, saq, sbq = c * Q_SCALE, sa * Q_SCALE, sb * Q_SCALE
        for h in range(N_HEADS):
            sl = slice(h * HEAD_PAD, (h + 1) * HEAD_PAD)
            q_ref[0, :, sl] = _rope_tile(q[:, sl], cq, saq, sbq).astype(BF16)


def _inproj(x, mod3, mod_row, w_in_r, qg, kvg, wuq, wuk, wuv, rope_c, rope_sa, rope_sb, with_q):
    b, l, _ = x.shape
    tm = min(TM, l)
    row = (lambda bi: bi) if mod_row is None else (lambda bi: mod_row)
    tok = lambda w: pl.BlockSpec((1, tm, w), lambda bi, i: (bi, i, 0))
    in_specs = [
        tok(D_MODEL),
        pl.BlockSpec((1, 1, D_MODEL), lambda bi, i: (row(bi), 0, 0)),
        pl.BlockSpec((1, 1, D_MODEL), lambda bi, i: (row(bi), 0, 1)),
        _const_spec(w_in_r.shape), _const_spec(qg.shape), _const_spec(kvg.shape),
        _const_spec(wuq.shape), _const_spec(wuk.shape), _const_spec(wuv.shape),
        pl.BlockSpec((tm, LANES), lambda bi, i: (i, 0)),
        pl.BlockSpec((tm, LANES), lambda bi, i: (i, 0)),
        pl.BlockSpec((tm, LANES), lambda bi, i: (i, 0)),
    ]
    out_specs = [tok(N_HEADS * HEAD_PAD), pl.BlockSpec((1, MLA_WIDTH, tm), lambda bi, i: (bi, 0, i)), tok(S5_WIDTH)]
    out_shape = [jax.ShapeDtypeStruct((b, l, N_HEADS * HEAD_PAD), BF16),
                 jax.ShapeDtypeStruct((b, MLA_WIDTH, l), BF16),
                 jax.ShapeDtypeStruct((b, l, S5_WIDTH), F32)]
    if with_q:
        out_specs = [tok(N_HEADS * HEAD_PAD)] + out_specs
        out_shape = [jax.ShapeDtypeStruct((b, l, N_HEADS * HEAD_PAD), BF16)] + out_shape
    return pl.pallas_call(
        functools.partial(_inproj_kernel, with_q),
        grid=(b, l // tm),
        in_specs=in_specs, out_specs=out_specs, out_shape=out_shape,
        compiler_params=_params("parallel", "parallel"),
        name="inproj_lat" if with_q else "inproj_ctx",
    )(x, mod3, mod3, w_in_r, qg, kvg, wuq, wuk, wuv, rope_c, rope_sa, rope_sb)


def _attn_kernel(q_ref, kc_ref, kl_ref, vc_ref, vl_ref, o_ref):
    def scores(h):
        ks = slice(h * HEAD_PAD, (h + 1) * HEAD_PAD)
        q = q_ref[0, :, ks]
        return (lax.dot_general(kc_ref[0, :, ks], q, _NT, preferred_element_type=F32),
                lax.dot_general(kl_ref[0, :, ks], q, _NT, preferred_element_type=F32))

    def with_ones(vt):
        return jnp.concatenate([vt, jnp.ones((DEN_ROWS, vt.shape[1]), BF16)], axis=0)

    outs = []
    s_next = scores(0)
    for h in range(HEADS_PER_STEP):
        s_c, s_l = s_next
        if h + 1 < HEADS_PER_STEP:
            s_next = scores(h + 1)
        vs = slice(h * V_DIM, (h + 1) * V_DIM)
        m = jnp.maximum(jnp.max(s_c, axis=0, keepdims=True), jnp.max(s_l, axis=0, keepdims=True))
        e_c = jnp.exp2(s_c - m).astype(BF16)
        e_l = jnp.exp2(s_l - m).astype(BF16)
        o_t = _dot(with_ones(vc_ref[0, vs, :]), e_c) + _dot(with_ones(vl_ref[0, vs, :]), e_l)
        outs.append(o_t[:V_DIM] / o_t[V_DIM:V_DIM + 1])
    o_ref[0] = jnp.concatenate(outs, axis=0).T.astype(o_ref.dtype)


def _attention(q, k_c, k_l, v_c, v_l):
    b, l, _ = q.shape
    lc = k_c.shape[1]
    kw, vw = HEADS_PER_STEP * HEAD_PAD, HEADS_PER_STEP * V_DIM
    return pl.pallas_call(
        _attn_kernel,
        grid=(b, N_HEADS // HEADS_PER_STEP, l // BQ),
        in_specs=[pl.BlockSpec((1, BQ, kw), lambda bi, hp, i: (bi, i, hp)),
                  pl.BlockSpec((1, lc, kw), lambda bi, hp, i: (bi, 0, hp), pipeline_mode=pl.Buffered(1)),
                  pl.BlockSpec((1, l, kw), lambda bi, hp, i: (bi, 0, hp), pipeline_mode=pl.Buffered(1)),
                  pl.BlockSpec((1, vw, lc), lambda bi, hp, i: (bi, hp, 0), pipeline_mode=pl.Buffered(1)),
                  pl.BlockSpec((1, vw, l), lambda bi, hp, i: (bi, hp, 0), pipeline_mode=pl.Buffered(1))],
        out_specs=pl.BlockSpec((1, BQ, vw), lambda bi, hp, i: (bi, i, hp)),
        out_shape=jax.ShapeDtypeStruct((b, l, MLA_WIDTH), BF16),
        compiler_params=_params("parallel", "parallel", "arbitrary"),
        name="attn",
    )(q, k_c, k_l, v_c, v_l)


def _s5scan_kernel(n_ctx_chunks, uc_ref, ul_ref, w_ref, a_ref, v_ref, m_ref, dskip_ref, y_ref, hs_ref, st_ref):
    d = pl.program_id(0)
    i = pl.program_id(1)

    @pl.when(i == 0)
    def _():
        st_ref[...] = jnp.zeros_like(st_ref)

    u_bt = jnp.where(i < n_ctx_chunks, uc_ref[...], ul_ref[...])
    u = jnp.swapaxes(u_bt, 0, 1).reshape(N_ROWS // SUBLANES, S5_T, SUBLANES, S5_WIDTH)
    u_pos = [u[:, s].reshape(N_ROWS, S5_WIDTH) for s in range(S5_T)]

    def group_lanes(lt):
        lanes = slice(lt * LANES, (lt + 1) * LANES)
        return jnp.concatenate([p[:, lanes] for p in u_pos], axis=1).astype(BF16)

    def tile_cols(lt):
        base = (lt // TILES_PER_SLAB) * 2 * SLAB + (lt % TILES_PER_SLAB) * TILE_COLS
        return slice(base, base + TILE_COLS), slice(base + SLAB, base + SLAB + TILE_COLS)

    for lt in range(N_LANE_TILES):
        re, im = tile_cols(lt)
        inc = _dot(group_lanes(lt), w_ref[0, lt])
        hs_ref[:, re] = inc[:, :TILE_COLS]
        hs_ref[:, im] = inc[:, TILE_COLS:]

    n_steps = N_ROWS // SUBLANES
    for cs in range(0, 2 * S5_COLS, 2 * SLAB):
        re = slice(cs, cs + SLAB)
        im = slice(cs + SLAB, cs + 2 * SLAB)
        a_re, a_im = a_ref[0, :, re], a_ref[0, :, im]

        def step(tt, carry, re=re, im=im, a_re=a_re, a_im=a_im):
            h_re, h_im = carry
            t = jnp.where(d == 0, tt, n_steps - 1 - tt)
            rows = pl.ds(pl.multiple_of(t * SUBLANES, SUBLANES), SUBLANES)
            n_re = a_re * h_re - a_im * h_im + hs_ref[rows, re]
            n_im = a_re * h_im + a_im * h_re + hs_ref[rows, im]
            hs_ref[rows, re] = h_re
            hs_ref[rows, im] = h_im
            return n_re, n_im

        h_re, h_im = lax.fori_loop(0, n_steps, step, (st_ref[:, re], st_ref[:, im]), unroll=4)
        st_ref[:, re] = h_re
        st_ref[:, im] = h_im

    for lt in range(N_LANE_TILES):
        re, im = tile_cols(lt)
        lanes = slice(lt * LANES, (lt + 1) * LANES)
        y = (_dot(hs_ref[:, re].astype(BF16), v_ref[0, lt, :TILE_COLS])
             + _dot(hs_ref[:, im].astype(BF16), v_ref[0, lt, TILE_COLS:])
             + _dot(group_lanes(lt), m_ref[0, lt]))
        skip = dskip_ref[0, :, lanes]
        y_pos = [(y[:, s * LANES:(s + 1) * LANES] + u_pos[s][:, lanes] * skip)
                 .reshape(N_ROWS // SUBLANES, 1, SUBLANES, LANES) for s in range(S5_T)]
        y_tb = jnp.concatenate(y_pos, axis=1).reshape(TC, SUBLANES, LANES)
        y_ref[0, :, :, lanes] = jnp.swapaxes(y_tb, 0, 1)


def _s5scan(u_c, u_l, w_mat, a_b, v_mat, m_mat, dskip):
    b, l, _ = u_l.shape
    nc, nl = u_c.shape[1] // TC, l // TC

    def ctx_blk(d, i):
        return jnp.clip(jnp.where(d == 0, i, nc - 1 - i), 0, nc - 1)

    def lat_blk(d, i):
        return jnp.clip(jnp.where(d == 0, i - nc, nl - 1 - (i - nc)), 0, nl - 1)

    def out_blk(d, i):
        return jnp.where(i < nc, nl + ctx_blk(d, i), lat_blk(d, i))

    return pl.pallas_call(
        functools.partial(_s5scan_kernel, nc),
        grid=(2, nc + nl),
        in_specs=[pl.BlockSpec((b, TC, S5_WIDTH), lambda d, i: (0, ctx_blk(d, i), 0)),
                  pl.BlockSpec((b, TC, S5_WIDTH), lambda d, i: (0, lat_blk(d, i), 0)),
                  pl.BlockSpec((1, N_LANE_TILES, S5_T * LANES, 2 * TILE_COLS), lambda d, i: (d, 0, 0, 0)),
                  pl.BlockSpec((1, SUBLANES, 2 * S5_COLS), lambda d, i: (d, 0, 0)),
                  pl.BlockSpec((1, N_LANE_TILES, 2 * TILE_COLS, S5_T * LANES), lambda d, i: (d, 0, 0, 0)),
                  pl.BlockSpec((1, N_LANE_TILES, S5_T * LANES, S5_T * LANES), lambda d, i: (d, 0, 0, 0)),
                  pl.BlockSpec((1, 1, S5_WIDTH), lambda d, i: (d, 0, 0))],
        out_specs=pl.BlockSpec((1, b, TC, S5_WIDTH), lambda d, i: (d, 0, out_blk(d, i), 0)),
        out_shape=jax.ShapeDtypeStruct((2, b, (nc + nl) * TC, S5_WIDTH), F32),
        scratch_shapes=[pltpu.VMEM((N_ROWS, 2 * S5_COLS), F32),
                        pltpu.VMEM((SUBLANES, 2 * S5_COLS), F32)],
        compiler_params=_params("arbitrary", "arbitrary"),
        name="s5scan",
    )(u_c, u_l, w_mat, a_b, v_mat, m_mat, dskip)


def _tail_kernel(x_ref, att_ref, yf_ref, yb_ref, g1_ref, sh2_ref, sc2_ref, g2_ref, wglu_ref, bglu_ref, wout_ref,
                 ln1g_ref, ln1b_ref, wgu_ref, wd_ref, ln2g_ref, ln2b_ref, o_ref, acc_ref):
    z = jax.nn.gelu(yf_ref[0, 0] + yb_ref[0, 0])
    s5o = z * jax.nn.sigmoid(_dot(z.astype(BF16), wglu_ref[...]) + bglu_ref[...])
    mix = _dot(att_ref[0], wout_ref[:MLA_WIDTH, :]) + _dot(s5o.astype(BF16), wout_ref[MLA_WIDTH:, :])
    x1 = _layer_norm(DN_ALPHA * x_ref[0] + g1_ref[0] * mix) * ln1g_ref[...] + ln1b_ref[...]

    xm = (_layer_norm(x1) * (1.0 + sc2_ref[0]) + sh2_ref[0]).astype(BF16)
    for j in range(D_FF // FF_CHUNK):
        gate = _dot(xm, wgu_ref[:, j * FF_CHUNK:(j + 1) * FF_CHUNK])
        up = _dot(xm, wgu_ref[:, D_FF + j * FF_CHUNK:D_FF + (j + 1) * FF_CHUNK])
        hidden = (gate * jax.nn.sigmoid(gate) * up).astype(BF16)
        part = _dot(hidden, wd_ref[j * FF_CHUNK:(j + 1) * FF_CHUNK, :])
        if j == 0:
            acc_ref[...] = part
        else:
            acc_ref[...] += part
    o_ref[0] = _layer_norm(DN_ALPHA * x1 + g2_ref[0] * acc_ref[...]) * ln2g_ref[...] + ln2b_ref[...]


def _tail(x, att, y_dir, mod3, w_glu, b_glu, w_out, ln1_g, ln1_b, w_gu, w_down, ln2_g, ln2_b):
    b, l, _ = x.shape
    tok = lambda w: pl.BlockSpec((1, TM, w), lambda bi, i: (bi, i, 0))
    mod = lambda j: pl.BlockSpec((1, 1, D_MODEL), lambda bi, i: (bi, 0, j))
    y_of = lambda d: pl.BlockSpec((1, 1, TM, S5_WIDTH), lambda bi, i: (d, bi, i, 0))
    consts = (w_glu, b_glu, w_out, ln1_g, ln1_b, w_gu, w_down, ln2_g, ln2_b)
    return pl.pallas_call(
        _tail_kernel,
        grid=(b, l // TM),
        in_specs=[tok(D_MODEL), tok(MLA_WIDTH), y_of(0), y_of(1), mod(2), mod(3), mod(4), mod(5)]
                 + [_const_spec(w.shape) for w in consts],
        out_specs=tok(D_MODEL),
        out_shape=jax.ShapeDtypeStruct((b, l, D_MODEL), F32),
        scratch_shapes=[pltpu.VMEM((TM, D_MODEL), F32)],
        compiler_params=_params("parallel", "parallel"),
        name="tail",
    )(x, att, y_dir, y_dir, mod3, mod3, mod3, mod3, *consts)


def _rope_tables(seq):
    pos = jnp.arange(seq)
    row = (pos // GRID_W).astype(F32)
    col = (pos % GRID_W).astype(F32)
    n_freq = ROPE // 4
    freqs = ROPE_THETA ** (-jnp.arange(n_freq, dtype=F32) / n_freq)
    ang = jnp.concatenate([row[:, None] * freqs, col[:, None] * freqs], axis=-1)
    cos, sin = jnp.cos(ang), jnp.sin(ang)
    zeros = lambda w: jnp.zeros((seq, w), F32)
    tail = LANES - NOPE - ROPE
    c = jnp.concatenate([jnp.ones((seq, NOPE), F32), cos, cos, zeros(tail)], axis=-1)
    sa = jnp.concatenate([zeros(NOPE), -sin, zeros(ROPE_HALF), zeros(tail)], axis=-1)
    sb = jnp.concatenate([zeros(NOPE), zeros(ROPE_HALF), sin, zeros(tail)], axis=-1)
    return c, sa, sb


def _pair_split(w):
    return jnp.concatenate([w[..., 0::2], w[..., 1::2]], axis=-1)


def kernel(x, c, ctx, c_ctx, w_ada, b_ada, w_in, q_norm_g, kv_norm_g, w_uq, w_uk, w_uv, s5_lambda_re, s5_lambda_im, s5_log_dt, s5_b_re, s5_b_im, s5_c_re, s5_c_im, s5_d, s5_w_glu, s5_b_glu, w_out, ln1_g, ln1_b, w_gate_up, w_down, ln2_g, ln2_b):
    assert w_ada.shape[0] == DEPTH == 1
    b, l, _ = x.shape
    assert b == SUBLANES, "the S5 scan maps the batch onto the sublanes of one vreg row"
    lc = ctx.shape[1]
    row2 = lambda t: t.reshape(1, -1)

    cond = jnp.concatenate([c, c_ctx[None, :], jnp.zeros((2 * SUBLANES - b - 1, D_MODEL), F32)], axis=0)
    mod = _adaln(cond, w_ada[0], row2(b_ada[0]))
    mod3 = mod.reshape(mod.shape[0], 1, 6 * D_MODEL)

    wi = w_in[0]
    rope_tile = jnp.zeros((D_MODEL, LANES), F32).at[:, NOPE:NOPE + ROPE].set(_pair_split(wi[:, KV_END:ROPE_END]))
    w_in_r = jnp.concatenate([wi[:, :KV_END], wi[:, ROPE_END:], rope_tile], axis=1).astype(BF16)
    uq = w_uq[0]
    uq = jnp.concatenate([uq[..., :NOPE], _pair_split(uq[..., NOPE:]),
                          jnp.zeros((Q_RANK, N_HEADS, HEAD_PAD - NOPE - ROPE), F32)], axis=-1)
    wuq = uq.reshape(Q_RANK, N_HEADS * HEAD_PAD).astype(BF16)
    uk = jnp.concatenate([w_uk[0], jnp.zeros((KV_RANK, N_HEADS, HEAD_PAD - NOPE), F32)], axis=-1)
    wuk = uk.reshape(KV_RANK, N_HEADS * HEAD_PAD).astype(BF16)
    wuv = w_uv[0].reshape(KV_RANK, MLA_WIDTH).T.astype(BF16)
    qg, kvg = row2(q_norm_g[0]), row2(kv_norm_g[0])

    rope_c, rope_sa, rope_sb = _rope_tables(l)
    flat_c = jnp.concatenate([jnp.ones((lc, NOPE + ROPE), F32), jnp.zeros((lc, LANES - NOPE - ROPE), F32)], axis=-1)
    flat_s = jnp.zeros((lc, LANES), F32)

    q, k_l, v_l, u_l = _inproj(x, mod3, None, w_in_r, qg, kvg, wuq, wuk, wuv, rope_c, rope_sa, rope_sb, True)
    k_c, v_c, u_c = _inproj(ctx, mod3, b, w_in_r, qg, kvg, wuq, wuk, wuv, flat_c, flat_s, flat_s, False)

    att = _attention(q, k_c, k_l, v_c, v_l)

    a_t, ab, ca, kk = _s5prep(s5_lambda_re[0], s5_lambda_im[0], s5_log_dt[0], s5_b_re[0], s5_b_im[0],
                              s5_c_re[0], s5_c_im[0])
    w_mat, v_mat, m_mat = _s5_matrices(ab, ca, kk)
    slabs = lambda t: t.reshape(2, S5_COLS // SLAB, 1, SLAB)
    a_cat = jnp.concatenate([slabs(a_t[0]), slabs(a_t[1])], axis=2).reshape(2, 2 * S5_COLS)
    a_b = jnp.broadcast_to(a_cat[:, None, :], (2, SUBLANES, 2 * S5_COLS))
    dskip = jnp.stack([s5_d[0], jnp.zeros_like(s5_d[0])])[:, None, :]
    y_dir = _s5scan(u_c, u_l, w_mat, a_b, v_mat, m_mat, dskip)

    return _tail(x, att, y_dir, mod3, s5_w_glu[0].astype(BF16), row2(s5_b_glu[0]), w_out[0].astype(BF16),
                 row2(ln1_g[0]), row2(ln1_b[0]), w_gate_up[0].astype(BF16), w_down[0].astype(BF16),
                 row2(ln2_g[0]), row2(ln2_b[0]))
```

```python
import functools
import math

import jax
import jax.numpy as jnp
import numpy as np
from jax import lax
from jax.experimental import pallas as pl
from jax.experimental.pallas import tpu as pltpu

D_MODEL = 1024
GRID_W = 64
N_HEADS = 8
NOPE = 64
ROPE = 32
V_DIM = 64
Q_RANK = 384
KV_RANK = 256
MLA_WIDTH = N_HEADS * V_DIM
S5_WIDTH = D_MODEL - MLA_WIDTH
S5_GROUP = 16
S5_GROUPS = S5_WIDTH // S5_GROUP
S5_STATE = 64
S5_COLS = S5_GROUPS * S5_STATE
KV_END = Q_RANK + KV_RANK
ROPE_END = KV_END + ROPE
D_FF = 2816
ROPE_THETA = 10000.0
NORM_EPS = 1e-6
DEPTH = 1
DN_ALPHA = (2.0 * DEPTH) ** 0.25

LANES = 128
SUBLANES = 8
HEAD_PAD = LANES
ROPE_HALF = ROPE // 2
P_COLS = Q_RANK + KV_RANK + S5_WIDTH + LANES
VMEM_LIMIT = 56 * 1024 * 1024

TM = 512
BQ = 512
DEN_ROWS = 16
HEADS_PER_STEP = 8
Q_SCALE = (NOPE + ROPE) ** -0.5 * math.log2(math.e)
TC = 128
S5_T = 4
N_ROWS = TC // S5_T * SUBLANES
N_LANE_TILES = S5_WIDTH // LANES
GROUPS_PER_TILE = LANES // S5_GROUP
TILE_COLS = GROUPS_PER_TILE * S5_STATE
TILES_PER_SLAB = 2
SLAB = TILES_PER_SLAB * TILE_COLS
FF_CHUNK = 256

F32 = jnp.float32
BF16 = jnp.bfloat16


def _params(*sem):
    return pltpu.CompilerParams(dimension_semantics=sem, vmem_limit_bytes=VMEM_LIMIT)


def _const_spec(shape):
    nd = len(shape)
    return pl.BlockSpec(shape, lambda *_: (0,) * nd, pipeline_mode=pl.Buffered(1))


def _layer_norm(x):
    mu = jnp.mean(x, axis=-1, keepdims=True)
    xc = x - mu
    var = jnp.mean(xc * xc, axis=-1, keepdims=True)
    return xc * lax.rsqrt(var + NORM_EPS)


def _rms_norm(x, g):
    return x * lax.rsqrt(jnp.mean(x * x, axis=-1, keepdims=True) + NORM_EPS) * g


def _dot(a, b):
    return jnp.dot(a, b, preferred_element_type=F32)


_NT = (((1,), (1,)), ((), ()))


def _adaln_kernel(cond_ref, w_ref, b_ref, o_ref):
    cnd = cond_ref[...]
    act = cnd * jax.nn.sigmoid(cnd)
    o_ref[...] = _dot(act.astype(BF16), w_ref[...].astype(BF16)) + b_ref[...]


def _adaln(cond, w_ada, b_ada):
    rows, n = cond.shape[0], w_ada.shape[1]
    tn = D_MODEL
    return pl.pallas_call(
        _adaln_kernel,
        grid=(n // tn,),
        in_specs=[pl.BlockSpec((rows, D_MODEL), lambda j: (0, 0)),
                  pl.BlockSpec((D_MODEL, tn), lambda j: (0, j)),
                  pl.BlockSpec((1, tn), lambda j: (0, j))],
        out_specs=pl.BlockSpec((rows, tn), lambda j: (0, j)),
        out_shape=jax.ShapeDtypeStruct((rows, n), F32),
        compiler_params=_params("arbitrary"),
        name="adaln",
    )(cond, w_ada, b_ada)


def _s5prep_kernel(lre_ref, lim_ref, ldt_ref, bre_ref, bim_ref, cre_ref, cim_ref, at_ref, ab_ref, ca_ref, kk_ref):
    lre, lim = lre_ref[...], lim_ref[...]
    dt = jnp.exp(ldt_ref[...])
    mag = jnp.exp(lre * dt)
    a_re, a_im = mag * jnp.cos(lim * dt), mag * jnp.sin(lim * dt)
    den = lre * lre + lim * lim
    f_re = ((a_re - 1) * lre + a_im * lim) / den
    f_im = (a_im * lre - (a_re - 1) * lim) / den
    b_re, b_im = bre_ref[...], bim_ref[...]
    c_re, c_im = cre_ref[...], cim_ref[...]

    def times_a(z_re, z_im):
        return a_re * z_re - a_im * z_im, a_re * z_im + a_im * z_re

    def over_states(x, y):
        return lax.dot_general(x, y, (((2,), (2,)), ((0,), (0,))), precision=lax.Precision.HIGHEST,
                               preferred_element_type=F32)

    p_re, p_im = f_re * b_re - f_im * b_im, f_re * b_im + f_im * b_re
    q_re, q_im = c_re, c_im
    w_re, w_im = jnp.ones_like(a_re), jnp.zeros_like(a_re)
    for k in range(S5_T):
        ab_ref[0, k] = p_re
        ab_ref[1, k] = p_im
        kk_ref[k] = over_states(c_re, p_re) - over_states(c_im, p_im)
        p_re, p_im = times_a(p_re, p_im)
        q_re, q_im = times_a(q_re, q_im)
        w_re, w_im = times_a(w_re, w_im)
        ca_ref[0, k] = q_re
        ca_ref[1, k] = q_im
    at_ref[0] = w_re
    at_ref[1] = w_im


def _s5prep(lam_re, lam_im, log_dt, b_re, b_im, c_re, c_im):
    n = 2 * S5_GROUPS
    full = (n, S5_GROUP, S5_STATE)
    per_state = lambda t: jnp.broadcast_to(t.reshape(n, 1, S5_STATE), full)
    ldt = jnp.broadcast_to(log_dt.reshape(n, 1, 1), full)
    chan_state = lambda t: t.reshape(full)
    shp = lambda *lead: jax.ShapeDtypeStruct(lead + full, F32)
    at, ab, ca, kk = pl.pallas_call(
        _s5prep_kernel,
        out_shape=(shp(2), shp(2, S5_T), shp(2, S5_T), jax.ShapeDtypeStruct((S5_T, n, S5_GROUP, S5_GROUP), F32)),
        compiler_params=pltpu.CompilerParams(vmem_limit_bytes=VMEM_LIMIT),
        name="s5prep",
    )(per_state(lam_re), per_state(lam_im), ldt, chan_state(jnp.swapaxes(b_re, -1, -2)),
      chan_state(jnp.swapaxes(b_im, -1, -2)), chan_state(c_re), chan_state(c_im))
    by_dir = lambda t, lead: t.reshape(lead + (2, S5_GROUPS) + t.shape[-2:])
    return by_dir(at, (2,))[:, :, :, 0], by_dir(ab, (2, S5_T)), by_dir(ca, (2, S5_T)), by_dir(kk, (S5_T,))


def _spread_groups(x, rows_per_group, col_width, n_col_blocks):
    g8 = GROUPS_PER_TILE
    spread = np.kron(np.eye(n_col_blocks), np.kron(np.ones((1, g8)), np.eye(col_width))).astype(np.float32)
    row_group = (np.arange(x.shape[1]) // rows_per_group) % g8
    col_group = (np.arange(spread.shape[1]) // col_width) % g8
    mask = row_group[:, None] == col_group[None, :]
    tiled = jnp.einsum("lrk,kc->lrc", x, spread, precision=lax.Precision.HIGHEST)
    return jnp.where(mask[None], tiled, 0.0).astype(BF16)


def _s5_matrices(ab, ca, kk):
    t_, g8, nt = S5_T, GROUPS_PER_TILE, N_LANE_TILES
    w_all, v_all, m_all = [], [], []
    for d in range(2):
        walked = (lambda s: s) if d == 0 else (lambda s: t_ - 1 - s)
        ab_d = jnp.stack([ab[:, t_ - 1 - walked(s), d] for s in range(t_)], axis=1)
        ab_d = ab_d.reshape(2, t_, nt, g8, S5_GROUP, S5_STATE).transpose(2, 1, 3, 4, 0, 5)
        w_all.append(ab_d.reshape(nt, t_ * LANES, 2 * S5_STATE))
        ca_d = jnp.stack([ca[:, walked(t), d] for t in range(t_)], axis=1)
        ca_d = ca_d * jnp.array([1.0, -1.0], F32).reshape(2, 1, 1, 1, 1)
        ca_d = ca_d.reshape(2, t_, nt, g8, S5_GROUP, S5_STATE).transpose(2, 0, 3, 5, 1, 4)
        v_all.append(ca_d.reshape(nt, 2 * TILE_COLS, t_ * S5_GROUP))
        zero = jnp.zeros_like(kk[0, d])
        blocks = jnp.stack([jnp.stack([kk[walked(t) - walked(s), d] if walked(t) >= walked(s) else zero
                                       for t in range(t_)], axis=0) for s in range(t_)], axis=0)
        blocks = blocks.reshape(t_, t_, nt, g8, S5_GROUP, S5_GROUP).transpose(2, 0, 3, 5, 1, 4)
        m_all.append(blocks.reshape(nt, t_ * LANES, t_ * S5_GROUP))

    def spread(parts, *args):
        out = _spread_groups(jnp.concatenate(parts, axis=0), *args)
        return out.reshape((2, nt) + out.shape[1:])

    return (spread(w_all, S5_GROUP, S5_STATE, 2), spread(v_all, S5_STATE, S5_GROUP, t_),
            spread(m_all, S5_GROUP, S5_GROUP, t_))


def _rope_tile(t, c, sa, sb):
    return t * c + pltpu.roll(t, LANES - ROPE_HALF, 1) * sa + pltpu.roll(t, ROPE_HALF, 1) * sb


def _inproj_kernel(with_q, x_ref, sh_ref, sc_ref, win_ref, qg_ref, kvg_ref, wuq_ref, wuk_ref, wuv_ref,
                   c_ref, sa_ref, sb_ref, *out_refs):
    if with_q:
        q_ref, k_ref, v_ref, u_ref = out_refs
    else:
        k_ref, v_ref, u_ref = out_refs
    x = x_ref[0]
    xm = _layer_norm(x) * (1.0 + sc_ref[0]) + sh_ref[0]
    p = _dot(xm.astype(BF16), win_ref[...])
    c, sa, sb = c_ref[...], sa_ref[...], sb_ref[...]

    kv_c = _rms_norm(p[:, Q_RANK:KV_END], kvg_ref[...]).astype(BF16)
    k = _dot(kv_c, wuk_ref[...])
    kr = _rope_tile(p[:, KV_END + S5_WIDTH:], c, sa, sb)
    for h in range(N_HEADS):
        k_ref[0, :, h * HEAD_PAD:(h + 1) * HEAD_PAD] = (k[:, h * HEAD_PAD:(h + 1) * HEAD_PAD] + kr).astype(BF16)
    v_ref[0] = lax.dot_general(wuv_ref[...], kv_c, _NT, preferred_element_type=F32).astype(BF16)
    u_ref[0] = p[:, KV_END:KV_END + S5_WIDTH]

    if with_q:
        q_c = _rms_norm(p[:, :Q_RANK], qg_ref[...]).astype(BF16)
        q = _dot(q_c, wuq_ref[...])
        cq, saq, sbq = c * Q_SCALE, sa * Q_SCALE, sb * Q_SCALE
        for h in range(N_HEADS):
            sl = slice(h * HEAD_PAD, (h + 1) * HEAD_PAD)
            q_ref[0, :, sl] = _rope_tile(q[:, sl], cq, saq, sbq).astype(BF16)


def _inproj(x, mod3, mod_row, w_in_r, qg, kvg, wuq, wuk, wuv, rope_c, rope_sa, rope_sb, with_q):
    b, l, _ = x.shape
    tm = min(TM, l)
    row = (lambda bi: bi) if mod_row is None else (lambda bi: mod_row)
    tok = lambda w: pl.BlockSpec((1, tm, w), lambda bi, i: (bi, i, 0))
    in_specs = [
        tok(D_MODEL),
        pl.BlockSpec((1, 1, D_MODEL), lambda bi, i: (row(bi), 0, 0)),
        pl.BlockSpec((1, 1, D_MODEL), lambda bi, i: (row(bi), 0, 1)),
        _const_spec(w_in_r.shape), _const_spec(qg.shape), _const_spec(kvg.shape),
        _const_spec(wuq.shape), _const_spec(wuk.shape), _const_spec(wuv.shape),
        pl.BlockSpec((tm, LANES), lambda bi, i: (i, 0)),
        pl.BlockSpec((tm, LANES), lambda bi, i: (i, 0)),
        pl.BlockSpec((tm, LANES), lambda bi, i: (i, 0)),
    ]
    out_specs = [tok(N_HEADS * HEAD_PAD), pl.BlockSpec((1, MLA_WIDTH, tm), lambda bi, i: (bi, 0, i)), tok(S5_WIDTH)]
    out_shape = [jax.ShapeDtypeStruct((b, l, N_HEADS * HEAD_PAD), BF16),
                 jax.ShapeDtypeStruct((b, MLA_WIDTH, l), BF16),
                 jax.ShapeDtypeStruct((b, l, S5_WIDTH), F32)]
    if with_q:
        out_specs = [tok(N_HEADS * HEAD_PAD)] + out_specs
        out_shape = [jax.ShapeDtypeStruct((b, l, N_HEADS * HEAD_PAD), BF16)] + out_shape
    return pl.pallas_call(
        functools.partial(_inproj_kernel, with_q),
        grid=(b, l // tm),
        in_specs=in_specs, out_specs=out_specs, out_shape=out_shape,
        compiler_params=_params("parallel", "parallel"),
        name="inproj_lat" if with_q else "inproj_ctx",
    )(x, mod3, mod3, w_in_r, qg, kvg, wuq, wuk, wuv, rope_c, rope_sa, rope_sb)


def _attn_kernel(q_ref, kc_ref, kl_ref, vc_ref, vl_ref, o_ref):
    def scores(h):
        ks = slice(h * HEAD_PAD, (h + 1) * HEAD_PAD)
        q = q_ref[0, :, ks]
        return (lax.dot_general(kc_ref[0, :, ks], q, _NT, preferred_element_type=F32),
                lax.dot_general(kl_ref[0, :, ks], q, _NT, preferred_element_type=F32))

    def with_ones(vt):
        return jnp.concatenate([vt, jnp.ones((DEN_ROWS, vt.shape[1]), BF16)], axis=0)

    outs = []
    s_next = scores(0)
    for h in range(HEADS_PER_STEP):
        s_c, s_l = s_next
        if h + 1 < HEADS_PER_STEP:
            s_next = scores(h + 1)
        vs = slice(h * V_DIM, (h + 1) * V_DIM)
        m = jnp.maximum(jnp.max(s_c, axis=0, keepdims=True), jnp.max(s_l, axis=0, keepdims=True))
        e_c = jnp.exp2(s_c - m).astype(BF16)
        e_l = jnp.exp2(s_l - m).astype(BF16)
        o_t = _dot(with_ones(vc_ref[0, vs, :]), e_c) + _dot(with_ones(vl_ref[0, vs, :]), e_l)
        outs.append(o_t[:V_DIM] / o_t[V_DIM:V_DIM + 1])
    o_ref[0] = jnp.concatenate(outs, axis=0).T.astype(o_ref.dtype)


def _attention(q, k_c, k_l, v_c, v_l):
    b, l, _ = q.shape
    lc = k_c.shape[1]
    kw, vw = HEADS_PER_STEP * HEAD_PAD, HEADS_PER_STEP * V_DIM
    return pl.pallas_call(
        _attn_kernel,
        grid=(b, N_HEADS // HEADS_PER_STEP, l // BQ),
        in_specs=[pl.BlockSpec((1, BQ, kw), lambda bi, hp, i: (bi, i, hp)),
                  pl.BlockSpec((1, lc, kw), lambda bi, hp, i: (bi, 0, hp)),
                  pl.BlockSpec((1, l, kw), lambda bi, hp, i: (bi, 0, hp)),
                  pl.BlockSpec((1, vw, lc), lambda bi, hp, i: (bi, hp, 0)),
                  pl.BlockSpec((1, vw, l), lambda bi, hp, i: (bi, hp, 0))],
        out_specs=pl.BlockSpec((1, BQ, vw), lambda bi, hp, i: (bi, i, hp)),
        out_shape=jax.ShapeDtypeStruct((b, l, MLA_WIDTH), BF16),
        compiler_params=_params("parallel", "parallel", "arbitrary"),
        name="attn",
    )(q, k_c, k_l, v_c, v_l)


def _s5scan_kernel(n_ctx_chunks, uc_ref, ul_ref, w_ref, a_ref, v_ref, m_ref, dskip_ref, y_ref, hs_ref, st_ref):
    d = pl.program_id(0)
    i = pl.program_id(1)

    @pl.when(i == 0)
    def _():
        st_ref[...] = jnp.zeros_like(st_ref)

    u_bt = jnp.where(i < n_ctx_chunks, uc_ref[...], ul_ref[...])
    u = jnp.swapaxes(u_bt, 0, 1).reshape(N_ROWS // SUBLANES, S5_T, SUBLANES, S5_WIDTH)
    u_pos = [u[:, s].reshape(N_ROWS, S5_WIDTH) for s in range(S5_T)]

    def group_lanes(lt):
        lanes = slice(lt * LANES, (lt + 1) * LANES)
        return jnp.concatenate([p[:, lanes] for p in u_pos], axis=1).astype(BF16)

    def tile_cols(lt):
        base = (lt // TILES_PER_SLAB) * 2 * SLAB + (lt % TILES_PER_SLAB) * TILE_COLS
        return slice(base, base + TILE_COLS), slice(base + SLAB, base + SLAB + TILE_COLS)

    for lt in range(N_LANE_TILES):
        re, im = tile_cols(lt)
        inc = _dot(group_lanes(lt), w_ref[0, lt])
        hs_ref[:, re] = inc[:, :TILE_COLS]
        hs_ref[:, im] = inc[:, TILE_COLS:]

    n_steps = N_ROWS // SUBLANES
    for cs in range(0, 2 * S5_COLS, 2 * SLAB):
        re = slice(cs, cs + SLAB)
        im = slice(cs + SLAB, cs + 2 * SLAB)
        a_re, a_im = a_ref[0, :, re], a_ref[0, :, im]

        def step(tt, carry, re=re, im=im, a_re=a_re, a_im=a_im):
            h_re, h_im = carry
            t = jnp.where(d == 0, tt, n_steps - 1 - tt)
            rows = pl.ds(pl.multiple_of(t * SUBLANES, SUBLANES), SUBLANES)
            n_re = a_re * h_re - a_im * h_im + hs_ref[rows, re]
            n_im = a_re * h_im + a_im * h_re + hs_ref[rows, im]
            hs_ref[rows, re] = h_re
            hs_ref[rows, im] = h_im
            return n_re, n_im

        h_re, h_im = lax.fori_loop(0, n_steps, step, (st_ref[:, re], st_ref[:, im]), unroll=4)
        st_ref[:, re] = h_re
        st_ref[:, im] = h_im

    for lt in range(N_LANE_TILES):
        re, im = tile_cols(lt)
        lanes = slice(lt * LANES, (lt + 1) * LANES)
        y = (_dot(hs_ref[:, re].astype(BF16), v_ref[0, lt, :TILE_COLS])
             + _dot(hs_ref[:, im].astype(BF16), v_ref[0, lt, TILE_COLS:])
             + _dot(group_lanes(lt), m_ref[0, lt]))
        skip = dskip_ref[0, :, lanes]
        y_pos = [(y[:, s * LANES:(s + 1) * LANES] + u_pos[s][:, lanes] * skip)
                 .reshape(N_ROWS // SUBLANES, 1, SUBLANES, LANES) for s in range(S5_T)]
        y_tb = jnp.concatenate(y_pos, axis=1).reshape(TC, SUBLANES, LANES)
        y_ref[0, :, :, lanes] = jnp.swapaxes(y_tb, 0, 1)


def _s5scan(u_c, u_l, w_mat, a_b, v_mat, m_mat, dskip):
    b, l, _ = u_l.shape
    nc, nl = u_c.shape[1] // TC, l // TC

    def ctx_blk(d, i):
        return jnp.clip(jnp.where(d == 0, i, nc - 1 - i), 0, nc - 1)

    def lat_blk(d, i):
        return jnp.clip(jnp.where(d == 0, i - nc, nl - 1 - (i - nc)), 0, nl - 1)

    def out_blk(d, i):
        return jnp.where(i < nc, nl + ctx_blk(d, i), lat_blk(d, i))

    return pl.pallas_call(
        functools.partial(_s5scan_kernel, nc),
        grid=(2, nc + nl),
        in_specs=[pl.BlockSpec((b, TC, S5_WIDTH), lambda d, i: (0, ctx_blk(d, i), 0)),
                  pl.BlockSpec((b, TC, S5_WIDTH), lambda d, i: (0, lat_blk(d, i), 0)),
                  pl.BlockSpec((1, N_LANE_TILES, S5_T * LANES, 2 * TILE_COLS), lambda d, i: (d, 0, 0, 0)),
                  pl.BlockSpec((1, SUBLANES, 2 * S5_COLS), lambda d, i: (d, 0, 0)),
                  pl.BlockSpec((1, N_LANE_TILES, 2 * TILE_COLS, S5_T * LANES), lambda d, i: (d, 0, 0, 0)),
                  pl.BlockSpec((1, N_LANE_TILES, S5_T * LANES, S5_T * LANES), lambda d, i: (d, 0, 0, 0)),
                  pl.BlockSpec((1, 1, S5_WIDTH), lambda d, i: (d, 0, 0))],
        out_specs=pl.BlockSpec((1, b, TC, S5_WIDTH), lambda d, i: (d, 0, out_blk(d, i), 0)),
        out_shape=jax.ShapeDtypeStruct((2, b, (nc + nl) * TC, S5_WIDTH), F32),
        scratch_shapes=[pltpu.VMEM((N_ROWS, 2 * S5_COLS), F32),
                        pltpu.VMEM((SUBLANES, 2 * S5_COLS), F32)],
        compiler_params=_params("arbitrary", "arbitrary"),
        name="s5scan",
    )(u_c, u_l, w_mat, a_b, v_mat, m_mat, dskip)


def _tail_kernel(x_ref, att_ref, yf_ref, yb_ref, g1_ref, sh2_ref, sc2_ref, g2_ref, wglu_ref, bglu_ref, wout_ref,
                 ln1g_ref, ln1b_ref, wgu_ref, wd_ref, ln2g_ref, ln2b_ref, o_ref, acc_ref):
    z = jax.nn.gelu(yf_ref[0, 0] + yb_ref[0, 0])
    s5o = z * jax.nn.sigmoid(_dot(z.astype(BF16), wglu_ref[...]) + bglu_ref[...])
    mix = _dot(att_ref[0], wout_ref[:MLA_WIDTH, :]) + _dot(s5o.astype(BF16), wout_ref[MLA_WIDTH:, :])
    x1 = _layer_norm(DN_ALPHA * x_ref[0] + g1_ref[0] * mix) * ln1g_ref[...] + ln1b_ref[...]

    xm = (_layer_norm(x1) * (1.0 + sc2_ref[0]) + sh2_ref[0]).astype(BF16)
    for j in range(D_FF // FF_CHUNK):
        gate = _dot(xm, wgu_ref[:, j * FF_CHUNK:(j + 1) * FF_CHUNK])
        up = _dot(xm, wgu_ref[:, D_FF + j * FF_CHUNK:D_FF + (j + 1) * FF_CHUNK])
        hidden = (gate * jax.nn.sigmoid(gate) * up).astype(BF16)
        part = _dot(hidden, wd_ref[j * FF_CHUNK:(j + 1) * FF_CHUNK, :])
        if j == 0:
            acc_ref[...] = part
        else:
            acc_ref[...] += part
    o_ref[0] = _layer_norm(DN_ALPHA * x1 + g2_ref[0] * acc_ref[...]) * ln2g_ref[...] + ln2b_ref[...]


def _tail(x, att, y_dir, mod3, w_glu, b_glu, w_out, ln1_g, ln1_b, w_gu, w_down, ln2_g, ln2_b):
    b, l, _ = x.shape
    tok = lambda w: pl.BlockSpec((1, TM, w), lambda bi, i: (bi, i, 0))
    mod = lambda j: pl.BlockSpec((1, 1, D_MODEL), lambda bi, i: (bi, 0, j))
    y_of = lambda d: pl.BlockSpec((1, 1, TM, S5_WIDTH), lambda bi, i: (d, bi, i, 0))
    consts = (w_glu, b_glu, w_out, ln1_g, ln1_b, w_gu, w_down, ln2_g, ln2_b)
    return pl.pallas_call(
        _tail_kernel,
        grid=(b, l // TM),
        in_specs=[tok(D_MODEL), tok(MLA_WIDTH), y_of(0), y_of(1), mod(2), mod(3), mod(4), mod(5)]
                 + [_const_spec(w.shape) for w in consts],
        out_specs=tok(D_MODEL),
        out_shape=jax.ShapeDtypeStruct((b, l, D_MODEL), F32),
        scratch_shapes=[pltpu.VMEM((TM, D_MODEL), F32)],
        compiler_params=_params("parallel", "parallel"),
        name="tail",
    )(x, att, y_dir, y_dir, mod3, mod3, mod3, mod3, *consts)


def _rope_tables(seq):
    pos = jnp.arange(seq)
    row = (pos // GRID_W).astype(F32)
    col = (pos % GRID_W).astype(F32)
    n_freq = ROPE // 4
    freqs = ROPE_THETA ** (-jnp.arange(n_freq, dtype=F32) / n_freq)
    ang = jnp.concatenate([row[:, None] * freqs, col[:, None] * freqs], axis=-1)
    cos, sin = jnp.cos(ang), jnp.sin(ang)
    zeros = lambda w: jnp.zeros((seq, w), F32)
    tail = LANES - NOPE - ROPE
    c = jnp.concatenate([jnp.ones((seq, NOPE), F32), cos, cos, zeros(tail)], axis=-1)
    sa = jnp.concatenate([zeros(NOPE), -sin, zeros(ROPE_HALF), zeros(tail)], axis=-1)
    sb = jnp.concatenate([zeros(NOPE), zeros(ROPE_HALF), sin, zeros(tail)], axis=-1)
    return c, sa, sb


def _pair_split(w):
    return jnp.concatenate([w[..., 0::2], w[..., 1::2]], axis=-1)


def kernel(x, c, ctx, c_ctx, w_ada, b_ada, w_in, q_norm_g, kv_norm_g, w_uq, w_uk, w_uv, s5_lambda_re, s5_lambda_im, s5_log_dt, s5_b_re, s5_b_im, s5_c_re, s5_c_im, s5_d, s5_w_glu, s5_b_glu, w_out, ln1_g, ln1_b, w_gate_up, w_down, ln2_g, ln2_b):
    assert w_ada.shape[0] == DEPTH == 1
    b, l, _ = x.shape
    assert b == SUBLANES, "the S5 scan maps the batch onto the sublanes of one vreg row"
    lc = ctx.shape[1]
    row2 = lambda t: t.reshape(1, -1)

    cond = jnp.concatenate([c, c_ctx[None, :], jnp.zeros((2 * SUBLANES - b - 1, D_MODEL), F32)], axis=0)
    mod = _adaln(cond, w_ada[0], row2(b_ada[0]))
    mod3 = mod.reshape(mod.shape[0], 1, 6 * D_MODEL)

    wi = w_in[0]
    rope_tile = jnp.zeros((D_MODEL, LANES), F32).at[:, NOPE:NOPE + ROPE].set(_pair_split(wi[:, KV_END:ROPE_END]))
    w_in_r = jnp.concatenate([wi[:, :KV_END], wi[:, ROPE_END:], rope_tile], axis=1).astype(BF16)
    uq = w_uq[0]
    uq = jnp.concatenate([uq[..., :NOPE], _pair_split(uq[..., NOPE:]),
                          jnp.zeros((Q_RANK, N_HEADS, HEAD_PAD - NOPE - ROPE), F32)], axis=-1)
    wuq = uq.reshape(Q_RANK, N_HEADS * HEAD_PAD).astype(BF16)
    uk = jnp.concatenate([w_uk[0], jnp.zeros((KV_RANK, N_HEADS, HEAD_PAD - NOPE), F32)], axis=-1)
    wuk = uk.reshape(KV_RANK, N_HEADS * HEAD_PAD).astype(BF16)
    wuv = w_uv[0].reshape(KV_RANK, MLA_WIDTH).T.astype(BF16)
    qg, kvg = row2(q_norm_g[0]), row2(kv_norm_g[0])

    rope_c, rope_sa, rope_sb = _rope_tables(l)
    flat_c = jnp.concatenate([jnp.ones((lc, NOPE + ROPE), F32), jnp.zeros((lc, LANES - NOPE - ROPE), F32)], axis=-1)
    flat_s = jnp.zeros((lc, LANES), F32)

    q, k_l, v_l, u_l = _inproj(x, mod3, None, w_in_r, qg, kvg, wuq, wuk, wuv, rope_c, rope_sa, rope_sb, True)
    k_c, v_c, u_c = _inproj(ctx, mod3, b, w_in_r, qg, kvg, wuq, wuk, wuv, flat_c, flat_s, flat_s, False)

    att = _attention(q, k_c, k_l, v_c, v_l)

    a_t, ab, ca, kk = _s5prep(s5_lambda_re[0], s5_lambda_im[0], s5_log_dt[0], s5_b_re[0], s5_b_im[0],
                              s5_c_re[0], s5_c_im[0])
    w_mat, v_mat, m_mat = _s5_matrices(ab, ca, kk)
    slabs = lambda t: t.reshape(2, S5_COLS // SLAB, 1, SLAB)
    a_cat = jnp.concatenate([slabs(a_t[0]), slabs(a_t[1])], axis=2).reshape(2, 2 * S5_COLS)
    a_b = jnp.broadcast_to(a_cat[:, None, :], (2, SUBLANES, 2 * S5_COLS))
    dskip = jnp.stack([s5_d[0], jnp.zeros_like(s5_d[0])])[:, None, :]
    y_dir = _s5scan(u_c, u_l, w_mat, a_b, v_mat, m_mat, dskip)

    return _tail(x, att, y_dir, mod3, s5_w_glu[0].astype(BF16), row2(s5_b_glu[0]), w_out[0].astype(BF16),
                 row2(ln1_g[0]), row2(ln1_b[0]), w_gate_up[0].astype(BF16), w_down[0].astype(BF16),
                 row2(ln2_g[0]), row2(ln2_b[0]))
```

```python
import functools
import math

import jax
import jax.numpy as jnp
import numpy as np
from jax import lax
from jax.experimental import pallas as pl
from jax.experimental.pallas import tpu as pltpu

D_MODEL = 1024
GRID_W = 64
N_HEADS = 8
NOPE = 64
ROPE = 32
V_DIM = 64
Q_RANK = 384
KV_RANK = 256
MLA_WIDTH = N_HEADS * V_DIM
S5_WIDTH = D_MODEL - MLA_WIDTH
S5_GROUP = 16
S5_GROUPS = S5_WIDTH // S5_GROUP
S5_STATE = 64
S5_COLS = S5_GROUPS * S5_STATE
KV_END = Q_RANK + KV_RANK
ROPE_END = KV_END + ROPE
D_FF = 2816
ROPE_THETA = 10000.0
NORM_EPS = 1e-6
DEPTH = 1
DN_ALPHA = (2.0 * DEPTH) ** 0.25

LANES = 128
SUBLANES = 8
HEAD_PAD = LANES
ROPE_HALF = ROPE // 2
P_COLS = Q_RANK + KV_RANK + S5_WIDTH + LANES
VMEM_LIMIT = 56 * 1024 * 1024

TM = 512
BQ = 512
DEN_ROWS = 16
HEADS_PER_STEP = 8
Q_SCALE = (NOPE + ROPE) ** -0.5 * math.log2(math.e)
TC = 128
S5_T = 4
N_ROWS = TC // S5_T * SUBLANES
N_LANE_TILES = S5_WIDTH // LANES
GROUPS_PER_TILE = LANES // S5_GROUP
TILE_COLS = GROUPS_PER_TILE * S5_STATE
TILES_PER_SLAB = 2
SLAB = TILES_PER_SLAB * TILE_COLS
FF_CHUNK = 256

F32 = jnp.float32
BF16 = jnp.bfloat16


def _params(*sem):
    return pltpu.CompilerParams(dimension_semantics=sem, vmem_limit_bytes=VMEM_LIMIT)


def _const_spec(shape):
    nd = len(shape)
    return pl.BlockSpec(shape, lambda *_: (0,) * nd, pipeline_mode=pl.Buffered(1))


def _layer_norm(x):
    mu = jnp.mean(x, axis=-1, keepdims=True)
    xc = x - mu
    var = jnp.mean(xc * xc, axis=-1, keepdims=True)
    return xc * lax.rsqrt(var + NORM_EPS)


def _rms_norm(x, g):
    return x * lax.rsqrt(jnp.mean(x * x, axis=-1, keepdims=True) + NORM_EPS) * g


def _dot(a, b):
    return jnp.dot(a, b, preferred_element_type=F32)


_NT = (((1,), (1,)), ((), ()))


def _adaln_kernel(cond_ref, w_ref, b_ref, o_ref):
    cnd = cond_ref[...]
    act = cnd * jax.nn.sigmoid(cnd)
    o_ref[...] = _dot(act.astype(BF16), w_ref[...].astype(BF16)) + b_ref[...]


def _adaln(cond, w_ada, b_ada):
    rows, n = cond.shape[0], w_ada.shape[1]
    tn = D_MODEL
    return pl.pallas_call(
        _adaln_kernel,
        grid=(n // tn,),
        in_specs=[pl.BlockSpec((rows, D_MODEL), lambda j: (0, 0)),
                  pl.BlockSpec((D_MODEL, tn), lambda j: (0, j)),
                  pl.BlockSpec((1, tn), lambda j: (0, j))],
        out_specs=pl.BlockSpec((rows, tn), lambda j: (0, j)),
        out_shape=jax.ShapeDtypeStruct((rows, n), F32),
        compiler_params=_params("arbitrary"),
        name="adaln",
    )(cond, w_ada, b_ada)


def _s5prep_kernel(lre_ref, lim_ref, ldt_ref, bre_ref, bim_ref, cre_ref, cim_ref, at_ref, ab_ref, ca_ref, kk_ref):
    lre, lim = lre_ref[...], lim_ref[...]
    dt = jnp.exp(ldt_ref[...])
    mag = jnp.exp(lre * dt)
    a_re, a_im = mag * jnp.cos(lim * dt), mag * jnp.sin(lim * dt)
    den = lre * lre + lim * lim
    f_re = ((a_re - 1) * lre + a_im * lim) / den
    f_im = (a_im * lre - (a_re - 1) * lim) / den
    b_re, b_im = bre_ref[...], bim_ref[...]
    c_re, c_im = cre_ref[...], cim_ref[...]

    def times_a(z_re, z_im):
        return a_re * z_re - a_im * z_im, a_re * z_im + a_im * z_re

    def over_states(x, y):
        return lax.dot_general(x, y, (((2,), (2,)), ((0,), (0,))), precision=lax.Precision.HIGHEST,
                               preferred_element_type=F32)

    p_re, p_im = f_re * b_re - f_im * b_im, f_re * b_im + f_im * b_re
    q_re, q_im = c_re, c_im
    w_re, w_im = jnp.ones_like(a_re), jnp.zeros_like(a_re)
    for k in range(S5_T):
        ab_ref[0, k] = p_re
        ab_ref[1, k] = p_im
        kk_ref[k] = over_states(c_re, p_re) - over_states(c_im, p_im)
        p_re, p_im = times_a(p_re, p_im)
        q_re, q_im = times_a(q_re, q_im)
        w_re, w_im = times_a(w_re, w_im)
        ca_ref[0, k] = q_re
        ca_ref[1, k] = q_im
    at_ref[0] = w_re
    at_ref[1] = w_im


def _s5prep(lam_re, lam_im, log_dt, b_re, b_im, c_re, c_im):
    n = 2 * S5_GROUPS
    full = (n, S5_GROUP, S5_STATE)
    per_state = lambda t: jnp.broadcast_to(t.reshape(n, 1, S5_STATE), full)
    ldt = jnp.broadcast_to(log_dt.reshape(n, 1, 1), full)
    chan_state = lambda t: t.reshape(full)
    shp = lambda *lead: jax.ShapeDtypeStruct(lead + full, F32)
    at, ab, ca, kk = pl.pallas_call(
        _s5prep_kernel,
        out_shape=(shp(2), shp(2, S5_T), shp(2, S5_T), jax.ShapeDtypeStruct((S5_T, n, S5_GROUP, S5_GROUP), F32)),
        compiler_params=pltpu.CompilerParams(vmem_limit_bytes=VMEM_LIMIT),
        name="s5prep",
    )(per_state(lam_re), per_state(lam_im), ldt, chan_state(jnp.swapaxes(b_re, -1, -2)),
      chan_state(jnp.swapaxes(b_im, -1, -2)), chan_state(c_re), chan_state(c_im))
    by_dir = lambda t, lead: t.reshape(lead + (2, S5_GROUPS) + t.shape[-2:])
    return by_dir(at, (2,))[:, :, :, 0], by_dir(ab, (2, S5_T)), by_dir(ca, (2, S5_T)), by_dir(kk, (S5_T,))


def _spread_groups(x, rows_per_group, col_width, n_col_blocks):
    g8 = GROUPS_PER_TILE
    spread = np.kron(np.eye(n_col_blocks), np.kron(np.ones((1, g8)), np.eye(col_width))).astype(np.float32)
    row_group = (np.arange(x.shape[1]) // rows_per_group) % g8
    col_group = (np.arange(spread.shape[1]) // col_width) % g8
    mask = row_group[:, None] == col_group[None, :]
    tiled = jnp.einsum("lrk,kc->lrc", x, spread, precision=lax.Precision.HIGHEST)
    return jnp.where(mask[None], tiled, 0.0).astype(BF16)


def _s5_matrices(ab, ca, kk):
    t_, g8, nt = S5_T, GROUPS_PER_TILE, N_LANE_TILES
    w_all, v_all, m_all = [], [], []
    for d in range(2):
        walked = (lambda s: s) if d == 0 else (lambda s: t_ - 1 - s)
        ab_d = jnp.stack([ab[:, t_ - 1 - walked(s), d] for s in range(t_)], axis=1)
        ab_d = ab_d.reshape(2, t_, nt, g8, S5_GROUP, S5_STATE).transpose(2, 1, 3, 4, 0, 5)
        w_all.append(ab_d.reshape(nt, t_ * LANES, 2 * S5_STATE))
        ca_d = jnp.stack([ca[:, walked(t), d] for t in range(t_)], axis=1)
        ca_d = ca_d * jnp.array([1.0, -1.0], F32).reshape(2, 1, 1, 1, 1)
        ca_d = ca_d.reshape(2, t_, nt, g8, S5_GROUP, S5_STATE).transpose(2, 0, 3, 5, 1, 4)
        v_all.append(ca_d.reshape(nt, 2 * TILE_COLS, t_ * S5_GROUP))
        zero = jnp.zeros_like(kk[0, d])
        blocks = jnp.stack([jnp.stack([kk[walked(t) - walked(s), d] if walked(t) >= walked(s) else zero
                                       for t in range(t_)], axis=0) for s in range(t_)], axis=0)
        blocks = blocks.reshape(t_, t_, nt, g8, S5_GROUP, S5_GROUP).transpose(2, 0, 3, 5, 1, 4)
        m_all.append(blocks.reshape(nt, t_ * LANES, t_ * S5_GROUP))

    def spread(parts, *args):
        out = _spread_groups(jnp.concatenate(parts, axis=0), *args)
        return out.reshape((2, nt) + out.shape[1:])

    return (spread(w_all, S5_GROUP, S5_STATE, 2), spread(v_all, S5_STATE, S5_GROUP, t_),
            spread(m_all, S5_GROUP, S5_GROUP, t_))


def _rope_tile(t, c, sa, sb):
    return t * c + pltpu.roll(t, LANES - ROPE_HALF, 1) * sa + pltpu.roll(t, ROPE_HALF, 1) * sb


def _inproj_kernel(with_q, x_ref, sh_ref, sc_ref, win_ref, qg_ref, kvg_ref, wuq_ref, wuk_ref, wuv_ref,
                   c_ref, sa_ref, sb_ref, *refs):
    if with_q:
        q_ref, k_ref, v_ref, u_ref = refs
    else:
        k_ref, v_ref, u_ref = refs[2:]
    x = x_ref[0]
    xm = _layer_norm(x) * (1.0 + sc_ref[0]) + sh_ref[0]
    p = _dot(xm.astype(BF16), win_ref[...])
    c, sa, sb = c_ref[...], sa_ref[...], sb_ref[...]

    kv_c = _rms_norm(p[:, Q_RANK:KV_END], kvg_ref[...]).astype(BF16)
    k = _dot(kv_c, wuk_ref[...])
    kr = _rope_tile(p[:, KV_END + S5_WIDTH:], c, sa, sb)
    for h in range(N_HEADS):
        k_ref[0, :, h * HEAD_PAD:(h + 1) * HEAD_PAD] = (k[:, h * HEAD_PAD:(h + 1) * HEAD_PAD] + kr).astype(BF16)
    v_ref[0] = lax.dot_general(wuv_ref[...], kv_c, _NT, preferred_element_type=F32).astype(BF16)
    u_ref[0] = p[:, KV_END:KV_END + S5_WIDTH]

    if with_q:
        q_c = _rms_norm(p[:, :Q_RANK], qg_ref[...]).astype(BF16)
        q = _dot(q_c, wuq_ref[...])
        cq, saq, sbq = c * Q_SCALE, sa * Q_SCALE, sb * Q_SCALE
        for h in range(N_HEADS):
            sl = slice(h * HEAD_PAD, (h + 1) * HEAD_PAD)
            q_ref[0, :, sl] = _rope_tile(q[:, sl], cq, saq, sbq).astype(BF16)


def _inproj(x, mod3, mod_row, w_in_r, qg, kvg, wuq, wuk, wuv, rope_c, rope_sa, rope_sb, n_keys, kv_into=None):
    b, l, _ = x.shape
    with_q = kv_into is None
    tm = min(TM, l)
    key_blk = 0 if with_q else (n_keys - l) // l
    assert with_q or (tm == l and (n_keys - l) % l == 0)
    row = (lambda bi: bi) if mod_row is None else (lambda bi: mod_row)
    tok = lambda w: pl.BlockSpec((1, tm, w), lambda bi, i: (bi, i, 0))
    in_specs = [
        tok(D_MODEL),
        pl.BlockSpec((1, 1, D_MODEL), lambda bi, i: (row(bi), 0, 0)),
        pl.BlockSpec((1, 1, D_MODEL), lambda bi, i: (row(bi), 0, 1)),
        _const_spec(w_in_r.shape), _const_spec(qg.shape), _const_spec(kvg.shape),
        _const_spec(wuq.shape), _const_spec(wuk.shape), _const_spec(wuv.shape),
        pl.BlockSpec((tm, LANES), lambda bi, i: (i, 0)),
        pl.BlockSpec((tm, LANES), lambda bi, i: (i, 0)),
        pl.BlockSpec((tm, LANES), lambda bi, i: (i, 0)),
    ]
    out_specs = [pl.BlockSpec((1, tm, N_HEADS * HEAD_PAD), lambda bi, i: (bi, key_blk + i, 0)),
                 pl.BlockSpec((1, MLA_WIDTH, tm), lambda bi, i: (bi, 0, key_blk + i)), tok(S5_WIDTH)]
    out_shape = [jax.ShapeDtypeStruct((b, n_keys, N_HEADS * HEAD_PAD), BF16),
                 jax.ShapeDtypeStruct((b, MLA_WIDTH, n_keys), BF16),
                 jax.ShapeDtypeStruct((b, l, S5_WIDTH), F32)]
    args = (x, mod3, mod3, w_in_r, qg, kvg, wuq, wuk, wuv, rope_c, rope_sa, rope_sb)
    aliases = {}
    if with_q:
        out_specs = [tok(N_HEADS * HEAD_PAD)] + out_specs
        out_shape = [jax.ShapeDtypeStruct((b, l, N_HEADS * HEAD_PAD), BF16)] + out_shape
    else:
        in_specs += [pl.BlockSpec(memory_space=pl.ANY), pl.BlockSpec(memory_space=pl.ANY)]
        aliases = {len(args): 0, len(args) + 1: 1}
        args += tuple(kv_into)
    return pl.pallas_call(
        functools.partial(_inproj_kernel, with_q),
        grid=(b, l // tm),
        in_specs=in_specs, out_specs=out_specs, out_shape=out_shape,
        input_output_aliases=aliases,
        compiler_params=_params("parallel", "parallel"),
        name="inproj_lat" if with_q else "inproj_ctx",
    )(*args)


def _attn_kernel(q_ref, k_ref, v_ref, o_ref):
    def scores(h):
        ks = slice(h * HEAD_PAD, (h + 1) * HEAD_PAD)
        return lax.dot_general(k_ref[0, :, ks], q_ref[0, :, ks], _NT, preferred_element_type=F32)

    ones = jnp.ones((DEN_ROWS, k_ref.shape[1]), BF16)
    outs = []
    s_next = scores(0)
    for h in range(HEADS_PER_STEP):
        s = s_next
        if h + 1 < HEADS_PER_STEP:
            s_next = scores(h + 1)
        e = jnp.exp2(s - jnp.max(s, axis=0, keepdims=True)).astype(BF16)
        vt = jnp.concatenate([v_ref[0, h * V_DIM:(h + 1) * V_DIM, :], ones], axis=0)
        o_t = _dot(vt, e)
        outs.append(o_t[:V_DIM] / o_t[V_DIM:V_DIM + 1])
    o_ref[0] = jnp.concatenate(outs, axis=0).T.astype(o_ref.dtype)


def _attention(q, k, v_t):
    b, l, _ = q.shape
    n_keys = k.shape[1]
    kw, vw = HEADS_PER_STEP * HEAD_PAD, HEADS_PER_STEP * V_DIM
    return pl.pallas_call(
        _attn_kernel,
        grid=(b, N_HEADS // HEADS_PER_STEP, l // BQ),
        in_specs=[pl.BlockSpec((1, BQ, kw), lambda bi, hp, i: (bi, i, hp)),
                  pl.BlockSpec((1, n_keys, kw), lambda bi, hp, i: (bi, 0, hp)),
                  pl.BlockSpec((1, vw, n_keys), lambda bi, hp, i: (bi, hp, 0))],
        out_specs=pl.BlockSpec((1, BQ, vw), lambda bi, hp, i: (bi, i, hp)),
        out_shape=jax.ShapeDtypeStruct((b, l, MLA_WIDTH), BF16),
        compiler_params=_params("parallel", "parallel", "arbitrary"),
        name="attn",
    )(q, k, v_t)


def _s5scan_kernel(n_ctx_chunks, uc_ref, ul_ref, w_ref, a_ref, v_ref, m_ref, dskip_ref, y_ref, hs_ref, st_ref):
    d = pl.program_id(0)
    i = pl.program_id(1)

    @pl.when(i == 0)
    def _():
        st_ref[...] = jnp.zeros_like(st_ref)

    u_bt = jnp.where(i < n_ctx_chunks, uc_ref[...], ul_ref[...])
    u = jnp.swapaxes(u_bt, 0, 1).reshape(N_ROWS // SUBLANES, S5_T, SUBLANES, S5_WIDTH)
    u_pos = [u[:, s].reshape(N_ROWS, S5_WIDTH) for s in range(S5_T)]

    def group_lanes(lt):
        lanes = slice(lt * LANES, (lt + 1) * LANES)
        return jnp.concatenate([p[:, lanes] for p in u_pos], axis=1).astype(BF16)

    def tile_cols(lt):
        base = (lt // TILES_PER_SLAB) * 2 * SLAB + (lt % TILES_PER_SLAB) * TILE_COLS
        return slice(base, base + TILE_COLS), slice(base + SLAB, base + SLAB + TILE_COLS)

    for lt in range(N_LANE_TILES):
        re, im = tile_cols(lt)
        inc = _dot(group_lanes(lt), w_ref[0, lt])
        hs_ref[:, re] = inc[:, :TILE_COLS]
        hs_ref[:, im] = inc[:, TILE_COLS:]

    n_steps = N_ROWS // SUBLANES
    for cs in range(0, 2 * S5_COLS, 2 * SLAB):
        re = slice(cs, cs + SLAB)
        im = slice(cs + SLAB, cs + 2 * SLAB)
        a_re, a_im = a_ref[0, :, re], a_ref[0, :, im]

        def step(tt, carry, re=re, im=im, a_re=a_re, a_im=a_im):
            h_re, h_im = carry
            t = jnp.where(d == 0, tt, n_steps - 1 - tt)
            rows = pl.ds(pl.multiple_of(t * SUBLANES, SUBLANES), SUBLANES)
            n_re = a_re * h_re - a_im * h_im + hs_ref[rows, re]
            n_im = a_re * h_im + a_im * h_re + hs_ref[rows, im]
            hs_ref[rows, re] = h_re
            hs_ref[rows, im] = h_im
            return n_re, n_im

        h_re, h_im = lax.fori_loop(0, n_steps, step, (st_ref[:, re], st_ref[:, im]), unroll=4)
        st_ref[:, re] = h_re
        st_ref[:, im] = h_im

    for lt in range(N_LANE_TILES):
        re, im = tile_cols(lt)
        lanes = slice(lt * LANES, (lt + 1) * LANES)
        y = (_dot(hs_ref[:, re].astype(BF16), v_ref[0, lt, :TILE_COLS])
             + _dot(hs_ref[:, im].astype(BF16), v_ref[0, lt, TILE_COLS:])
             + _dot(group_lanes(lt), m_ref[0, lt]))
        skip = dskip_ref[0, :, lanes]
        y_pos = [(y[:, s * LANES:(s + 1) * LANES] + u_pos[s][:, lanes] * skip)
                 .reshape(N_ROWS // SUBLANES, 1, SUBLANES, LANES) for s in range(S5_T)]
        y_tb = jnp.concatenate(y_pos, axis=1).reshape(TC, SUBLANES, LANES)
        y_ref[0, :, :, lanes] = jnp.swapaxes(y_tb, 0, 1)


def _s5scan(u_c, u_l, w_mat, a_b, v_mat, m_mat, dskip):
    b, l, _ = u_l.shape
    nc, nl = u_c.shape[1] // TC, l // TC

    def ctx_blk(d, i):
        return jnp.clip(jnp.where(d == 0, i, nc - 1 - i), 0, nc - 1)

    def lat_blk(d, i):
        return jnp.clip(jnp.where(d == 0, i - nc, nl - 1 - (i - nc)), 0, nl - 1)

    def out_blk(d, i):
        return jnp.where(i < nc, nl + ctx_blk(d, i), lat_blk(d, i))

    return pl.pallas_call(
        functools.partial(_s5scan_kernel, nc),
        grid=(2, nc + nl),
        in_specs=[pl.BlockSpec((b, TC, S5_WIDTH), lambda d, i: (0, ctx_blk(d, i), 0)),
                  pl.BlockSpec((b, TC, S5_WIDTH), lambda d, i: (0, lat_blk(d, i), 0)),
                  pl.BlockSpec((1, N_LANE_TILES, S5_T * LANES, 2 * TILE_COLS), lambda d, i: (d, 0, 0, 0)),
                  pl.BlockSpec((1, SUBLANES, 2 * S5_COLS), lambda d, i: (d, 0, 0)),
                  pl.BlockSpec((1, N_LANE_TILES, 2 * TILE_COLS, S5_T * LANES), lambda d, i: (d, 0, 0, 0)),
                  pl.BlockSpec((1, N_LANE_TILES, S5_T * LANES, S5_T * LANES), lambda d, i: (d, 0, 0, 0)),
                  pl.BlockSpec((1, 1, S5_WIDTH), lambda d, i: (d, 0, 0))],
        out_specs=pl.BlockSpec((1, b, TC, S5_WIDTH), lambda d, i: (d, 0, out_blk(d, i), 0)),
        out_shape=jax.ShapeDtypeStruct((2, b, (nc + nl) * TC, S5_WIDTH), F32),
        scratch_shapes=[pltpu.VMEM((N_ROWS, 2 * S5_COLS), F32),
                        pltpu.VMEM((SUBLANES, 2 * S5_COLS), F32)],
        compiler_params=_params("arbitrary", "arbitrary"),
        name="s5scan",
    )(u_c, u_l, w_mat, a_b, v_mat, m_mat, dskip)


def _tail_kernel(x_ref, att_ref, yf_ref, yb_ref, g1_ref, sh2_ref, sc2_ref, g2_ref, wglu_ref, bglu_ref, wout_ref,
                 ln1g_ref, ln1b_ref, wgu_ref, wd_ref, ln2g_ref, ln2b_ref, o_ref, acc_ref):
    z = jax.nn.gelu(yf_ref[0, 0] + yb_ref[0, 0])
    s5o = z * jax.nn.sigmoid(_dot(z.astype(BF16), wglu_ref[...]) + bglu_ref[...])
    mix = _dot(att_ref[0], wout_ref[:MLA_WIDTH, :]) + _dot(s5o.astype(BF16), wout_ref[MLA_WIDTH:, :])
    x1 = _layer_norm(DN_ALPHA * x_ref[0] + g1_ref[0] * mix) * ln1g_ref[...] + ln1b_ref[...]

    xm = (_layer_norm(x1) * (1.0 + sc2_ref[0]) + sh2_ref[0]).astype(BF16)
    for j in range(D_FF // FF_CHUNK):
        gate = _dot(xm, wgu_ref[:, j * FF_CHUNK:(j + 1) * FF_CHUNK])
        up = _dot(xm, wgu_ref[:, D_FF + j * FF_CHUNK:D_FF + (j + 1) * FF_CHUNK])
        hidden = (gate * jax.nn.sigmoid(gate) * up).astype(BF16)
        part = _dot(hidden, wd_ref[j * FF_CHUNK:(j + 1) * FF_CHUNK, :])
        if j == 0:
            acc_ref[...] = part
        else:
            acc_ref[...] += part
    o_ref[0] = _layer_norm(DN_ALPHA * x1 + g2_ref[0] * acc_ref[...]) * ln2g_ref[...] + ln2b_ref[...]


def _tail(x, att, y_dir, mod3, w_glu, b_glu, w_out, ln1_g, ln1_b, w_gu, w_down, ln2_g, ln2_b):
    b, l, _ = x.shape
    tok = lambda w: pl.BlockSpec((1, TM, w), lambda bi, i: (bi, i, 0))
    mod = lambda j: pl.BlockSpec((1, 1, D_MODEL), lambda bi, i: (bi, 0, j))
    y_of = lambda d: pl.BlockSpec((1, 1, TM, S5_WIDTH), lambda bi, i: (d, bi, i, 0))
    consts = (w_glu, b_glu, w_out, ln1_g, ln1_b, w_gu, w_down, ln2_g, ln2_b)
    return pl.pallas_call(
        _tail_kernel,
        grid=(b, l // TM),
        in_specs=[tok(D_MODEL), tok(MLA_WIDTH), y_of(0), y_of(1), mod(2), mod(3), mod(4), mod(5)]
                 + [_const_spec(w.shape) for w in consts],
        out_specs=tok(D_MODEL),
        out_shape=jax.ShapeDtypeStruct((b, l, D_MODEL), F32),
        scratch_shapes=[pltpu.VMEM((TM, D_MODEL), F32)],
        compiler_params=_params("parallel", "parallel"),
        name="tail",
    )(x, att, y_dir, y_dir, mod3, mod3, mod3, mod3, *consts)


def _rope_tables(seq):
    pos = jnp.arange(seq)
    row = (pos // GRID_W).astype(F32)
    col = (pos % GRID_W).astype(F32)
    n_freq = ROPE // 4
    freqs = ROPE_THETA ** (-jnp.arange(n_freq, dtype=F32) / n_freq)
    ang = jnp.concatenate([row[:, None] * freqs, col[:, None] * freqs], axis=-1)
    cos, sin = jnp.cos(ang), jnp.sin(ang)
    zeros = lambda w: jnp.zeros((seq, w), F32)
    tail = LANES - NOPE - ROPE
    c = jnp.concatenate([jnp.ones((seq, NOPE), F32), cos, cos, zeros(tail)], axis=-1)
    sa = jnp.concatenate([zeros(NOPE), -sin, zeros(ROPE_HALF), zeros(tail)], axis=-1)
    sb = jnp.concatenate([zeros(NOPE), zeros(ROPE_HALF), sin, zeros(tail)], axis=-1)
    return c, sa, sb


def _pair_split(w):
    return jnp.concatenate([w[..., 0::2], w[..., 1::2]], axis=-1)


def kernel(x, c, ctx, c_ctx, w_ada, b_ada, w_in, q_norm_g, kv_norm_g, w_uq, w_uk, w_uv, s5_lambda_re, s5_lambda_im, s5_log_dt, s5_b_re, s5_b_im, s5_c_re, s5_c_im, s5_d, s5_w_glu, s5_b_glu, w_out, ln1_g, ln1_b, w_gate_up, w_down, ln2_g, ln2_b):
    assert w_ada.shape[0] == DEPTH == 1
    b, l, _ = x.shape
    assert b == SUBLANES, "the S5 scan maps the batch onto the sublanes of one vreg row"
    lc = ctx.shape[1]
    row2 = lambda t: t.reshape(1, -1)

    cond = jnp.concatenate([c, c_ctx[None, :], jnp.zeros((2 * SUBLANES - b - 1, D_MODEL), F32)], axis=0)
    mod = _adaln(cond, w_ada[0], row2(b_ada[0]))
    mod3 = mod.reshape(mod.shape[0], 1, 6 * D_MODEL)

    wi = w_in[0]
    rope_tile = jnp.zeros((D_MODEL, LANES), F32).at[:, NOPE:NOPE + ROPE].set(_pair_split(wi[:, KV_END:ROPE_END]))
    w_in_r = jnp.concatenate([wi[:, :KV_END], wi[:, ROPE_END:], rope_tile], axis=1).astype(BF16)
    uq = w_uq[0]
    uq = jnp.concatenate([uq[..., :NOPE], _pair_split(uq[..., NOPE:]),
                          jnp.zeros((Q_RANK, N_HEADS, HEAD_PAD - NOPE - ROPE), F32)], axis=-1)
    wuq = uq.reshape(Q_RANK, N_HEADS * HEAD_PAD).astype(BF16)
    uk = jnp.concatenate([w_uk[0], jnp.zeros((KV_RANK, N_HEADS, HEAD_PAD - NOPE), F32)], axis=-1)
    wuk = uk.reshape(KV_RANK, N_HEADS * HEAD_PAD).astype(BF16)
    wuv = w_uv[0].reshape(KV_RANK, MLA_WIDTH).T.astype(BF16)
    qg, kvg = row2(q_norm_g[0]), row2(kv_norm_g[0])

    rope_c, rope_sa, rope_sb = _rope_tables(l)
    flat_c = jnp.concatenate([jnp.ones((lc, NOPE + ROPE), F32), jnp.zeros((lc, LANES - NOPE - ROPE), F32)], axis=-1)
    flat_s = jnp.zeros((lc, LANES), F32)

    q, k, v_t, u_l = _inproj(x, mod3, None, w_in_r, qg, kvg, wuq, wuk, wuv, rope_c, rope_sa, rope_sb, l + lc)
    k, v_t, u_c = _inproj(ctx, mod3, b, w_in_r, qg, kvg, wuq, wuk, wuv, flat_c, flat_s, flat_s, l + lc,
                          kv_into=(k, v_t))

    att = _attention(q, k, v_t)

    a_t, ab, ca, kk = _s5prep(s5_lambda_re[0], s5_lambda_im[0], s5_log_dt[0], s5_b_re[0], s5_b_im[0],
                              s5_c_re[0], s5_c_im[0])
    w_mat, v_mat, m_mat = _s5_matrices(ab, ca, kk)
    slabs = lambda t: t.reshape(2, S5_COLS // SLAB, 1, SLAB)
    a_cat = jnp.concatenate([slabs(a_t[0]), slabs(a_t[1])], axis=2).reshape(2, 2 * S5_COLS)
    a_b = jnp.broadcast_to(a_cat[:, None, :], (2, SUBLANES, 2 * S5_COLS))
    dskip = jnp.stack([s5_d[0], jnp.zeros_like(s5_d[0])])[:, None, :]
    y_dir = _s5scan(u_c, u_l, w_mat, a_b, v_mat, m_mat, dskip)

    return _tail(x, att, y_dir, mod3, s5_w_glu[0].astype(BF16), row2(s5_b_glu[0]), w_out[0].astype(BF16),
                 row2(ln1_g[0]), row2(ln1_b[0]), w_gate_up[0].astype(BF16), w_down[0].astype(BF16),
                 row2(ln2_g[0]), row2(ln2_b[0]))
```

```python
import functools
import math

import jax
import jax.numpy as jnp
import numpy as np
from jax import lax
from jax.experimental import pallas as pl
from jax.experimental.pallas import tpu as pltpu

D_MODEL = 1024
GRID_W = 64
N_HEADS = 8
NOPE = 64
ROPE = 32
V_DIM = 64
Q_RANK = 384
KV_RANK = 256
MLA_WIDTH = N_HEADS * V_DIM
S5_WIDTH = D_MODEL - MLA_WIDTH
S5_GROUP = 16
S5_GROUPS = S5_WIDTH // S5_GROUP
S5_STATE = 64
S5_COLS = S5_GROUPS * S5_STATE
KV_END = Q_RANK + KV_RANK
ROPE_END = KV_END + ROPE
D_FF = 2816
ROPE_THETA = 10000.0
NORM_EPS = 1e-6
DEPTH = 1
DN_ALPHA = (2.0 * DEPTH) ** 0.25

LANES = 128
SUBLANES = 8
HEAD_PAD = LANES
ROPE_HALF = ROPE // 2
NOPE_LO = LANES // 2 - ROPE_HALF
P_COLS = Q_RANK + KV_RANK + S5_WIDTH + LANES
VMEM_LIMIT = 56 * 1024 * 1024

TM = 512
BQ = 512
DEN_ROWS = 16
HEADS_PER_STEP = 8
Q_SCALE = (NOPE + ROPE) ** -0.5 * math.log2(math.e)
TC = 128
S5_T = 4
N_ROWS = TC // S5_T * SUBLANES
N_LANE_TILES = S5_WIDTH // LANES
GROUPS_PER_TILE = LANES // S5_GROUP
TILE_COLS = GROUPS_PER_TILE * S5_STATE
TILES_PER_SLAB = 2
SLAB = TILES_PER_SLAB * TILE_COLS
FF_CHUNK = 256

F32 = jnp.float32
BF16 = jnp.bfloat16


def _params(*sem):
    return pltpu.CompilerParams(dimension_semantics=sem, vmem_limit_bytes=VMEM_LIMIT)


def _const_spec(shape):
    nd = len(shape)
    return pl.BlockSpec(shape, lambda *_: (0,) * nd, pipeline_mode=pl.Buffered(1))


def _layer_norm(x):
    mu = jnp.mean(x, axis=-1, keepdims=True)
    xc = x - mu
    var = jnp.mean(xc * xc, axis=-1, keepdims=True)
    return xc * lax.rsqrt(var + NORM_EPS)


def _rms_norm(x, g):
    return x * lax.rsqrt(jnp.mean(x * x, axis=-1, keepdims=True) + NORM_EPS) * g


def _dot(a, b):
    return jnp.dot(a, b, preferred_element_type=F32)


_NT = (((1,), (1,)), ((), ()))


def _adaln_kernel(cond_ref, w_ref, b_ref, o_ref):
    cnd = cond_ref[...]
    act = cnd * jax.nn.sigmoid(cnd)
    o_ref[...] = _dot(act.astype(BF16), w_ref[...].astype(BF16)) + b_ref[...]


def _adaln(cond, w_ada, b_ada):
    rows, n = cond.shape[0], w_ada.shape[1]
    tn = D_MODEL
    return pl.pallas_call(
        _adaln_kernel,
        grid=(n // tn,),
        in_specs=[pl.BlockSpec((rows, D_MODEL), lambda j: (0, 0)),
                  pl.BlockSpec((D_MODEL, tn), lambda j: (0, j)),
                  pl.BlockSpec((1, tn), lambda j: (0, j))],
        out_specs=pl.BlockSpec((rows, tn), lambda j: (0, j)),
        out_shape=jax.ShapeDtypeStruct((rows, n), F32),
        compiler_params=_params("arbitrary"),
        name="adaln",
    )(cond, w_ada, b_ada)


def _s5prep_kernel(lre_ref, lim_ref, ldt_ref, bre_ref, bim_ref, cre_ref, cim_ref, at_ref, ab_ref, ca_ref, kk_ref):
    lre, lim = lre_ref[...], lim_ref[...]
    dt = jnp.exp(ldt_ref[...])
    mag = jnp.exp(lre * dt)
    a_re, a_im = mag * jnp.cos(lim * dt), mag * jnp.sin(lim * dt)
    den = lre * lre + lim * lim
    f_re = ((a_re - 1) * lre + a_im * lim) / den
    f_im = (a_im * lre - (a_re - 1) * lim) / den
    b_re, b_im = bre_ref[...], bim_ref[...]
    c_re, c_im = cre_ref[...], cim_ref[...]

    def times_a(z_re, z_im):
        return a_re * z_re - a_im * z_im, a_re * z_im + a_im * z_re

    def over_states(x, y):
        return lax.dot_general(x, y, (((2,), (2,)), ((0,), (0,))), precision=lax.Precision.HIGHEST,
                               preferred_element_type=F32)

    p_re, p_im = f_re * b_re - f_im * b_im, f_re * b_im + f_im * b_re
    q_re, q_im = c_re, c_im
    w_re, w_im = jnp.ones_like(a_re), jnp.zeros_like(a_re)
    for k in range(S5_T):
        ab_ref[0, k] = p_re
        ab_ref[1, k] = p_im
        kk_ref[k] = over_states(c_re, p_re) - over_states(c_im, p_im)
        p_re, p_im = times_a(p_re, p_im)
        q_re, q_im = times_a(q_re, q_im)
        w_re, w_im = times_a(w_re, w_im)
        ca_ref[0, k] = q_re
        ca_ref[1, k] = q_im
    at_ref[0] = w_re
    at_ref[1] = w_im


def _s5prep(lam_re, lam_im, log_dt, b_re, b_im, c_re, c_im):
    n = 2 * S5_GROUPS
    full = (n, S5_GROUP, S5_STATE)
    per_state = lambda t: jnp.broadcast_to(t.reshape(n, 1, S5_STATE), full)
    ldt = jnp.broadcast_to(log_dt.reshape(n, 1, 1), full)
    chan_state = lambda t: t.reshape(full)
    shp = lambda *lead: jax.ShapeDtypeStruct(lead + full, F32)
    at, ab, ca, kk = pl.pallas_call(
        _s5prep_kernel,
        out_shape=(shp(2), shp(2, S5_T), shp(2, S5_T), jax.ShapeDtypeStruct((S5_T, n, S5_GROUP, S5_GROUP), F32)),
        compiler_params=pltpu.CompilerParams(vmem_limit_bytes=VMEM_LIMIT),
        name="s5prep",
    )(per_state(lam_re), per_state(lam_im), ldt, chan_state(jnp.swapaxes(b_re, -1, -2)),
      chan_state(jnp.swapaxes(b_im, -1, -2)), chan_state(c_re), chan_state(c_im))
    by_dir = lambda t, lead: t.reshape(lead + (2, S5_GROUPS) + t.shape[-2:])
    return by_dir(at, (2,))[:, :, :, 0], by_dir(ab, (2, S5_T)), by_dir(ca, (2, S5_T)), by_dir(kk, (S5_T,))


def _spread_groups(x, rows_per_group, col_width, n_col_blocks):
    g8 = GROUPS_PER_TILE
    n, rows, k = x.shape
    spread = np.kron(np.eye(n_col_blocks), np.kron(np.ones((1, g8)), np.eye(col_width))).astype(np.float32)
    cols = spread.shape[1]

    def body(x_ref, spread_ref, o_ref):
        tiled = _dot(x_ref[0].astype(BF16), spread_ref[...])
        row_group = (lax.broadcasted_iota(jnp.int32, (rows, cols), 0) // rows_per_group) % g8
        col_group = (lax.broadcasted_iota(jnp.int32, (rows, cols), 1) // col_width) % g8
        o_ref[0] = jnp.where(row_group == col_group, tiled, 0.0).astype(BF16)

    return pl.pallas_call(
        body,
        grid=(n,),
        in_specs=[pl.BlockSpec((1, rows, k), lambda i: (i, 0, 0)), pl.BlockSpec((k, cols), lambda i: (0, 0))],
        out_specs=pl.BlockSpec((1, rows, cols), lambda i: (i, 0, 0)),
        out_shape=jax.ShapeDtypeStruct((n, rows, cols), BF16),
        compiler_params=_params("parallel"),
        name="s5place",
    )(x, jnp.asarray(spread, dtype=BF16))


def _s5_matrices(ab, ca, kk):
    t_, g8, nt = S5_T, GROUPS_PER_TILE, N_LANE_TILES
    w_all, v_all, m_all = [], [], []
    for d in range(2):
        walked = (lambda s: s) if d == 0 else (lambda s: t_ - 1 - s)
        ab_d = jnp.stack([ab[:, t_ - 1 - walked(s), d] for s in range(t_)], axis=1)
        ab_d = ab_d.reshape(2, t_, nt, g8, S5_GROUP, S5_STATE).transpose(2, 1, 3, 4, 0, 5)
        w_all.append(ab_d.reshape(nt, t_ * LANES, 2 * S5_STATE))
        ca_d = jnp.stack([ca[:, walked(t), d] for t in range(t_)], axis=1)
        ca_d = ca_d * jnp.array([1.0, -1.0], F32).reshape(2, 1, 1, 1, 1)
        ca_d = ca_d.reshape(2, t_, nt, g8, S5_GROUP, S5_STATE).transpose(2, 0, 3, 5, 1, 4)
        v_all.append(ca_d.reshape(nt, 2 * TILE_COLS, t_ * S5_GROUP))
        zero = jnp.zeros_like(kk[0, d])
        blocks = jnp.stack([jnp.stack([kk[walked(t) - walked(s), d] if walked(t) >= walked(s) else zero
                                       for t in range(t_)], axis=0) for s in range(t_)], axis=0)
        blocks = blocks.reshape(t_, t_, nt, g8, S5_GROUP, S5_GROUP).transpose(2, 0, 3, 5, 1, 4)
        m_all.append(blocks.reshape(nt, t_ * LANES, t_ * S5_GROUP))

    def spread(parts, *args):
        out = _spread_groups(jnp.concatenate(parts, axis=0), *args)
        return out.reshape((2, nt) + out.shape[1:])

    return (spread(w_all, S5_GROUP, S5_STATE, 2), spread(v_all, S5_STATE, S5_GROUP, t_),
            spread(m_all, S5_GROUP, S5_GROUP, t_))


def _rope_tile(t, c, s):
    return t * c + pltpu.roll(t, LANES // 2, 1) * s


def _inproj_kernel(with_q, x_ref, sh_ref, sc_ref, win_ref, qg_ref, kvg_ref, wuq_ref, wuk_ref, wuv_ref,
                   c_ref, s_ref, *out_refs):
    if with_q:
        q_ref, k_ref, v_ref, u_ref = out_refs
    else:
        k_ref, v_ref, u_ref = out_refs
    x = x_ref[0]
    xm = _layer_norm(x) * (1.0 + sc_ref[0]) + sh_ref[0]
    p = _dot(xm.astype(BF16), win_ref[...])
    c, s = c_ref[...], s_ref[...]

    kv_c = _rms_norm(p[:, Q_RANK:KV_END], kvg_ref[...]).astype(BF16)
    k = _dot(kv_c, wuk_ref[...])
    kr = _rope_tile(p[:, KV_END + S5_WIDTH:], c, s)
    for h in range(N_HEADS):
        k_ref[0, :, h * HEAD_PAD:(h + 1) * HEAD_PAD] = (k[:, h * HEAD_PAD:(h + 1) * HEAD_PAD] + kr).astype(BF16)
    v_ref[0] = lax.dot_general(wuv_ref[...], kv_c, _NT, preferred_element_type=F32).astype(BF16)
    u_ref[0] = p[:, KV_END:KV_END + S5_WIDTH]

    if with_q:
        q_c = _rms_norm(p[:, :Q_RANK], qg_ref[...]).astype(BF16)
        q = _dot(q_c, wuq_ref[...])
        cq, sq = c * Q_SCALE, s * Q_SCALE
        for h in range(N_HEADS):
            sl = slice(h * HEAD_PAD, (h + 1) * HEAD_PAD)
            q_ref[0, :, sl] = _rope_tile(q[:, sl], cq, sq).astype(BF16)


def _inproj(x, mod3, mod_row, w_in_r, qg, kvg, wuq, wuk, wuv, rope_c, rope_s, with_q):
    b, l, _ = x.shape
    tm = min(TM, l)
    row = (lambda bi: bi) if mod_row is None else (lambda bi: mod_row)
    tok = lambda w: pl.BlockSpec((1, tm, w), lambda bi, i: (bi, i, 0))
    in_specs = [
        tok(D_MODEL),
        pl.BlockSpec((1, 1, D_MODEL), lambda bi, i: (row(bi), 0, 0)),
        pl.BlockSpec((1, 1, D_MODEL), lambda bi, i: (row(bi), 0, 1)),
        _const_spec(w_in_r.shape), _const_spec(qg.shape), _const_spec(kvg.shape),
        _const_spec(wuq.shape), _const_spec(wuk.shape), _const_spec(wuv.shape),
        pl.BlockSpec((tm, LANES), lambda bi, i: (i, 0)),
        pl.BlockSpec((tm, LANES), lambda bi, i: (i, 0)),
    ]
    out_specs = [tok(N_HEADS * HEAD_PAD), pl.BlockSpec((1, MLA_WIDTH, tm), lambda bi, i: (bi, 0, i)), tok(S5_WIDTH)]
    out_shape = [jax.ShapeDtypeStruct((b, l, N_HEADS * HEAD_PAD), BF16),
                 jax.ShapeDtypeStruct((b, MLA_WIDTH, l), BF16),
                 jax.ShapeDtypeStruct((b, l, S5_WIDTH), F32)]
    if with_q:
        out_specs = [tok(N_HEADS * HEAD_PAD)] + out_specs
        out_shape = [jax.ShapeDtypeStruct((b, l, N_HEADS * HEAD_PAD), BF16)] + out_shape
    return pl.pallas_call(
        functools.partial(_inproj_kernel, with_q),
        grid=(b, l // tm),
        in_specs=in_specs, out_specs=out_specs, out_shape=out_shape,
        compiler_params=_params("parallel", "parallel"),
        name="inproj_lat" if with_q else "inproj_ctx",
    )(x, mod3, mod3, w_in_r, qg, kvg, wuq, wuk, wuv, rope_c, rope_s)


def _attn_kernel(q_ref, kc_ref, kl_ref, vc_ref, vl_ref, o_ref):
    def scores(h):
        ks = slice(h * HEAD_PAD, (h + 1) * HEAD_PAD)
        q = q_ref[0, :, ks]
        return (lax.dot_general(kc_ref[0, :, ks], q, _NT, preferred_element_type=F32),
                lax.dot_general(kl_ref[0, :, ks], q, _NT, preferred_element_type=F32))

    def with_ones(vt):
        return jnp.concatenate([vt, jnp.ones((DEN_ROWS, vt.shape[1]), BF16)], axis=0)

    outs = []
    s_next = scores(0)
    for h in range(HEADS_PER_STEP):
        s_c, s_l = s_next
        if h + 1 < HEADS_PER_STEP:
            s_next = scores(h + 1)
        vs = slice(h * V_DIM, (h + 1) * V_DIM)
        m = jnp.maximum(jnp.max(s_c, axis=0, keepdims=True), jnp.max(s_l, axis=0, keepdims=True))
        e_c = jnp.exp2(s_c - m).astype(BF16)
        e_l = jnp.exp2(s_l - m).astype(BF16)
        o_t = _dot(with_ones(vc_ref[0, vs, :]), e_c) + _dot(with_ones(vl_ref[0, vs, :]), e_l)
        outs.append(o_t[:V_DIM] / o_t[V_DIM:V_DIM + 1])
    o_ref[0] = jnp.concatenate(outs, axis=0).T.astype(o_ref.dtype)


def _attention(q, k_c, k_l, v_c, v_l):
    b, l, _ = q.shape
    lc = k_c.shape[1]
    kw, vw = HEADS_PER_STEP * HEAD_PAD, HEADS_PER_STEP * V_DIM
    return pl.pallas_call(
        _attn_kernel,
        grid=(b, N_HEADS // HEADS_PER_STEP, l // BQ),
        in_specs=[pl.BlockSpec((1, BQ, kw), lambda bi, hp, i: (bi, i, hp)),
                  pl.BlockSpec((1, lc, kw), lambda bi, hp, i: (bi, 0, hp)),
                  pl.BlockSpec((1, l, kw), lambda bi, hp, i: (bi, 0, hp)),
                  pl.BlockSpec((1, vw, lc), lambda bi, hp, i: (bi, hp, 0)),
                  pl.BlockSpec((1, vw, l), lambda bi, hp, i: (bi, hp, 0))],
        out_specs=pl.BlockSpec((1, BQ, vw), lambda bi, hp, i: (bi, i, hp)),
        out_shape=jax.ShapeDtypeStruct((b, l, MLA_WIDTH), BF16),
        compiler_params=_params("parallel", "parallel", "arbitrary"),
        name="attn",
    )(q, k_c, k_l, v_c, v_l)


def _s5scan_kernel(n_ctx_chunks, uc_ref, ul_ref, w_ref, a_ref, v_ref, m_ref, dskip_ref, y_ref, hs_ref, st_ref):
    d = pl.program_id(0)
    i = pl.program_id(1)

    @pl.when(i == 0)
    def _():
        st_ref[...] = jnp.zeros_like(st_ref)

    u_bt = jnp.where(i < n_ctx_chunks, uc_ref[...], ul_ref[...])
    u = jnp.swapaxes(u_bt, 0, 1).reshape(N_ROWS // SUBLANES, S5_T, SUBLANES, S5_WIDTH)
    u_pos = [u[:, s].reshape(N_ROWS, S5_WIDTH) for s in range(S5_T)]

    def group_lanes(lt):
        lanes = slice(lt * LANES, (lt + 1) * LANES)
        return jnp.concatenate([p[:, lanes] for p in u_pos], axis=1).astype(BF16)

    def tile_cols(lt):
        base = (lt // TILES_PER_SLAB) * 2 * SLAB + (lt % TILES_PER_SLAB) * TILE_COLS
        return slice(base, base + TILE_COLS), slice(base + SLAB, base + SLAB + TILE_COLS)

    for lt in range(N_LANE_TILES):
        re, im = tile_cols(lt)
        inc = _dot(group_lanes(lt), w_ref[0, lt])
        hs_ref[:, re] = inc[:, :TILE_COLS]
        hs_ref[:, im] = inc[:, TILE_COLS:]

    n_steps = N_ROWS // SUBLANES
    for cs in range(0, 2 * S5_COLS, 2 * SLAB):
        re = slice(cs, cs + SLAB)
        im = slice(cs + SLAB, cs + 2 * SLAB)
        a_re, a_im = a_ref[0, :, re], a_ref[0, :, im]

        def step(tt, carry, re=re, im=im, a_re=a_re, a_im=a_im):
            h_re, h_im = carry
            t = jnp.where(d == 0, tt, n_steps - 1 - tt)
            rows = pl.ds(pl.multiple_of(t * SUBLANES, SUBLANES), SUBLANES)
            n_re = a_re * h_re - a_im * h_im + hs_ref[rows, re]
            n_im = a_re * h_im + a_im * h_re + hs_ref[rows, im]
            hs_ref[rows, re] = h_re
            hs_ref[rows, im] = h_im
            return n_re, n_im

        h_re, h_im = lax.fori_loop(0, n_steps, step, (st_ref[:, re], st_ref[:, im]), unroll=4)
        st_ref[:, re] = h_re
        st_ref[:, im] = h_im

    for lt in range(N_LANE_TILES):
        re, im = tile_cols(lt)
        lanes = slice(lt * LANES, (lt + 1) * LANES)
        y = (_dot(hs_ref[:, re].astype(BF16), v_ref[0, lt, :TILE_COLS])
             + _dot(hs_ref[:, im].astype(BF16), v_ref[0, lt, TILE_COLS:])
             + _dot(group_lanes(lt), m_ref[0, lt]))
        skip = dskip_ref[0, :, lanes]
        y_pos = [(y[:, s * LANES:(s + 1) * LANES] + u_pos[s][:, lanes] * skip)
                 .reshape(N_ROWS // SUBLANES, 1, SUBLANES, LANES) for s in range(S5_T)]
        y_tb = jnp.concatenate(y_pos, axis=1).reshape(TC, SUBLANES, LANES)
        y_ref[0, :, :, lanes] = jnp.swapaxes(y_tb, 0, 1)


def _s5scan(u_c, u_l, w_mat, a_b, v_mat, m_mat, dskip):
    b, l, _ = u_l.shape
    nc, nl = u_c.shape[1] // TC, l // TC

    def ctx_blk(d, i):
        return jnp.clip(jnp.where(d == 0, i, nc - 1 - i), 0, nc - 1)

    def lat_blk(d, i):
        return jnp.clip(jnp.where(d == 0, i - nc, nl - 1 - (i - nc)), 0, nl - 1)

    def out_blk(d, i):
        return jnp.where(i < nc, nl + ctx_blk(d, i), lat_blk(d, i))

    return pl.pallas_call(
        functools.partial(_s5scan_kernel, nc),
        grid=(2, nc + nl),
        in_specs=[pl.BlockSpec((b, TC, S5_WIDTH), lambda d, i: (0, ctx_blk(d, i), 0)),
                  pl.BlockSpec((b, TC, S5_WIDTH), lambda d, i: (0, lat_blk(d, i), 0)),
                  pl.BlockSpec((1, N_LANE_TILES, S5_T * LANES, 2 * TILE_COLS), lambda d, i: (d, 0, 0, 0)),
                  pl.BlockSpec((1, SUBLANES, 2 * S5_COLS), lambda d, i: (d, 0, 0)),
                  pl.BlockSpec((1, N_LANE_TILES, 2 * TILE_COLS, S5_T * LANES), lambda d, i: (d, 0, 0, 0)),
                  pl.BlockSpec((1, N_LANE_TILES, S5_T * LANES, S5_T * LANES), lambda d, i: (d, 0, 0, 0)),
                  pl.BlockSpec((1, 1, S5_WIDTH), lambda d, i: (d, 0, 0))],
        out_specs=pl.BlockSpec((1, b, TC, S5_WIDTH), lambda d, i: (d, 0, out_blk(d, i), 0)),
        out_shape=jax.ShapeDtypeStruct((2, b, (nc + nl) * TC, S5_WIDTH), F32),
        scratch_shapes=[pltpu.VMEM((N_ROWS, 2 * S5_COLS), F32),
                        pltpu.VMEM((SUBLANES, 2 * S5_COLS), F32)],
        compiler_params=_params("arbitrary", "arbitrary"),
        name="s5scan",
    )(u_c, u_l, w_mat, a_b, v_mat, m_mat, dskip)


def _tail_kernel(x_ref, att_ref, yf_ref, yb_ref, g1_ref, sh2_ref, sc2_ref, g2_ref, wglu_ref, bglu_ref, wout_ref,
                 ln1g_ref, ln1b_ref, wgu_ref, wd_ref, ln2g_ref, ln2b_ref, o_ref, acc_ref):
    z = jax.nn.gelu(yf_ref[0, 0] + yb_ref[0, 0])
    s5o = z * jax.nn.sigmoid(_dot(z.astype(BF16), wglu_ref[...]) + bglu_ref[...])
    mix = _dot(att_ref[0], wout_ref[:MLA_WIDTH, :]) + _dot(s5o.astype(BF16), wout_ref[MLA_WIDTH:, :])
    x1 = _layer_norm(DN_ALPHA * x_ref[0] + g1_ref[0] * mix) * ln1g_ref[...] + ln1b_ref[...]

    xm = (_layer_norm(x1) * (1.0 + sc2_ref[0]) + sh2_ref[0]).astype(BF16)
    for j in range(D_FF // FF_CHUNK):
        gate = _dot(xm, wgu_ref[:, j * FF_CHUNK:(j + 1) * FF_CHUNK])
        up = _dot(xm, wgu_ref[:, D_FF + j * FF_CHUNK:D_FF + (j + 1) * FF_CHUNK])
        hidden = (gate * jax.nn.sigmoid(gate) * up).astype(BF16)
        part = _dot(hidden, wd_ref[j * FF_CHUNK:(j + 1) * FF_CHUNK, :])
        if j == 0:
            acc_ref[...] = part
        else:
            acc_ref[...] += part
    o_ref[0] = _layer_norm(DN_ALPHA * x1 + g2_ref[0] * acc_ref[...]) * ln2g_ref[...] + ln2b_ref[...]


def _tail(x, att, y_dir, mod3, w_glu, b_glu, w_out, ln1_g, ln1_b, w_gu, w_down, ln2_g, ln2_b):
    b, l, _ = x.shape
    tok = lambda w: pl.BlockSpec((1, TM, w), lambda bi, i: (bi, i, 0))
    mod = lambda j: pl.BlockSpec((1, 1, D_MODEL), lambda bi, i: (bi, 0, j))
    y_of = lambda d: pl.BlockSpec((1, 1, TM, S5_WIDTH), lambda bi, i: (d, bi, i, 0))
    consts = (w_glu, b_glu, w_out, ln1_g, ln1_b, w_gu, w_down, ln2_g, ln2_b)
    return pl.pallas_call(
        _tail_kernel,
        grid=(b, l // TM),
        in_specs=[tok(D_MODEL), tok(MLA_WIDTH), y_of(0), y_of(1), mod(2), mod(3), mod(4), mod(5)]
                 + [_const_spec(w.shape) for w in consts],
        out_specs=tok(D_MODEL),
        out_shape=jax.ShapeDtypeStruct((b, l, D_MODEL), F32),
        scratch_shapes=[pltpu.VMEM((TM, D_MODEL), F32)],
        compiler_params=_params("parallel", "parallel"),
        name="tail",
    )(x, att, y_dir, y_dir, mod3, mod3, mod3, mod3, *consts)


def _head_lanes(nope, x0, x1, xp=jnp):
    pad = xp.zeros(nope.shape[:-1] + (LANES // 2 - (NOPE - NOPE_LO) - ROPE_HALF,), nope.dtype)
    return xp.concatenate([nope[..., :NOPE_LO], x0, nope[..., NOPE_LO:], pad, x1], axis=-1)


def _rope_tables(seq):
    pos = np.arange(seq)
    row = (pos // GRID_W).astype(np.float32)
    col = (pos % GRID_W).astype(np.float32)
    n_freq = ROPE // 4
    freqs = np.float32(ROPE_THETA) ** (-np.arange(n_freq, dtype=np.float32) / np.float32(n_freq))
    ang = np.concatenate([row[:, None] * freqs, col[:, None] * freqs], axis=-1).astype(np.float32)
    cos, sin = np.cos(ang), np.sin(ang)
    c = _head_lanes(np.ones((seq, NOPE), np.float32), cos, cos, xp=np)
    s = _head_lanes(np.zeros((seq, NOPE), np.float32), -sin, sin, xp=np)
    return jnp.asarray(c), jnp.asarray(s)


def kernel(x, c, ctx, c_ctx, w_ada, b_ada, w_in, q_norm_g, kv_norm_g, w_uq, w_uk, w_uv, s5_lambda_re, s5_lambda_im, s5_log_dt, s5_b_re, s5_b_im, s5_c_re, s5_c_im, s5_d, s5_w_glu, s5_b_glu, w_out, ln1_g, ln1_b, w_gate_up, w_down, ln2_g, ln2_b):
    assert w_ada.shape[0] == DEPTH == 1
    b, l, _ = x.shape
    assert b == SUBLANES, "the S5 scan maps the batch onto the sublanes of one vreg row"
    lc = ctx.shape[1]
    row2 = lambda t: t.reshape(1, -1)

    cond = jnp.concatenate([c, c_ctx[None, :], jnp.zeros((2 * SUBLANES - b - 1, D_MODEL), F32)], axis=0)
    mod = _adaln(cond, w_ada[0], row2(b_ada[0]))
    mod3 = mod.reshape(mod.shape[0], 1, 6 * D_MODEL)

    wi = w_in[0]
    first, second = (lambda w: w[..., 0::2]), (lambda w: w[..., 1::2])
    w_kr = wi[:, KV_END:ROPE_END]
    rope_tile = _head_lanes(jnp.zeros((D_MODEL, NOPE), F32), first(w_kr), second(w_kr))
    w_in_r = jnp.concatenate([wi[:, :KV_END], wi[:, ROPE_END:], rope_tile], axis=1).astype(BF16)
    uq = w_uq[0]
    wuq = _head_lanes(uq[..., :NOPE], first(uq[..., NOPE:]), second(uq[..., NOPE:]))
    wuq = wuq.reshape(Q_RANK, N_HEADS * HEAD_PAD).astype(BF16)
    no_rope = jnp.zeros((KV_RANK, N_HEADS, ROPE_HALF), F32)
    wuk = _head_lanes(w_uk[0], no_rope, no_rope).reshape(KV_RANK, N_HEADS * HEAD_PAD).astype(BF16)
    wuv = w_uv[0].reshape(KV_RANK, MLA_WIDTH).T.astype(BF16)
    qg, kvg = row2(q_norm_g[0]), row2(kv_norm_g[0])

    rope_c, rope_s = _rope_tables(l)
    ones_half = np.ones((lc, ROPE_HALF), np.float32)
    flat_c = jnp.asarray(_head_lanes(np.ones((lc, NOPE), np.float32), ones_half, ones_half, xp=np))
    flat_s = jnp.zeros((lc, LANES), F32)

    q, k_l, v_l, u_l = _inproj(x, mod3, None, w_in_r, qg, kvg, wuq, wuk, wuv, rope_c, rope_s, True)
    k_c, v_c, u_c = _inproj(ctx, mod3, b, w_in_r, qg, kvg, wuq, wuk, wuv, flat_c, flat_s, False)

    att = _attention(q, k_c, k_l, v_c, v_l)

    a_t, ab, ca, kk = _s5prep(s5_lambda_re[0], s5_lambda_im[0], s5_log_dt[0], s5_b_re[0], s5_b_im[0],
                              s5_c_re[0], s5_c_im[0])
    w_mat, v_mat, m_mat = _s5_matrices(ab, ca, kk)
    slabs = lambda t: t.reshape(2, S5_COLS // SLAB, 1, SLAB)
    a_cat = jnp.concatenate([slabs(a_t[0]), slabs(a_t[1])], axis=2).reshape(2, 2 * S5_COLS)
    a_b = jnp.broadcast_to(a_cat[:, None, :], (2, SUBLANES, 2 * S5_COLS))
    dskip = jnp.stack([s5_d[0], jnp.zeros_like(s5_d[0])])[:, None, :]
    y_dir = _s5scan(u_c, u_l, w_mat, a_b, v_mat, m_mat, dskip)

    return _tail(x, att, y_dir, mod3, s5_w_glu[0].astype(BF16), row2(s5_b_glu[0]), w_out[0].astype(BF16),
                 row2(ln1_g[0]), row2(ln1_b[0]), w_gate_up[0].astype(BF16), w_down[0].astype(BF16),
                 row2(ln2_g[0]), row2(ln2_b[0]))
```

```python
import functools
import math

import jax
import jax.numpy as jnp
import numpy as np
from jax import lax
from jax.experimental import pallas as pl
from jax.experimental.pallas import tpu as pltpu

D_MODEL = 1024
GRID_W = 64
N_HEADS = 8
NOPE = 64
ROPE = 32
V_DIM = 64
Q_RANK = 384
KV_RANK = 256
MLA_WIDTH = N_HEADS * V_DIM
S5_WIDTH = D_MODEL - MLA_WIDTH
S5_GROUP = 16
S5_GROUPS = S5_WIDTH // S5_GROUP
S5_STATE = 64
S5_COLS = S5_GROUPS * S5_STATE
KV_END = Q_RANK + KV_RANK
ROPE_END = KV_END + ROPE
D_FF = 2816
ROPE_THETA = 10000.0
NORM_EPS = 1e-6
DEPTH = 1
DN_ALPHA = (2.0 * DEPTH) ** 0.25

LANES = 128
SUBLANES = 8
HEAD_PAD = LANES
ROPE_HALF = ROPE // 2
NOPE_LO = LANES // 2 - ROPE_HALF
P_COLS = Q_RANK + KV_RANK + S5_WIDTH + LANES
VMEM_LIMIT = 56 * 1024 * 1024

TM = 512
BQ = 512
DEN_ROWS = 16
HEADS_PER_STEP = 8
BOUND_SLACK = 1.01
MAX_EXP2_SHIFT = 100.0
Q_SCALE = (NOPE + ROPE) ** -0.5 * math.log2(math.e)
TC = 128
S5_T = 4
N_ROWS = TC // S5_T * SUBLANES
N_LANE_TILES = S5_WIDTH // LANES
GROUPS_PER_TILE = LANES // S5_GROUP
TILE_COLS = GROUPS_PER_TILE * S5_STATE
TILES_PER_SLAB = 2
SLAB = TILES_PER_SLAB * TILE_COLS
FF_CHUNK = 256

F32 = jnp.float32
BF16 = jnp.bfloat16


def _params(*sem):
    return pltpu.CompilerParams(dimension_semantics=sem, vmem_limit_bytes=VMEM_LIMIT)


def _const_spec(shape):
    nd = len(shape)
    return pl.BlockSpec(shape, lambda *_: (0,) * nd, pipeline_mode=pl.Buffered(1))


def _layer_norm(x):
    mu = jnp.mean(x, axis=-1, keepdims=True)
    xc = x - mu
    var = jnp.mean(xc * xc, axis=-1, keepdims=True)
    return xc * lax.rsqrt(var + NORM_EPS)


def _rms_norm(x, g):
    return x * lax.rsqrt(jnp.mean(x * x, axis=-1, keepdims=True) + NORM_EPS) * g


def _dot(a, b):
    return jnp.dot(a, b, preferred_element_type=F32)


_NT = (((1,), (1,)), ((), ()))


def _adaln_kernel(cond_ref, w_ref, b_ref, o_ref):
    cnd = cond_ref[...]
    act = cnd * jax.nn.sigmoid(cnd)
    o_ref[...] = _dot(act.astype(BF16), w_ref[...].astype(BF16)) + b_ref[...]


def _adaln(cond, w_ada, b_ada):
    rows, n = cond.shape[0], w_ada.shape[1]
    tn = D_MODEL
    return pl.pallas_call(
        _adaln_kernel,
        grid=(n // tn,),
        in_specs=[pl.BlockSpec((rows, D_MODEL), lambda j: (0, 0)),
                  pl.BlockSpec((D_MODEL, tn), lambda j: (0, j)),
                  pl.BlockSpec((1, tn), lambda j: (0, j))],
        out_specs=pl.BlockSpec((rows, tn), lambda j: (0, j)),
        out_shape=jax.ShapeDtypeStruct((rows, n), F32),
        compiler_params=_params("arbitrary"),
        name="adaln",
    )(cond, w_ada, b_ada)


def _s5prep_kernel(lre_ref, lim_ref, ldt_ref, bre_ref, bim_ref, cre_ref, cim_ref, at_ref, ab_ref, ca_ref, kk_ref):
    lre, lim = lre_ref[...], lim_ref[...]
    dt = jnp.exp(ldt_ref[...])
    mag = jnp.exp(lre * dt)
    a_re, a_im = mag * jnp.cos(lim * dt), mag * jnp.sin(lim * dt)
    den = lre * lre + lim * lim
    f_re = ((a_re - 1) * lre + a_im * lim) / den
    f_im = (a_im * lre - (a_re - 1) * lim) / den
    b_re, b_im = bre_ref[...], bim_ref[...]
    c_re, c_im = cre_ref[...], cim_ref[...]

    def times_a(z_re, z_im):
        return a_re * z_re - a_im * z_im, a_re * z_im + a_im * z_re

    def over_states(x, y):
        return lax.dot_general(x, y, (((2,), (2,)), ((0,), (0,))), precision=lax.Precision.HIGHEST,
                               preferred_element_type=F32)

    p_re, p_im = f_re * b_re - f_im * b_im, f_re * b_im + f_im * b_re
    q_re, q_im = c_re, c_im
    w_re, w_im = jnp.ones_like(a_re), jnp.zeros_like(a_re)
    for k in range(S5_T):
        ab_ref[0, k] = p_re
        ab_ref[1, k] = p_im
        kk_ref[k] = over_states(c_re, p_re) - over_states(c_im, p_im)
        p_re, p_im = times_a(p_re, p_im)
        q_re, q_im = times_a(q_re, q_im)
        w_re, w_im = times_a(w_re, w_im)
        ca_ref[0, k] = q_re
        ca_ref[1, k] = q_im
    at_ref[0] = w_re
    at_ref[1] = w_im


def _s5prep(lam_re, lam_im, log_dt, b_re, b_im, c_re, c_im):
    n = 2 * S5_GROUPS
    full = (n, S5_GROUP, S5_STATE)
    per_state = lambda t: jnp.broadcast_to(t.reshape(n, 1, S5_STATE), full)
    ldt = jnp.broadcast_to(log_dt.reshape(n, 1, 1), full)
    chan_state = lambda t: t.reshape(full)
    shp = lambda *lead: jax.ShapeDtypeStruct(lead + full, F32)
    at, ab, ca, kk = pl.pallas_call(
        _s5prep_kernel,
        out_shape=(shp(2), shp(2, S5_T), shp(2, S5_T), jax.ShapeDtypeStruct((S5_T, n, S5_GROUP, S5_GROUP), F32)),
        compiler_params=pltpu.CompilerParams(vmem_limit_bytes=VMEM_LIMIT),
        name="s5prep",
    )(per_state(lam_re), per_state(lam_im), ldt, chan_state(jnp.swapaxes(b_re, -1, -2)),
      chan_state(jnp.swapaxes(b_im, -1, -2)), chan_state(c_re), chan_state(c_im))
    by_dir = lambda t, lead: t.reshape(lead + (2, S5_GROUPS) + t.shape[-2:])
    return by_dir(at, (2,))[:, :, :, 0], by_dir(ab, (2, S5_T)), by_dir(ca, (2, S5_T)), by_dir(kk, (S5_T,))


def _spread_groups(x, rows_per_group, col_width, n_col_blocks):
    g8 = GROUPS_PER_TILE
    n, rows, k = x.shape
    spread = np.kron(np.eye(n_col_blocks), np.kron(np.ones((1, g8)), np.eye(col_width))).astype(np.float32)
    cols = spread.shape[1]

    def body(x_ref, spread_ref, o_ref):
        tiled = _dot(x_ref[0].astype(BF16), spread_ref[...])
        row_group = (lax.broadcasted_iota(jnp.int32, (rows, cols), 0) // rows_per_group) % g8
        col_group = (lax.broadcasted_iota(jnp.int32, (rows, cols), 1) // col_width) % g8
        o_ref[0] = jnp.where(row_group == col_group, tiled, 0.0).astype(BF16)

    return pl.pallas_call(
        body,
        grid=(n,),
        in_specs=[pl.BlockSpec((1, rows, k), lambda i: (i, 0, 0)), pl.BlockSpec((k, cols), lambda i: (0, 0))],
        out_specs=pl.BlockSpec((1, rows, cols), lambda i: (i, 0, 0)),
        out_shape=jax.ShapeDtypeStruct((n, rows, cols), BF16),
        compiler_params=_params("parallel"),
        name="s5place",
    )(x, jnp.asarray(spread, dtype=BF16))


def _s5_matrices(ab, ca, kk):
    t_, g8, nt = S5_T, GROUPS_PER_TILE, N_LANE_TILES
    w_all, v_all, m_all = [], [], []
    for d in range(2):
        walked = (lambda s: s) if d == 0 else (lambda s: t_ - 1 - s)
        ab_d = jnp.stack([ab[:, t_ - 1 - walked(s), d] for s in range(t_)], axis=1)
        ab_d = ab_d.reshape(2, t_, nt, g8, S5_GROUP, S5_STATE).transpose(2, 1, 3, 4, 0, 5)
        w_all.append(ab_d.reshape(nt, t_ * LANES, 2 * S5_STATE))
        ca_d = jnp.stack([ca[:, walked(t), d] for t in range(t_)], axis=1)
        ca_d = ca_d * jnp.array([1.0, -1.0], F32).reshape(2, 1, 1, 1, 1)
        ca_d = ca_d.reshape(2, t_, nt, g8, S5_GROUP, S5_STATE).transpose(2, 0, 3, 5, 1, 4)
        v_all.append(ca_d.reshape(nt, 2 * TILE_COLS, t_ * S5_GROUP))
        zero = jnp.zeros_like(kk[0, d])
        blocks = jnp.stack([jnp.stack([kk[walked(t) - walked(s), d] if walked(t) >= walked(s) else zero
                                       for t in range(t_)], axis=0) for s in range(t_)], axis=0)
        blocks = blocks.reshape(t_, t_, nt, g8, S5_GROUP, S5_GROUP).transpose(2, 0, 3, 5, 1, 4)
        m_all.append(blocks.reshape(nt, t_ * LANES, t_ * S5_GROUP))

    def spread(parts, *args):
        out = _spread_groups(jnp.concatenate(parts, axis=0), *args)
        return out.reshape((2, nt) + out.shape[1:])

    return (spread(w_all, S5_GROUP, S5_STATE, 2), spread(v_all, S5_STATE, S5_GROUP, t_),
            spread(m_all, S5_GROUP, S5_GROUP, t_))


def _rope_tile(t, c, s):
    return t * c + pltpu.roll(t, LANES // 2, 1) * s


def _inproj_kernel(with_q, x_ref, sh_ref, sc_ref, win_ref, qg_ref, kvg_ref, wuq_ref, wuk_ref, wuv_ref,
                   c_ref, s_ref, *out_refs):
    if with_q:
        q_ref, k_ref, v_ref, u_ref = out_refs
    else:
        k_ref, v_ref, u_ref = out_refs
    x = x_ref[0]
    xm = _layer_norm(x) * (1.0 + sc_ref[0]) + sh_ref[0]
    p = _dot(xm.astype(BF16), win_ref[...])
    c, s = c_ref[...], s_ref[...]

    kv_c = _rms_norm(p[:, Q_RANK:KV_END], kvg_ref[...]).astype(BF16)
    k = _dot(kv_c, wuk_ref[...])
    kr = _rope_tile(p[:, KV_END + S5_WIDTH:], c, s)
    for h in range(N_HEADS):
        k_ref[0, :, h * HEAD_PAD:(h + 1) * HEAD_PAD] = (k[:, h * HEAD_PAD:(h + 1) * HEAD_PAD] + kr).astype(BF16)
    v_ref[0] = lax.dot_general(wuv_ref[...], kv_c, _NT, preferred_element_type=F32).astype(BF16)
    u_ref[0] = p[:, KV_END:KV_END + S5_WIDTH]

    if with_q:
        q_c = _rms_norm(p[:, :Q_RANK], qg_ref[...]).astype(BF16)
        q = _dot(q_c, wuq_ref[...])
        cq, sq = c * Q_SCALE, s * Q_SCALE
        for h in range(N_HEADS):
            sl = slice(h * HEAD_PAD, (h + 1) * HEAD_PAD)
            q_ref[0, :, sl] = _rope_tile(q[:, sl], cq, sq).astype(BF16)


def _inproj(x, mod3, mod_row, w_in_r, qg, kvg, wuq, wuk, wuv, rope_c, rope_s, with_q):
    b, l, _ = x.shape
    tm = min(TM, l)
    row = (lambda bi: bi) if mod_row is None else (lambda bi: mod_row)
    tok = lambda w: pl.BlockSpec((1, tm, w), lambda bi, i: (bi, i, 0))
    in_specs = [
        tok(D_MODEL),
        pl.BlockSpec((1, 1, D_MODEL), lambda bi, i: (row(bi), 0, 0)),
        pl.BlockSpec((1, 1, D_MODEL), lambda bi, i: (row(bi), 0, 1)),
        _const_spec(w_in_r.shape), _const_spec(qg.shape), _const_spec(kvg.shape),
        _const_spec(wuq.shape), _const_spec(wuk.shape), _const_spec(wuv.shape),
        pl.BlockSpec((tm, LANES), lambda bi, i: (i, 0)),
        pl.BlockSpec((tm, LANES), lambda bi, i: (i, 0)),
    ]
    out_specs = [tok(N_HEADS * HEAD_PAD), pl.BlockSpec((1, MLA_WIDTH, tm), lambda bi, i: (bi, 0, i)), tok(S5_WIDTH)]
    out_shape = [jax.ShapeDtypeStruct((b, l, N_HEADS * HEAD_PAD), BF16),
                 jax.ShapeDtypeStruct((b, MLA_WIDTH, l), BF16),
                 jax.ShapeDtypeStruct((b, l, S5_WIDTH), F32)]
    if with_q:
        out_specs = [tok(N_HEADS * HEAD_PAD)] + out_specs
        out_shape = [jax.ShapeDtypeStruct((b, l, N_HEADS * HEAD_PAD), BF16)] + out_shape
    return pl.pallas_call(
        functools.partial(_inproj_kernel, with_q),
        grid=(b, l // tm),
        in_specs=in_specs, out_specs=out_specs, out_shape=out_shape,
        compiler_params=_params("parallel", "parallel"),
        name="inproj_lat" if with_q else "inproj_ctx",
    )(x, mod3, mod3, w_in_r, qg, kvg, wuq, wuk, wuv, rope_c, rope_s)


def _attn_kernel(bounded, *refs):
    if bounded:
        bound_ref, q_ref, kc_ref, kl_ref, vc_ref, vl_ref, o_ref = refs
    else:
        q_ref, kc_ref, kl_ref, vc_ref, vl_ref, o_ref = refs

    def scores(h):
        ks = slice(h * HEAD_PAD, (h + 1) * HEAD_PAD)
        q = q_ref[0, :, ks]
        return (lax.dot_general(kc_ref[0, :, ks], q, _NT, preferred_element_type=F32),
                lax.dot_general(kl_ref[0, :, ks], q, _NT, preferred_element_type=F32))

    def with_ones(vt):
        return jnp.concatenate([vt, jnp.ones((DEN_ROWS, vt.shape[1]), BF16)], axis=0)

    outs = []
    s_next = scores(0)
    for h in range(HEADS_PER_STEP):
        s_c, s_l = s_next
        if h + 1 < HEADS_PER_STEP:
            s_next = scores(h + 1)
        vs = slice(h * V_DIM, (h + 1) * V_DIM)
        if bounded:
            m = bound_ref[0]
        else:
            m = jnp.maximum(jnp.max(s_c, axis=0, keepdims=True), jnp.max(s_l, axis=0, keepdims=True))
        e_c = jnp.exp2(s_c - m).astype(BF16)
        e_l = jnp.exp2(s_l - m).astype(BF16)
        o_t = _dot(with_ones(vc_ref[0, vs, :]), e_c) + _dot(with_ones(vl_ref[0, vs, :]), e_l)
        outs.append(o_t[:V_DIM] / o_t[V_DIM:V_DIM + 1])
    o_ref[0] = jnp.concatenate(outs, axis=0).T.astype(o_ref.dtype)


def _attention(q, k_c, k_l, v_c, v_l, bound=None):
    b, l, _ = q.shape
    lc = k_c.shape[1]
    kw, vw = HEADS_PER_STEP * HEAD_PAD, HEADS_PER_STEP * V_DIM
    bounded = bound is not None
    in_specs = [pl.BlockSpec((1, BQ, kw), lambda bi, hp, i: (bi, i, hp)),
                pl.BlockSpec((1, lc, kw), lambda bi, hp, i: (bi, 0, hp)),
                pl.BlockSpec((1, l, kw), lambda bi, hp, i: (bi, 0, hp)),
                pl.BlockSpec((1, vw, lc), lambda bi, hp, i: (bi, hp, 0)),
                pl.BlockSpec((1, vw, l), lambda bi, hp, i: (bi, hp, 0))]
    args = (q, k_c, k_l, v_c, v_l)
    if bounded:
        in_specs = [pl.BlockSpec(memory_space=pltpu.SMEM)] + in_specs
        args = (bound.reshape(1).astype(F32),) + args
    return pl.pallas_call(
        functools.partial(_attn_kernel, bounded),
        grid=(b, N_HEADS // HEADS_PER_STEP, l // BQ),
        in_specs=in_specs,
        out_specs=pl.BlockSpec((1, BQ, vw), lambda bi, hp, i: (bi, i, hp)),
        out_shape=jax.ShapeDtypeStruct((b, l, MLA_WIDTH), BF16),
        compiler_params=_params("parallel", "parallel", "arbitrary"),
        name="attn_bounded" if bounded else "attn",
    )(*args)


def _score_bound(q, k_c, k_l):
    def max_sq_norm(t):
        t = t.astype(F32).reshape(t.shape[0], t.shape[1], N_HEADS, HEAD_PAD)
        return jnp.max(jnp.sum(t * t, axis=-1))

    bound = BOUND_SLACK * jnp.sqrt(max_sq_norm(q) * jnp.maximum(max_sq_norm(k_c), max_sq_norm(k_l)))
    return bound, 2.0 * bound <= MAX_EXP2_SHIFT


def _s5scan_kernel(n_ctx_chunks, uc_ref, ul_ref, w_ref, a_ref, v_ref, m_ref, dskip_ref, y_ref, hs_ref, st_ref):
    d = pl.program_id(0)
    i = pl.program_id(1)

    @pl.when(i == 0)
    def _():
        st_ref[...] = jnp.zeros_like(st_ref)

    u_bt = jnp.where(i < n_ctx_chunks, uc_ref[...], ul_ref[...])
    u = jnp.swapaxes(u_bt, 0, 1).reshape(N_ROWS // SUBLANES, S5_T, SUBLANES, S5_WIDTH)
    u_pos = [u[:, s].reshape(N_ROWS, S5_WIDTH) for s in range(S5_T)]

    def group_lanes(lt):
        lanes = slice(lt * LANES, (lt + 1) * LANES)
        return jnp.concatenate([p[:, lanes] for p in u_pos], axis=1).astype(BF16)

    def tile_cols(lt):
        base = (lt // TILES_PER_SLAB) * 2 * SLAB + (lt % TILES_PER_SLAB) * TILE_COLS
        return slice(base, base + TILE_COLS), slice(base + SLAB, base + SLAB + TILE_COLS)

    for lt in range(N_LANE_TILES):
        re, im = tile_cols(lt)
        inc = _dot(group_lanes(lt), w_ref[0, lt])
        hs_ref[:, re] = inc[:, :TILE_COLS]
        hs_ref[:, im] = inc[:, TILE_COLS:]

    n_steps = N_ROWS // SUBLANES
    for cs in range(0, 2 * S5_COLS, 2 * SLAB):
        re = slice(cs, cs + SLAB)
        im = slice(cs + SLAB, cs + 2 * SLAB)
        a_re, a_im = a_ref[0, :, re], a_ref[0, :, im]

        def step(tt, carry, re=re, im=im, a_re=a_re, a_im=a_im):
            h_re, h_im = carry
            t = jnp.where(d == 0, tt, n_steps - 1 - tt)
            rows = pl.ds(pl.multiple_of(t * SUBLANES, SUBLANES), SUBLANES)
            n_re = a_re * h_re - a_im * h_im + hs_ref[rows, re]
            n_im = a_re * h_im + a_im * h_re + hs_ref[rows, im]
            hs_ref[rows, re] = h_re
            hs_ref[rows, im] = h_im
            return n_re, n_im

        h_re, h_im = lax.fori_loop(0, n_steps, step, (st_ref[:, re], st_ref[:, im]), unroll=4)
        st_ref[:, re] = h_re
        st_ref[:, im] = h_im

    for lt in range(N_LANE_TILES):
        re, im = tile_cols(lt)
        lanes = slice(lt * LANES, (lt + 1) * LANES)
        y = (_dot(hs_ref[:, re].astype(BF16), v_ref[0, lt, :TILE_COLS])
             + _dot(hs_ref[:, im].astype(BF16), v_ref[0, lt, TILE_COLS:])
             + _dot(group_lanes(lt), m_ref[0, lt]))
        skip = dskip_ref[0, :, lanes]
        y_pos = [(y[:, s * LANES:(s + 1) * LANES] + u_pos[s][:, lanes] * skip)
                 .reshape(N_ROWS // SUBLANES, 1, SUBLANES, LANES) for s in range(S5_T)]
        y_tb = jnp.concatenate(y_pos, axis=1).reshape(TC, SUBLANES, LANES)
        y_ref[0, :, :, lanes] = jnp.swapaxes(y_tb, 0, 1)


def _s5scan(u_c, u_l, w_mat, a_b, v_mat, m_mat, dskip):
    b, l, _ = u_l.shape
    nc, nl = u_c.shape[1] // TC, l // TC

    def ctx_blk(d, i):
        return jnp.clip(jnp.where(d == 0, i, nc - 1 - i), 0, nc - 1)

    def lat_blk(d, i):
        return jnp.clip(jnp.where(d == 0, i - nc, nl - 1 - (i - nc)), 0, nl - 1)

    def out_blk(d, i):
        return jnp.where(i < nc, nl + ctx_blk(d, i), lat_blk(d, i))

    return pl.pallas_call(
        functools.partial(_s5scan_kernel, nc),
        grid=(2, nc + nl),
        in_specs=[pl.BlockSpec((b, TC, S5_WIDTH), lambda d, i: (0, ctx_blk(d, i), 0)),
                  pl.BlockSpec((b, TC, S5_WIDTH), lambda d, i: (0, lat_blk(d, i), 0)),
                  pl.BlockSpec((1, N_LANE_TILES, S5_T * LANES, 2 * TILE_COLS), lambda d, i: (d, 0, 0, 0)),
                  pl.BlockSpec((1, SUBLANES, 2 * S5_COLS), lambda d, i: (d, 0, 0)),
                  pl.BlockSpec((1, N_LANE_TILES, 2 * TILE_COLS, S5_T * LANES), lambda d, i: (d, 0, 0, 0)),
                  pl.BlockSpec((1, N_LANE_TILES, S5_T * LANES, S5_T * LANES), lambda d, i: (d, 0, 0, 0)),
                  pl.BlockSpec((1, 1, S5_WIDTH), lambda d, i: (d, 0, 0))],
        out_specs=pl.BlockSpec((1, b, TC, S5_WIDTH), lambda d, i: (d, 0, out_blk(d, i), 0)),
        out_shape=jax.ShapeDtypeStruct((2, b, (nc + nl) * TC, S5_WIDTH), F32),
        scratch_shapes=[pltpu.VMEM((N_ROWS, 2 * S5_COLS), F32),
                        pltpu.VMEM((SUBLANES, 2 * S5_COLS), F32)],
        compiler_params=_params("arbitrary", "arbitrary"),
        name="s5scan",
    )(u_c, u_l, w_mat, a_b, v_mat, m_mat, dskip)


def _tail_kernel(x_ref, att_ref, yf_ref, yb_ref, g1_ref, sh2_ref, sc2_ref, g2_ref, wglu_ref, bglu_ref, wout_ref,
                 ln1g_ref, ln1b_ref, wgu_ref, wd_ref, ln2g_ref, ln2b_ref, o_ref, acc_ref):
    z = jax.nn.gelu(yf_ref[0, 0] + yb_ref[0, 0])
    s5o = z * jax.nn.sigmoid(_dot(z.astype(BF16), wglu_ref[...]) + bglu_ref[...])
    mix = _dot(att_ref[0], wout_ref[:MLA_WIDTH, :]) + _dot(s5o.astype(BF16), wout_ref[MLA_WIDTH:, :])
    x1 = _layer_norm(DN_ALPHA * x_ref[0] + g1_ref[0] * mix) * ln1g_ref[...] + ln1b_ref[...]

    xm = (_layer_norm(x1) * (1.0 + sc2_ref[0]) + sh2_ref[0]).astype(BF16)
    for j in range(D_FF // FF_CHUNK):
        gate = _dot(xm, wgu_ref[:, j * FF_CHUNK:(j + 1) * FF_CHUNK])
        up = _dot(xm, wgu_ref[:, D_FF + j * FF_CHUNK:D_FF + (j + 1) * FF_CHUNK])
        hidden = (gate * jax.nn.sigmoid(gate) * up).astype(BF16)
        part = _dot(hidden, wd_ref[j * FF_CHUNK:(j + 1) * FF_CHUNK, :])
        if j == 0:
            acc_ref[...] = part
        else:
            acc_ref[...] += part
    o_ref[0] = _layer_norm(DN_ALPHA * x1 + g2_ref[0] * acc_ref[...]) * ln2g_ref[...] + ln2b_ref[...]


def _tail(x, att, y_dir, mod3, w_glu, b_glu, w_out, ln1_g, ln1_b, w_gu, w_down, ln2_g, ln2_b):
    b, l, _ = x.shape
    tok = lambda w: pl.BlockSpec((1, TM, w), lambda bi, i: (bi, i, 0))
    mod = lambda j: pl.BlockSpec((1, 1, D_MODEL), lambda bi, i: (bi, 0, j))
    y_of = lambda d: pl.BlockSpec((1, 1, TM, S5_WIDTH), lambda bi, i: (d, bi, i, 0))
    consts = (w_glu, b_glu, w_out, ln1_g, ln1_b, w_gu, w_down, ln2_g, ln2_b)
    return pl.pallas_call(
        _tail_kernel,
        grid=(b, l // TM),
        in_specs=[tok(D_MODEL), tok(MLA_WIDTH), y_of(0), y_of(1), mod(2), mod(3), mod(4), mod(5)]
                 + [_const_spec(w.shape) for w in consts],
        out_specs=tok(D_MODEL),
        out_shape=jax.ShapeDtypeStruct((b, l, D_MODEL), F32),
        scratch_shapes=[pltpu.VMEM((TM, D_MODEL), F32)],
        compiler_params=_params("parallel", "parallel"),
        name="tail",
    )(x, att, y_dir, y_dir, mod3, mod3, mod3, mod3, *consts)


def _head_lanes(nope, x0, x1, xp=jnp):
    pad = xp.zeros(nope.shape[:-1] + (LANES // 2 - (NOPE - NOPE_LO) - ROPE_HALF,), nope.dtype)
    return xp.concatenate([nope[..., :NOPE_LO], x0, nope[..., NOPE_LO:], pad, x1], axis=-1)


def _rope_tables(seq):
    pos = np.arange(seq)
    row = (pos // GRID_W).astype(np.float32)
    col = (pos % GRID_W).astype(np.float32)
    n_freq = ROPE // 4
    freqs = np.float32(ROPE_THETA) ** (-np.arange(n_freq, dtype=np.float32) / np.float32(n_freq))
    ang = np.concatenate([row[:, None] * freqs, col[:, None] * freqs], axis=-1).astype(np.float32)
    cos, sin = np.cos(ang), np.sin(ang)
    c = _head_lanes(np.ones((seq, NOPE), np.float32), cos, cos, xp=np)
    s = _head_lanes(np.zeros((seq, NOPE), np.float32), -sin, sin, xp=np)
    return jnp.asarray(c), jnp.asarray(s)


def kernel(x, c, ctx, c_ctx, w_ada, b_ada, w_in, q_norm_g, kv_norm_g, w_uq, w_uk, w_uv, s5_lambda_re, s5_lambda_im, s5_log_dt, s5_b_re, s5_b_im, s5_c_re, s5_c_im, s5_d, s5_w_glu, s5_b_glu, w_out, ln1_g, ln1_b, w_gate_up, w_down, ln2_g, ln2_b):
    assert w_ada.shape[0] == DEPTH == 1
    b, l, _ = x.shape
    assert b == SUBLANES, "the S5 scan maps the batch onto the sublanes of one vreg row"
    lc = ctx.shape[1]
    row2 = lambda t: t.reshape(1, -1)

    cond = jnp.concatenate([c, c_ctx[None, :], jnp.zeros((2 * SUBLANES - b - 1, D_MODEL), F32)], axis=0)
    mod = _adaln(cond, w_ada[0], row2(b_ada[0]))
    mod3 = mod.reshape(mod.shape[0], 1, 6 * D_MODEL)

    wi = w_in[0]
    first, second = (lambda w: w[..., 0::2]), (lambda w: w[..., 1::2])
    w_kr = wi[:, KV_END:ROPE_END]
    rope_tile = _head_lanes(jnp.zeros((D_MODEL, NOPE), F32), first(w_kr), second(w_kr))
    w_in_r = jnp.concatenate([wi[:, :KV_END], wi[:, ROPE_END:], rope_tile], axis=1).astype(BF16)
    uq = w_uq[0]
    wuq = _head_lanes(uq[..., :NOPE], first(uq[..., NOPE:]), second(uq[..., NOPE:]))
    wuq = wuq.reshape(Q_RANK, N_HEADS * HEAD_PAD).astype(BF16)
    no_rope = jnp.zeros((KV_RANK, N_HEADS, ROPE_HALF), F32)
    wuk = _head_lanes(w_uk[0], no_rope, no_rope).reshape(KV_RANK, N_HEADS * HEAD_PAD).astype(BF16)
    wuv = w_uv[0].reshape(KV_RANK, MLA_WIDTH).T.astype(BF16)
    qg, kvg = row2(q_norm_g[0]), row2(kv_norm_g[0])

    rope_c, rope_s = _rope_tables(l)
    ones_half = np.ones((lc, ROPE_HALF), np.float32)
    flat_c = jnp.asarray(_head_lanes(np.ones((lc, NOPE), np.float32), ones_half, ones_half, xp=np))
    flat_s = jnp.zeros((lc, LANES), F32)

    q, k_l, v_l, u_l = _inproj(x, mod3, None, w_in_r, qg, kvg, wuq, wuk, wuv, rope_c, rope_s, True)
    k_c, v_c, u_c = _inproj(ctx, mod3, b, w_in_r, qg, kvg, wuq, wuk, wuv, flat_c, flat_s, False)

    bound, bound_is_safe = _score_bound(q, k_c, k_l)
    att = lax.cond(bound_is_safe,
                   lambda: _attention(q, k_c, k_l, v_c, v_l, bound),
                   lambda: _attention(q, k_c, k_l, v_c, v_l))

    a_t, ab, ca, kk = _s5prep(s5_lambda_re[0], s5_lambda_im[0], s5_log_dt[0], s5_b_re[0], s5_b_im[0],
                              s5_c_re[0], s5_c_im[0])
    w_mat, v_mat, m_mat = _s5_matrices(ab, ca, kk)
    slabs = lambda t: t.reshape(2, S5_COLS // SLAB, 1, SLAB)
    a_cat = jnp.concatenate([slabs(a_t[0]), slabs(a_t[1])], axis=2).reshape(2, 2 * S5_COLS)
    a_b = jnp.broadcast_to(a_cat[:, None, :], (2, SUBLANES, 2 * S5_COLS))
    dskip = jnp.stack([s5_d[0], jnp.zeros_like(s5_d[0])])[:, None, :]
    y_dir = _s5scan(u_c, u_l, w_mat, a_b, v_mat, m_mat, dskip)

    return _tail(x, att, y_dir, mod3, s5_w_glu[0].astype(BF16), row2(s5_b_glu[0]), w_out[0].astype(BF16),
                 row2(ln1_g[0]), row2(ln1_b[0]), w_gate_up[0].astype(BF16), w_down[0].astype(BF16),
                 row2(ln2_g[0]), row2(ln2_b[0]))
```

```python
import functools
import math

import jax
import jax.numpy as jnp
import numpy as np
from jax import lax
from jax.experimental import pallas as pl
from jax.experimental.pallas import tpu as pltpu

D_MODEL = 1024
GRID_W = 64
N_HEADS = 8
NOPE = 64
ROPE = 32
V_DIM = 64
Q_RANK = 384
KV_RANK = 256
MLA_WIDTH = N_HEADS * V_DIM
S5_WIDTH = D_MODEL - MLA_WIDTH
S5_GROUP = 16
S5_GROUPS = S5_WIDTH // S5_GROUP
S5_STATE = 64
S5_COLS = S5_GROUPS * S5_STATE
KV_END = Q_RANK + KV_RANK
ROPE_END = KV_END + ROPE
D_FF = 2816
ROPE_THETA = 10000.0
NORM_EPS = 1e-6
DEPTH = 1
DN_ALPHA = (2.0 * DEPTH) ** 0.25

LANES = 128
SUBLANES = 8
HEAD_PAD = LANES
ROPE_HALF = ROPE // 2
NOPE_LO = LANES // 2 - ROPE_HALF
P_COLS = Q_RANK + KV_RANK + S5_WIDTH + LANES
VMEM_LIMIT = 56 * 1024 * 1024

TM = 512
BQ = 512
DEN_ROWS = 16
HEADS_PER_STEP = 8
DEN_MIN, DEN_MAX = 2.0 ** -90, 2.0 ** 100
Q_SCALE = (NOPE + ROPE) ** -0.5 * math.log2(math.e)
TC = 128
S5_T = 4
N_ROWS = TC // S5_T * SUBLANES
N_LANE_TILES = S5_WIDTH // LANES
GROUPS_PER_TILE = LANES // S5_GROUP
TILE_COLS = GROUPS_PER_TILE * S5_STATE
TILES_PER_SLAB = 2
SLAB = TILES_PER_SLAB * TILE_COLS
FF_CHUNK = 256

F32 = jnp.float32
BF16 = jnp.bfloat16


def _params(*sem):
    return pltpu.CompilerParams(dimension_semantics=sem, vmem_limit_bytes=VMEM_LIMIT)


def _const_spec(shape):
    nd = len(shape)
    return pl.BlockSpec(shape, lambda *_: (0,) * nd, pipeline_mode=pl.Buffered(1))


def _layer_norm(x):
    mu = jnp.mean(x, axis=-1, keepdims=True)
    xc = x - mu
    var = jnp.mean(xc * xc, axis=-1, keepdims=True)
    return xc * lax.rsqrt(var + NORM_EPS)


def _rms_norm(x, g):
    return x * lax.rsqrt(jnp.mean(x * x, axis=-1, keepdims=True) + NORM_EPS) * g


def _dot(a, b):
    return jnp.dot(a, b, preferred_element_type=F32)


_NT = (((1,), (1,)), ((), ()))


def _adaln_kernel(cond_ref, w_ref, b_ref, o_ref):
    cnd = cond_ref[...]
    act = cnd * jax.nn.sigmoid(cnd)
    o_ref[...] = _dot(act.astype(BF16), w_ref[...].astype(BF16)) + b_ref[...]


def _adaln(cond, w_ada, b_ada):
    rows, n = cond.shape[0], w_ada.shape[1]
    tn = D_MODEL
    return pl.pallas_call(
        _adaln_kernel,
        grid=(n // tn,),
        in_specs=[pl.BlockSpec((rows, D_MODEL), lambda j: (0, 0)),
                  pl.BlockSpec((D_MODEL, tn), lambda j: (0, j)),
                  pl.BlockSpec((1, tn), lambda j: (0, j))],
        out_specs=pl.BlockSpec((rows, tn), lambda j: (0, j)),
        out_shape=jax.ShapeDtypeStruct((rows, n), F32),
        compiler_params=_params("arbitrary"),
        name="adaln",
    )(cond, w_ada, b_ada)


def _s5prep_kernel(lre_ref, lim_ref, ldt_ref, bre_ref, bim_ref, cre_ref, cim_ref, at_ref, ab_ref, ca_ref, kk_ref):
    lre, lim = lre_ref[...], lim_ref[...]
    dt = jnp.exp(ldt_ref[...])
    mag = jnp.exp(lre * dt)
    a_re, a_im = mag * jnp.cos(lim * dt), mag * jnp.sin(lim * dt)
    den = lre * lre + lim * lim
    f_re = ((a_re - 1) * lre + a_im * lim) / den
    f_im = (a_im * lre - (a_re - 1) * lim) / den
    b_re, b_im = bre_ref[...], bim_ref[...]
    c_re, c_im = cre_ref[...], cim_ref[...]

    def times_a(z_re, z_im):
        return a_re * z_re - a_im * z_im, a_re * z_im + a_im * z_re

    def over_states(x, y):
        return lax.dot_general(x, y, (((2,), (2,)), ((0,), (0,))), precision=lax.Precision.HIGHEST,
                               preferred_element_type=F32)

    p_re, p_im = f_re * b_re - f_im * b_im, f_re * b_im + f_im * b_re
    q_re, q_im = c_re, c_im
    w_re, w_im = jnp.ones_like(a_re), jnp.zeros_like(a_re)
    for k in range(S5_T):
        ab_ref[0, k] = p_re
        ab_ref[1, k] = p_im
        kk_ref[k] = over_states(c_re, p_re) - over_states(c_im, p_im)
        p_re, p_im = times_a(p_re, p_im)
        q_re, q_im = times_a(q_re, q_im)
        w_re, w_im = times_a(w_re, w_im)
        ca_ref[0, k] = q_re
        ca_ref[1, k] = q_im
    at_ref[0] = w_re
    at_ref[1] = w_im


def _s5prep(lam_re, lam_im, log_dt, b_re, b_im, c_re, c_im):
    n = 2 * S5_GROUPS
    full = (n, S5_GROUP, S5_STATE)
    per_state = lambda t: jnp.broadcast_to(t.reshape(n, 1, S5_STATE), full)
    ldt = jnp.broadcast_to(log_dt.reshape(n, 1, 1), full)
    chan_state = lambda t: t.reshape(full)
    shp = lambda *lead: jax.ShapeDtypeStruct(lead + full, F32)
    at, ab, ca, kk = pl.pallas_call(
        _s5prep_kernel,
        out_shape=(shp(2), shp(2, S5_T), shp(2, S5_T), jax.ShapeDtypeStruct((S5_T, n, S5_GROUP, S5_GROUP), F32)),
        compiler_params=pltpu.CompilerParams(vmem_limit_bytes=VMEM_LIMIT),
        name="s5prep",
    )(per_state(lam_re), per_state(lam_im), ldt, chan_state(jnp.swapaxes(b_re, -1, -2)),
      chan_state(jnp.swapaxes(b_im, -1, -2)), chan_state(c_re), chan_state(c_im))
    by_dir = lambda t, lead: t.reshape(lead + (2, S5_GROUPS) + t.shape[-2:])
    return by_dir(at, (2,))[:, :, :, 0], by_dir(ab, (2, S5_T)), by_dir(ca, (2, S5_T)), by_dir(kk, (S5_T,))


def _spread_groups(x, rows_per_group, col_width, n_col_blocks):
    g8 = GROUPS_PER_TILE
    n, rows, k = x.shape
    spread = np.kron(np.eye(n_col_blocks), np.kron(np.ones((1, g8)), np.eye(col_width))).astype(np.float32)
    cols = spread.shape[1]

    def body(x_ref, spread_ref, o_ref):
        tiled = _dot(x_ref[0].astype(BF16), spread_ref[...])
        row_group = (lax.broadcasted_iota(jnp.int32, (rows, cols), 0) // rows_per_group) % g8
        col_group = (lax.broadcasted_iota(jnp.int32, (rows, cols), 1) // col_width) % g8
        o_ref[0] = jnp.where(row_group == col_group, tiled, 0.0).astype(BF16)

    return pl.pallas_call(
        body,
        grid=(n,),
        in_specs=[pl.BlockSpec((1, rows, k), lambda i: (i, 0, 0)), pl.BlockSpec((k, cols), lambda i: (0, 0))],
        out_specs=pl.BlockSpec((1, rows, cols), lambda i: (i, 0, 0)),
        out_shape=jax.ShapeDtypeStruct((n, rows, cols), BF16),
        compiler_params=_params("parallel"),
        name="s5place",
    )(x, jnp.asarray(spread, dtype=BF16))


def _s5_matrices(ab, ca, kk):
    t_, g8, nt = S5_T, GROUPS_PER_TILE, N_LANE_TILES
    w_all, v_all, m_all = [], [], []
    for d in range(2):
        walked = (lambda s: s) if d == 0 else (lambda s: t_ - 1 - s)
        ab_d = jnp.stack([ab[:, t_ - 1 - walked(s), d] for s in range(t_)], axis=1)
        ab_d = ab_d.reshape(2, t_, nt, g8, S5_GROUP, S5_STATE).transpose(2, 1, 3, 4, 0, 5)
        w_all.append(ab_d.reshape(nt, t_ * LANES, 2 * S5_STATE))
        ca_d = jnp.stack([ca[:, walked(t), d] for t in range(t_)], axis=1)
        ca_d = ca_d * jnp.array([1.0, -1.0], F32).reshape(2, 1, 1, 1, 1)
        ca_d = ca_d.reshape(2, t_, nt, g8, S5_GROUP, S5_STATE).transpose(2, 0, 3, 5, 1, 4)
        v_all.append(ca_d.reshape(nt, 2 * TILE_COLS, t_ * S5_GROUP))
        zero = jnp.zeros_like(kk[0, d])
        blocks = jnp.stack([jnp.stack([kk[walked(t) - walked(s), d] if walked(t) >= walked(s) else zero
                                       for t in range(t_)], axis=0) for s in range(t_)], axis=0)
        blocks = blocks.reshape(t_, t_, nt, g8, S5_GROUP, S5_GROUP).transpose(2, 0, 3, 5, 1, 4)
        m_all.append(blocks.reshape(nt, t_ * LANES, t_ * S5_GROUP))

    def spread(parts, *args):
        out = _spread_groups(jnp.concatenate(parts, axis=0), *args)
        return out.reshape((2, nt) + out.shape[1:])

    return (spread(w_all, S5_GROUP, S5_STATE, 2), spread(v_all, S5_STATE, S5_GROUP, t_),
            spread(m_all, S5_GROUP, S5_GROUP, t_))


def _rope_tile(t, c, s):
    return t * c + pltpu.roll(t, LANES // 2, 1) * s


def _inproj_kernel(with_q, x_ref, sh_ref, sc_ref, win_ref, qg_ref, kvg_ref, wuq_ref, wuk_ref, wuv_ref,
                   c_ref, s_ref, *out_refs):
    if with_q:
        q_ref, k_ref, v_ref, u_ref = out_refs
    else:
        k_ref, v_ref, u_ref = out_refs
    x = x_ref[0]
    xm = _layer_norm(x) * (1.0 + sc_ref[0]) + sh_ref[0]
    p = _dot(xm.astype(BF16), win_ref[...])
    c, s = c_ref[...], s_ref[...]

    kv_c = _rms_norm(p[:, Q_RANK:KV_END], kvg_ref[...]).astype(BF16)
    k = _dot(kv_c, wuk_ref[...])
    kr = _rope_tile(p[:, KV_END + S5_WIDTH:], c, s)
    for h in range(N_HEADS):
        k_ref[0, :, h * HEAD_PAD:(h + 1) * HEAD_PAD] = (k[:, h * HEAD_PAD:(h + 1) * HEAD_PAD] + kr).astype(BF16)
    v_ref[0] = lax.dot_general(wuv_ref[...], kv_c, _NT, preferred_element_type=F32).astype(BF16)
    u_ref[0] = p[:, KV_END:KV_END + S5_WIDTH]

    if with_q:
        q_c = _rms_norm(p[:, :Q_RANK], qg_ref[...]).astype(BF16)
        q = _dot(q_c, wuq_ref[...])
        cq, sq = c * Q_SCALE, s * Q_SCALE
        for h in range(N_HEADS):
            sl = slice(h * HEAD_PAD, (h + 1) * HEAD_PAD)
            q_ref[0, :, sl] = _rope_tile(q[:, sl], cq, sq).astype(BF16)


def _inproj(x, mod3, mod_row, w_in_r, qg, kvg, wuq, wuk, wuv, rope_c, rope_s, with_q):
    b, l, _ = x.shape
    tm = min(TM, l)
    row = (lambda bi: bi) if mod_row is None else (lambda bi: mod_row)
    tok = lambda w: pl.BlockSpec((1, tm, w), lambda bi, i: (bi, i, 0))
    in_specs = [
        tok(D_MODEL),
        pl.BlockSpec((1, 1, D_MODEL), lambda bi, i: (row(bi), 0, 0)),
        pl.BlockSpec((1, 1, D_MODEL), lambda bi, i: (row(bi), 0, 1)),
        _const_spec(w_in_r.shape), _const_spec(qg.shape), _const_spec(kvg.shape),
        _const_spec(wuq.shape), _const_spec(wuk.shape), _const_spec(wuv.shape),
        pl.BlockSpec((tm, LANES), lambda bi, i: (i, 0)),
        pl.BlockSpec((tm, LANES), lambda bi, i: (i, 0)),
    ]
    out_specs = [tok(N_HEADS * HEAD_PAD), pl.BlockSpec((1, MLA_WIDTH, tm), lambda bi, i: (bi, 0, i)), tok(S5_WIDTH)]
    out_shape = [jax.ShapeDtypeStruct((b, l, N_HEADS * HEAD_PAD), BF16),
                 jax.ShapeDtypeStruct((b, MLA_WIDTH, l), BF16),
                 jax.ShapeDtypeStruct((b, l, S5_WIDTH), F32)]
    if with_q:
        out_specs = [tok(N_HEADS * HEAD_PAD)] + out_specs
        out_shape = [jax.ShapeDtypeStruct((b, l, N_HEADS * HEAD_PAD), BF16)] + out_shape
    return pl.pallas_call(
        functools.partial(_inproj_kernel, with_q),
        grid=(b, l // tm),
        in_specs=in_specs, out_specs=out_specs, out_shape=out_shape,
        compiler_params=_params("parallel", "parallel"),
        name="inproj_lat" if with_q else "inproj_ctx",
    )(x, mod3, mod3, w_in_r, qg, kvg, wuq, wuk, wuv, rope_c, rope_s)


def _attn_kernel(fast, q_ref, kc_ref, kl_ref, vc_ref, vl_ref, o_ref, *stat_refs):
    def scores(h):
        ks = slice(h * HEAD_PAD, (h + 1) * HEAD_PAD)
        q = q_ref[0, :, ks]
        return (lax.dot_general(kc_ref[0, :, ks], q, _NT, preferred_element_type=F32),
                lax.dot_general(kl_ref[0, :, ks], q, _NT, preferred_element_type=F32))

    def with_ones(vt):
        return jnp.concatenate([vt, jnp.ones((DEN_ROWS, vt.shape[1]), BF16)], axis=0)

    outs, den_lo, den_hi = [], None, None
    s_next = scores(0)
    for h in range(HEADS_PER_STEP):
        s_c, s_l = s_next
        if h + 1 < HEADS_PER_STEP:
            s_next = scores(h + 1)
        vs = slice(h * V_DIM, (h + 1) * V_DIM)
        m = jnp.max(s_c, axis=0, keepdims=True)
        if not fast:
            m = jnp.maximum(m, jnp.max(s_l, axis=0, keepdims=True))
        e_c = jnp.exp2(s_c - m).astype(BF16)
        e_l = jnp.exp2(s_l - m).astype(BF16)
        o_t = _dot(with_ones(vc_ref[0, vs, :]), e_c) + _dot(with_ones(vl_ref[0, vs, :]), e_l)
        den = o_t[V_DIM:V_DIM + 1]
        outs.append(o_t[:V_DIM] / den)
        den_lo = den if den_lo is None else jnp.minimum(den_lo, den)
        den_hi = den if den_hi is None else jnp.maximum(den_hi, den)
    o_ref[0] = jnp.concatenate(outs, axis=0).T.astype(o_ref.dtype)
    if fast:
        stat_refs[0][0, 0] = jnp.concatenate([den_lo, den_hi], axis=0)


def _attention(q, k_c, k_l, v_c, v_l, fast):
    b, l, _ = q.shape
    lc = k_c.shape[1]
    assert HEADS_PER_STEP == N_HEADS
    kw, vw = N_HEADS * HEAD_PAD, N_HEADS * V_DIM
    out_specs = [pl.BlockSpec((1, BQ, vw), lambda bi, i: (bi, i, 0))]
    out_shape = [jax.ShapeDtypeStruct((b, l, MLA_WIDTH), BF16)]
    if fast:
        out_specs.append(pl.BlockSpec((1, 1, 2, BQ), lambda bi, i: (bi, i, 0, 0)))
        out_shape.append(jax.ShapeDtypeStruct((b, l // BQ, 2, BQ), F32))
    return pl.pallas_call(
        functools.partial(_attn_kernel, fast),
        grid=(b, l // BQ),
        in_specs=[pl.BlockSpec((1, BQ, kw), lambda bi, i: (bi, i, 0)),
                  pl.BlockSpec((1, lc, kw), lambda bi, i: (bi, 0, 0)),
                  pl.BlockSpec((1, l, kw), lambda bi, i: (bi, 0, 0)),
                  pl.BlockSpec((1, vw, lc), lambda bi, i: (bi, 0, 0)),
                  pl.BlockSpec((1, vw, l), lambda bi, i: (bi, 0, 0))],
        out_specs=out_specs, out_shape=out_shape,
        compiler_params=_params("parallel", "arbitrary"),
        name="attn_fast" if fast else "attn",
    )(q, k_c, k_l, v_c, v_l)


def _verified_attention(q, k_c, k_l, v_c, v_l):
    att, den_range = _attention(q, k_c, k_l, v_c, v_l, fast=True)
    in_range = jnp.logical_and(jnp.all(den_range[:, :, 0] >= DEN_MIN), jnp.all(den_range[:, :, 1] <= DEN_MAX))
    return lax.cond(in_range, lambda: att, lambda: _attention(q, k_c, k_l, v_c, v_l, fast=False)[0])


def _s5scan_kernel(n_ctx_chunks, uc_ref, ul_ref, w_ref, a_ref, v_ref, m_ref, dskip_ref, y_ref, hs_ref, st_ref):
    d = pl.program_id(0)
    i = pl.program_id(1)

    @pl.when(i == 0)
    def _():
        st_ref[...] = jnp.zeros_like(st_ref)

    u_bt = jnp.where(i < n_ctx_chunks, uc_ref[...], ul_ref[...])
    u = jnp.swapaxes(u_bt, 0, 1).reshape(N_ROWS // SUBLANES, S5_T, SUBLANES, S5_WIDTH)
    u_pos = [u[:, s].reshape(N_ROWS, S5_WIDTH) for s in range(S5_T)]

    def group_lanes(lt):
        lanes = slice(lt * LANES, (lt + 1) * LANES)
        return jnp.concatenate([p[:, lanes] for p in u_pos], axis=1).astype(BF16)

    def tile_cols(lt):
        base = (lt // TILES_PER_SLAB) * 2 * SLAB + (lt % TILES_PER_SLAB) * TILE_COLS
        return slice(base, base + TILE_COLS), slice(base + SLAB, base + SLAB + TILE_COLS)

    for lt in range(N_LANE_TILES):
        re, im = tile_cols(lt)
        inc = _dot(group_lanes(lt), w_ref[0, lt])
        hs_ref[:, re] = inc[:, :TILE_COLS]
        hs_ref[:, im] = inc[:, TILE_COLS:]

    n_steps = N_ROWS // SUBLANES
    for cs in range(0, 2 * S5_COLS, 2 * SLAB):
        re = slice(cs, cs + SLAB)
        im = slice(cs + SLAB, cs + 2 * SLAB)
        a_re, a_im = a_ref[0, :, re], a_ref[0, :, im]

        def step(tt, carry, re=re, im=im, a_re=a_re, a_im=a_im):
            h_re, h_im = carry
            t = jnp.where(d == 0, tt, n_steps - 1 - tt)
            rows = pl.ds(pl.multiple_of(t * SUBLANES, SUBLANES), SUBLANES)
            n_re = a_re * h_re - a_im * h_im + hs_ref[rows, re]
            n_im = a_re * h_im + a_im * h_re + hs_ref[rows, im]
            hs_ref[rows, re] = h_re
            hs_ref[rows, im] = h_im
            return n_re, n_im

        h_re, h_im = lax.fori_loop(0, n_steps, step, (st_ref[:, re], st_ref[:, im]), unroll=4)
        st_ref[:, re] = h_re
        st_ref[:, im] = h_im

    for lt in range(N_LANE_TILES):
        re, im = tile_cols(lt)
        lanes = slice(lt * LANES, (lt + 1) * LANES)
        y = (_dot(hs_ref[:, re].astype(BF16), v_ref[0, lt, :TILE_COLS])
             + _dot(hs_ref[:, im].astype(BF16), v_ref[0, lt, TILE_COLS:])
             + _dot(group_lanes(lt), m_ref[0, lt]))
        skip = dskip_ref[0, :, lanes]
        y_pos = [(y[:, s * LANES:(s + 1) * LANES] + u_pos[s][:, lanes] * skip)
                 .reshape(N_ROWS // SUBLANES, 1, SUBLANES, LANES) for s in range(S5_T)]
        y_tb = jnp.concatenate(y_pos, axis=1).reshape(TC, SUBLANES, LANES)
        y_ref[0, :, :, lanes] = jnp.swapaxes(y_tb, 0, 1)


def _s5scan(u_c, u_l, w_mat, a_b, v_mat, m_mat, dskip):
    b, l, _ = u_l.shape
    nc, nl = u_c.shape[1] // TC, l // TC

    def ctx_blk(d, i):
        return jnp.clip(jnp.where(d == 0, i, nc - 1 - i), 0, nc - 1)

    def lat_blk(d, i):
        return jnp.clip(jnp.where(d == 0, i - nc, nl - 1 - (i - nc)), 0, nl - 1)

    def out_blk(d, i):
        return jnp.where(i < nc, nl + ctx_blk(d, i), lat_blk(d, i))

    return pl.pallas_call(
        functools.partial(_s5scan_kernel, nc),
        grid=(2, nc + nl),
        in_specs=[pl.BlockSpec((b, TC, S5_WIDTH), lambda d, i: (0, ctx_blk(d, i), 0)),
                  pl.BlockSpec((b, TC, S5_WIDTH), lambda d, i: (0, lat_blk(d, i), 0)),
                  pl.BlockSpec((1, N_LANE_TILES, S5_T * LANES, 2 * TILE_COLS), lambda d, i: (d, 0, 0, 0)),
                  pl.BlockSpec((1, SUBLANES, 2 * S5_COLS), lambda d, i: (d, 0, 0)),
                  pl.BlockSpec((1, N_LANE_TILES, 2 * TILE_COLS, S5_T * LANES), lambda d, i: (d, 0, 0, 0)),
                  pl.BlockSpec((1, N_LANE_TILES, S5_T * LANES, S5_T * LANES), lambda d, i: (d, 0, 0, 0)),
                  pl.BlockSpec((1, 1, S5_WIDTH), lambda d, i: (d, 0, 0))],
        out_specs=pl.BlockSpec((1, b, TC, S5_WIDTH), lambda d, i: (d, 0, out_blk(d, i), 0)),
        out_shape=jax.ShapeDtypeStruct((2, b, (nc + nl) * TC, S5_WIDTH), F32),
        scratch_shapes=[pltpu.VMEM((N_ROWS, 2 * S5_COLS), F32),
                        pltpu.VMEM((SUBLANES, 2 * S5_COLS), F32)],
        compiler_params=_params("arbitrary", "arbitrary"),
        name="s5scan",
    )(u_c, u_l, w_mat, a_b, v_mat, m_mat, dskip)


def _tail_kernel(x_ref, att_ref, yf_ref, yb_ref, g1_ref, sh2_ref, sc2_ref, g2_ref, wglu_ref, bglu_ref, wout_ref,
                 ln1g_ref, ln1b_ref, wgu_ref, wd_ref, ln2g_ref, ln2b_ref, o_ref, acc_ref):
    z = jax.nn.gelu(yf_ref[0, 0] + yb_ref[0, 0])
    s5o = z * jax.nn.sigmoid(_dot(z.astype(BF16), wglu_ref[...]) + bglu_ref[...])
    mix = _dot(att_ref[0], wout_ref[:MLA_WIDTH, :]) + _dot(s5o.astype(BF16), wout_ref[MLA_WIDTH:, :])
    x1 = _layer_norm(DN_ALPHA * x_ref[0] + g1_ref[0] * mix) * ln1g_ref[...] + ln1b_ref[...]

    xm = (_layer_norm(x1) * (1.0 + sc2_ref[0]) + sh2_ref[0]).astype(BF16)
    for j in range(D_FF // FF_CHUNK):
        gate = _dot(xm, wgu_ref[:, j * FF_CHUNK:(j + 1) * FF_CHUNK])
        up = _dot(xm, wgu_ref[:, D_FF + j * FF_CHUNK:D_FF + (j + 1) * FF_CHUNK])
        hidden = (gate * jax.nn.sigmoid(gate) * up).astype(BF16)
        part = _dot(hidden, wd_ref[j * FF_CHUNK:(j + 1) * FF_CHUNK, :])
        if j == 0:
            acc_ref[...] = part
        else:
            acc_ref[...] += part
    o_ref[0] = _layer_norm(DN_ALPHA * x1 + g2_ref[0] * acc_ref[...]) * ln2g_ref[...] + ln2b_ref[...]


def _tail(x, att, y_dir, mod3, w_glu, b_glu, w_out, ln1_g, ln1_b, w_gu, w_down, ln2_g, ln2_b):
    b, l, _ = x.shape
    tok = lambda w: pl.BlockSpec((1, TM, w), lambda bi, i: (bi, i, 0))
    mod = lambda j: pl.BlockSpec((1, 1, D_MODEL), lambda bi, i: (bi, 0, j))
    y_of = lambda d: pl.BlockSpec((1, 1, TM, S5_WIDTH), lambda bi, i: (d, bi, i, 0))
    consts = (w_glu, b_glu, w_out, ln1_g, ln1_b, w_gu, w_down, ln2_g, ln2_b)
    return pl.pallas_call(
        _tail_kernel,
        grid=(b, l // TM),
        in_specs=[tok(D_MODEL), tok(MLA_WIDTH), y_of(0), y_of(1), mod(2), mod(3), mod(4), mod(5)]
                 + [_const_spec(w.shape) for w in consts],
        out_specs=tok(D_MODEL),
        out_shape=jax.ShapeDtypeStruct((b, l, D_MODEL), F32),
        scratch_shapes=[pltpu.VMEM((TM, D_MODEL), F32)],
        compiler_params=_params("parallel", "parallel"),
        name="tail",
    )(x, att, y_dir, y_dir, mod3, mod3, mod3, mod3, *consts)


def _head_lanes(nope, x0, x1, xp=jnp):
    pad = xp.zeros(nope.shape[:-1] + (LANES // 2 - (NOPE - NOPE_LO) - ROPE_HALF,), nope.dtype)
    return xp.concatenate([nope[..., :NOPE_LO], x0, nope[..., NOPE_LO:], pad, x1], axis=-1)


def _rope_tables(seq):
    pos = np.arange(seq)
    row = (pos // GRID_W).astype(np.float32)
    col = (pos % GRID_W).astype(np.float32)
    n_freq = ROPE // 4
    freqs = np.float32(ROPE_THETA) ** (-np.arange(n_freq, dtype=np.float32) / np.float32(n_freq))
    ang = np.concatenate([row[:, None] * freqs, col[:, None] * freqs], axis=-1).astype(np.float32)
    cos, sin = np.cos(ang), np.sin(ang)
    c = _head_lanes(np.ones((seq, NOPE), np.float32), cos, cos, xp=np)
    s = _head_lanes(np.zeros((seq, NOPE), np.float32), -sin, sin, xp=np)
    return jnp.asarray(c), jnp.asarray(s)


def kernel(x, c, ctx, c_ctx, w_ada, b_ada, w_in, q_norm_g, kv_norm_g, w_uq, w_uk, w_uv, s5_lambda_re, s5_lambda_im, s5_log_dt, s5_b_re, s5_b_im, s5_c_re, s5_c_im, s5_d, s5_w_glu, s5_b_glu, w_out, ln1_g, ln1_b, w_gate_up, w_down, ln2_g, ln2_b):
    assert w_ada.shape[0] == DEPTH == 1
    b, l, _ = x.shape
    assert b == SUBLANES, "the S5 scan maps the batch onto the sublanes of one vreg row"
    lc = ctx.shape[1]
    row2 = lambda t: t.reshape(1, -1)

    cond = jnp.concatenate([c, c_ctx[None, :], jnp.zeros((2 * SUBLANES - b - 1, D_MODEL), F32)], axis=0)
    mod = _adaln(cond, w_ada[0], row2(b_ada[0]))
    mod3 = mod.reshape(mod.shape[0], 1, 6 * D_MODEL)

    wi = w_in[0]
    first, second = (lambda w: w[..., 0::2]), (lambda w: w[..., 1::2])
    w_kr = wi[:, KV_END:ROPE_END]
    rope_tile = _head_lanes(jnp.zeros((D_MODEL, NOPE), F32), first(w_kr), second(w_kr))
    w_in_r = jnp.concatenate([wi[:, :KV_END], wi[:, ROPE_END:], rope_tile], axis=1).astype(BF16)
    uq = w_uq[0]
    wuq = _head_lanes(uq[..., :NOPE], first(uq[..., NOPE:]), second(uq[..., NOPE:]))
    wuq = wuq.reshape(Q_RANK, N_HEADS * HEAD_PAD).astype(BF16)
    no_rope = jnp.zeros((KV_RANK, N_HEADS, ROPE_HALF), F32)
    wuk = _head_lanes(w_uk[0], no_rope, no_rope).reshape(KV_RANK, N_HEADS * HEAD_PAD).astype(BF16)
    wuv = w_uv[0].reshape(KV_RANK, MLA_WIDTH).T.astype(BF16)
    qg, kvg = row2(q_norm_g[0]), row2(kv_norm_g[0])

    rope_c, rope_s = _rope_tables(l)
    ones_half = np.ones((lc, ROPE_HALF), np.float32)
    flat_c = jnp.asarray(_head_lanes(np.ones((lc, NOPE), np.float32), ones_half, ones_half, xp=np))
    flat_s = jnp.zeros((lc, LANES), F32)

    q, k_l, v_l, u_l = _inproj(x, mod3, None, w_in_r, qg, kvg, wuq, wuk, wuv, rope_c, rope_s, True)
    k_c, v_c, u_c = _inproj(ctx, mod3, b, w_in_r, qg, kvg, wuq, wuk, wuv, flat_c, flat_s, False)

    att = _verified_attention(q, k_c, k_l, v_c, v_l)

    a_t, ab, ca, kk = _s5prep(s5_lambda_re[0], s5_lambda_im[0], s5_log_dt[0], s5_b_re[0], s5_b_im[0],
                              s5_c_re[0], s5_c_im[0])
    w_mat, v_mat, m_mat = _s5_matrices(ab, ca, kk)
    slabs = lambda t: t.reshape(2, S5_COLS // SLAB, 1, SLAB)
    a_cat = jnp.concatenate([slabs(a_t[0]), slabs(a_t[1])], axis=2).reshape(2, 2 * S5_COLS)
    a_b = jnp.broadcast_to(a_cat[:, None, :], (2, SUBLANES, 2 * S5_COLS))
    dskip = jnp.stack([s5_d[0], jnp.zeros_like(s5_d[0])])[:, None, :]
    y_dir = _s5scan(u_c, u_l, w_mat, a_b, v_mat, m_mat, dskip)

    return _tail(x, att, y_dir, mod3, s5_w_glu[0].astype(BF16), row2(s5_b_glu[0]), w_out[0].astype(BF16),
                 row2(ln1_g[0]), row2(ln1_b[0]), w_gate_up[0].astype(BF16), w_down[0].astype(BF16),
                 row2(ln2_g[0]), row2(ln2_b[0]))
```

```python
import functools
import math

import jax
import jax.numpy as jnp
import numpy as np
from jax import lax
from jax.experimental import pallas as pl
from jax.experimental.pallas import tpu as pltpu

D_MODEL = 1024
GRID_W = 64
N_HEADS = 8
NOPE = 64
ROPE = 32
V_DIM = 64
Q_RANK = 384
KV_RANK = 256
MLA_WIDTH = N_HEADS * V_DIM
S5_WIDTH = D_MODEL - MLA_WIDTH
S5_GROUP = 16
S5_GROUPS = S5_WIDTH // S5_GROUP
S5_STATE = 64
S5_COLS = S5_GROUPS * S5_STATE
KV_END = Q_RANK + KV_RANK
ROPE_END = KV_END + ROPE
D_FF = 2816
ROPE_THETA = 10000.0
NORM_EPS = 1e-6
DEPTH = 1
DN_ALPHA = (2.0 * DEPTH) ** 0.25

LANES = 128
SUBLANES = 8
HEAD_PAD = LANES
ROPE_HALF = ROPE // 2
NOPE_LO = LANES // 2 - ROPE_HALF
P_COLS = Q_RANK + KV_RANK + S5_WIDTH + LANES
VMEM_LIMIT = 56 * 1024 * 1024

TM = 512
TM_IN = 1024
BQ = 512
DEN_ROWS = 16
HEADS_PER_STEP = 8
DEN_MIN, DEN_MAX = 2.0 ** -90, 2.0 ** 100
Q_SCALE = (NOPE + ROPE) ** -0.5 * math.log2(math.e)
TC = 128
S5_T = 4
N_ROWS = TC // S5_T * SUBLANES
N_LANE_TILES = S5_WIDTH // LANES
GROUPS_PER_TILE = LANES // S5_GROUP
TILE_COLS = GROUPS_PER_TILE * S5_STATE
TILES_PER_SLAB = 2
SLAB = TILES_PER_SLAB * TILE_COLS
FF_CHUNK = 256

F32 = jnp.float32
BF16 = jnp.bfloat16


def _params(*sem):
    return pltpu.CompilerParams(dimension_semantics=sem, vmem_limit_bytes=VMEM_LIMIT)


def _const_spec(shape):
    nd = len(shape)
    return pl.BlockSpec(shape, lambda *_: (0,) * nd, pipeline_mode=pl.Buffered(1))


def _layer_norm(x):
    mu = jnp.mean(x, axis=-1, keepdims=True)
    xc = x - mu
    var = jnp.mean(xc * xc, axis=-1, keepdims=True)
    return xc * lax.rsqrt(var + NORM_EPS)


def _rms_norm(x, g):
    return x * lax.rsqrt(jnp.mean(x * x, axis=-1, keepdims=True) + NORM_EPS) * g


def _dot(a, b):
    return jnp.dot(a, b, preferred_element_type=F32)


_NT = (((1,), (1,)), ((), ()))


def _adaln_kernel(cond_ref, w_ref, b_ref, o_ref):
    cnd = cond_ref[...]
    act = cnd * jax.nn.sigmoid(cnd)
    o_ref[...] = _dot(act.astype(BF16), w_ref[...].astype(BF16)) + b_ref[...]


def _adaln(cond, w_ada, b_ada):
    rows, n = cond.shape[0], w_ada.shape[1]
    tn = D_MODEL
    return pl.pallas_call(
        _adaln_kernel,
        grid=(n // tn,),
        in_specs=[pl.BlockSpec((rows, D_MODEL), lambda j: (0, 0)),
                  pl.BlockSpec((D_MODEL, tn), lambda j: (0, j)),
                  pl.BlockSpec((1, tn), lambda j: (0, j))],
        out_specs=pl.BlockSpec((rows, tn), lambda j: (0, j)),
        out_shape=jax.ShapeDtypeStruct((rows, n), F32),
        compiler_params=_params("arbitrary"),
        name="adaln",
    )(cond, w_ada, b_ada)


def _s5prep_kernel(lre_ref, lim_ref, ldt_ref, bre_ref, bim_ref, cre_ref, cim_ref, at_ref, ab_ref, ca_ref, kk_ref):
    lre, lim = lre_ref[...], lim_ref[...]
    dt = jnp.exp(ldt_ref[...])
    mag = jnp.exp(lre * dt)
    a_re, a_im = mag * jnp.cos(lim * dt), mag * jnp.sin(lim * dt)
    den = lre * lre + lim * lim
    f_re = ((a_re - 1) * lre + a_im * lim) / den
    f_im = (a_im * lre - (a_re - 1) * lim) / den
    b_re, b_im = bre_ref[...], bim_ref[...]
    c_re, c_im = cre_ref[...], cim_ref[...]

    def times_a(z_re, z_im):
        return a_re * z_re - a_im * z_im, a_re * z_im + a_im * z_re

    def over_states(x, y):
        return lax.dot_general(x, y, (((2,), (2,)), ((0,), (0,))), precision=lax.Precision.HIGHEST,
                               preferred_element_type=F32)

    p_re, p_im = f_re * b_re - f_im * b_im, f_re * b_im + f_im * b_re
    q_re, q_im = c_re, c_im
    w_re, w_im = jnp.ones_like(a_re), jnp.zeros_like(a_re)
    for k in range(S5_T):
        ab_ref[0, k] = p_re
        ab_ref[1, k] = p_im
        kk_ref[k] = over_states(c_re, p_re) - over_states(c_im, p_im)
        p_re, p_im = times_a(p_re, p_im)
        q_re, q_im = times_a(q_re, q_im)
        w_re, w_im = times_a(w_re, w_im)
        ca_ref[0, k] = q_re
        ca_ref[1, k] = q_im
    at_ref[0] = w_re
    at_ref[1] = w_im


def _s5prep(lam_re, lam_im, log_dt, b_re, b_im, c_re, c_im):
    n = 2 * S5_GROUPS
    full = (n, S5_GROUP, S5_STATE)
    per_state = lambda t: jnp.broadcast_to(t.reshape(n, 1, S5_STATE), full)
    ldt = jnp.broadcast_to(log_dt.reshape(n, 1, 1), full)
    chan_state = lambda t: t.reshape(full)
    shp = lambda *lead: jax.ShapeDtypeStruct(lead + full, F32)
    at, ab, ca, kk = pl.pallas_call(
        _s5prep_kernel,
        out_shape=(shp(2), shp(2, S5_T), shp(2, S5_T), jax.ShapeDtypeStruct((S5_T, n, S5_GROUP, S5_GROUP), F32)),
        compiler_params=pltpu.CompilerParams(vmem_limit_bytes=VMEM_LIMIT),
        name="s5prep",
    )(per_state(lam_re), per_state(lam_im), ldt, chan_state(jnp.swapaxes(b_re, -1, -2)),
      chan_state(jnp.swapaxes(b_im, -1, -2)), chan_state(c_re), chan_state(c_im))
    by_dir = lambda t, lead: t.reshape(lead + (2, S5_GROUPS) + t.shape[-2:])
    return by_dir(at, (2,))[:, :, :, 0], by_dir(ab, (2, S5_T)), by_dir(ca, (2, S5_T)), by_dir(kk, (S5_T,))


def _spread_groups(x, rows_per_group, col_width, n_col_blocks):
    g8 = GROUPS_PER_TILE
    n, rows, k = x.shape
    spread = np.kron(np.eye(n_col_blocks), np.kron(np.ones((1, g8)), np.eye(col_width))).astype(np.float32)
    cols = spread.shape[1]

    def body(x_ref, spread_ref, o_ref):
        tiled = _dot(x_ref[0].astype(BF16), spread_ref[...])
        row_group = (lax.broadcasted_iota(jnp.int32, (rows, cols), 0) // rows_per_group) % g8
        col_group = (lax.broadcasted_iota(jnp.int32, (rows, cols), 1) // col_width) % g8
        o_ref[0] = jnp.where(row_group == col_group, tiled, 0.0).astype(BF16)

    return pl.pallas_call(
        body,
        grid=(n,),
        in_specs=[pl.BlockSpec((1, rows, k), lambda i: (i, 0, 0)), pl.BlockSpec((k, cols), lambda i: (0, 0))],
        out_specs=pl.BlockSpec((1, rows, cols), lambda i: (i, 0, 0)),
        out_shape=jax.ShapeDtypeStruct((n, rows, cols), BF16),
        compiler_params=_params("parallel"),
        name="s5place",
    )(x, jnp.asarray(spread, dtype=BF16))


def _s5_matrices(ab, ca, kk):
    t_, g8, nt = S5_T, GROUPS_PER_TILE, N_LANE_TILES
    w_all, v_all, m_all = [], [], []
    for d in range(2):
        walked = (lambda s: s) if d == 0 else (lambda s: t_ - 1 - s)
        ab_d = jnp.stack([ab[:, t_ - 1 - walked(s), d] for s in range(t_)], axis=1)
        ab_d = ab_d.reshape(2, t_, nt, g8, S5_GROUP, S5_STATE).transpose(2, 1, 3, 4, 0, 5)
        w_all.append(ab_d.reshape(nt, t_ * LANES, 2 * S5_STATE))
        ca_d = jnp.stack([ca[:, walked(t), d] for t in range(t_)], axis=1)
        ca_d = ca_d * jnp.array([1.0, -1.0], F32).reshape(2, 1, 1, 1, 1)
        ca_d = ca_d.reshape(2, t_, nt, g8, S5_GROUP, S5_STATE).transpose(2, 0, 3, 5, 1, 4)
        v_all.append(ca_d.reshape(nt, 2 * TILE_COLS, t_ * S5_GROUP))
        zero = jnp.zeros_like(kk[0, d])
        blocks = jnp.stack([jnp.stack([kk[walked(t) - walked(s), d] if walked(t) >= walked(s) else zero
                                       for t in range(t_)], axis=0) for s in range(t_)], axis=0)
        blocks = blocks.reshape(t_, t_, nt, g8, S5_GROUP, S5_GROUP).transpose(2, 0, 3, 5, 1, 4)
        m_all.append(blocks.reshape(nt, t_ * LANES, t_ * S5_GROUP))

    def spread(parts, *args):
        out = _spread_groups(jnp.concatenate(parts, axis=0), *args)
        return out.reshape((2, nt) + out.shape[1:])

    return (spread(w_all, S5_GROUP, S5_STATE, 2), spread(v_all, S5_STATE, S5_GROUP, t_),
            spread(m_all, S5_GROUP, S5_GROUP, t_))


def _rope_tile(t, c, s):
    return t * c + pltpu.roll(t, LANES // 2, 1) * s


def _inproj_kernel(with_q, x_ref, sh_ref, sc_ref, win_ref, qg_ref, kvg_ref, wuq_ref, wuk_ref, wuv_ref,
                   c_ref, s_ref, *out_refs):
    if with_q:
        q_ref, k_ref, v_ref, u_ref = out_refs
    else:
        k_ref, v_ref, u_ref = out_refs
    x = x_ref[0]
    xm = _layer_norm(x) * (1.0 + sc_ref[0]) + sh_ref[0]
    p = _dot(xm.astype(BF16), win_ref[...])
    c, s = c_ref[...], s_ref[...]

    kv_c = _rms_norm(p[:, Q_RANK:KV_END], kvg_ref[...]).astype(BF16)
    k = _dot(kv_c, wuk_ref[...])
    kr = _rope_tile(p[:, KV_END + S5_WIDTH:], c, s)
    for h in range(N_HEADS):
        k_ref[0, :, h * HEAD_PAD:(h + 1) * HEAD_PAD] = (k[:, h * HEAD_PAD:(h + 1) * HEAD_PAD] + kr).astype(BF16)
    v_ref[0] = lax.dot_general(wuv_ref[...], kv_c, _NT, preferred_element_type=F32).astype(BF16)
    u_ref[0] = p[:, KV_END:KV_END + S5_WIDTH]

    if with_q:
        q_c = _rms_norm(p[:, :Q_RANK], qg_ref[...]).astype(BF16)
        q = _dot(q_c, wuq_ref[...])
        cq, sq = c * Q_SCALE, s * Q_SCALE
        for h in range(N_HEADS):
            sl = slice(h * HEAD_PAD, (h + 1) * HEAD_PAD)
            q_ref[0, :, sl] = _rope_tile(q[:, sl], cq, sq).astype(BF16)


def _inproj(x, mod3, mod_row, w_in_r, qg, kvg, wuq, wuk, wuv, rope_c, rope_s, with_q):
    b, l, _ = x.shape
    tm = min(TM_IN, l)
    row = (lambda bi: bi) if mod_row is None else (lambda bi: mod_row)
    tok = lambda w: pl.BlockSpec((1, tm, w), lambda bi, i: (bi, i, 0))
    in_specs = [
        tok(D_MODEL),
        pl.BlockSpec((1, 1, D_MODEL), lambda bi, i: (row(bi), 0, 0)),
        pl.BlockSpec((1, 1, D_MODEL), lambda bi, i: (row(bi), 0, 1)),
        _const_spec(w_in_r.shape), _const_spec(qg.shape), _const_spec(kvg.shape),
        _const_spec(wuq.shape), _const_spec(wuk.shape), _const_spec(wuv.shape),
        pl.BlockSpec((tm, LANES), lambda bi, i: (i, 0)),
        pl.BlockSpec((tm, LANES), lambda bi, i: (i, 0)),
    ]
    out_specs = [tok(N_HEADS * HEAD_PAD), pl.BlockSpec((1, MLA_WIDTH, tm), lambda bi, i: (bi, 0, i)), tok(S5_WIDTH)]
    out_shape = [jax.ShapeDtypeStruct((b, l, N_HEADS * HEAD_PAD), BF16),
                 jax.ShapeDtypeStruct((b, MLA_WIDTH, l), BF16),
                 jax.ShapeDtypeStruct((b, l, S5_WIDTH), F32)]
    if with_q:
        out_specs = [tok(N_HEADS * HEAD_PAD)] + out_specs
        out_shape = [jax.ShapeDtypeStruct((b, l, N_HEADS * HEAD_PAD), BF16)] + out_shape
    return pl.pallas_call(
        functools.partial(_inproj_kernel, with_q),
        grid=(b, l // tm),
        in_specs=in_specs, out_specs=out_specs, out_shape=out_shape,
        compiler_params=_params("parallel", "parallel"),
        name="inproj_lat" if with_q else "inproj_ctx",
    )(x, mod3, mod3, w_in_r, qg, kvg, wuq, wuk, wuv, rope_c, rope_s)


def _attn_kernel(fast, q_ref, kc_ref, kl_ref, vc_ref, vl_ref, o_ref, *stat_refs):
    def scores(h):
        ks = slice(h * HEAD_PAD, (h + 1) * HEAD_PAD)
        q = q_ref[0, :, ks]
        return (lax.dot_general(kc_ref[0, :, ks], q, _NT, preferred_element_type=F32),
                lax.dot_general(kl_ref[0, :, ks], q, _NT, preferred_element_type=F32))

    def with_ones(vt):
        return jnp.concatenate([vt, jnp.ones((DEN_ROWS, vt.shape[1]), BF16)], axis=0)

    outs, den_lo, den_hi = [], None, None
    s_next = scores(0)
    for h in range(HEADS_PER_STEP):
        s_c, s_l = s_next
        if h + 1 < HEADS_PER_STEP:
            s_next = scores(h + 1)
        vs = slice(h * V_DIM, (h + 1) * V_DIM)
        m = jnp.max(s_c, axis=0, keepdims=True)
        if not fast:
            m = jnp.maximum(m, jnp.max(s_l, axis=0, keepdims=True))
        e_c = jnp.exp2(s_c - m).astype(BF16)
        e_l = jnp.exp2(s_l - m).astype(BF16)
        o_t = _dot(with_ones(vc_ref[0, vs, :]), e_c) + _dot(with_ones(vl_ref[0, vs, :]), e_l)
        den = o_t[V_DIM:V_DIM + 1]
        outs.append(o_t[:V_DIM] / den)
        den_lo = den if den_lo is None else jnp.minimum(den_lo, den)
        den_hi = den if den_hi is None else jnp.maximum(den_hi, den)
    o_ref[0] = jnp.concatenate(outs, axis=0).T.astype(o_ref.dtype)
    if fast:
        stat_refs[0][0, 0] = jnp.concatenate([den_lo, den_hi], axis=0)


def _attention(q, k_c, k_l, v_c, v_l, fast):
    b, l, _ = q.shape
    lc = k_c.shape[1]
    assert HEADS_PER_STEP == N_HEADS
    kw, vw = N_HEADS * HEAD_PAD, N_HEADS * V_DIM
    out_specs = [pl.BlockSpec((1, BQ, vw), lambda bi, i: (bi, i, 0))]
    out_shape = [jax.ShapeDtypeStruct((b, l, MLA_WIDTH), BF16)]
    if fast:
        out_specs.append(pl.BlockSpec((1, 1, 2, BQ), lambda bi, i: (bi, i, 0, 0)))
        out_shape.append(jax.ShapeDtypeStruct((b, l // BQ, 2, BQ), F32))
    return pl.pallas_call(
        functools.partial(_attn_kernel, fast),
        grid=(b, l // BQ),
        in_specs=[pl.BlockSpec((1, BQ, kw), lambda bi, i: (bi, i, 0)),
                  pl.BlockSpec((1, lc, kw), lambda bi, i: (bi, 0, 0)),
                  pl.BlockSpec((1, l, kw), lambda bi, i: (bi, 0, 0)),
                  pl.BlockSpec((1, vw, lc), lambda bi, i: (bi, 0, 0)),
                  pl.BlockSpec((1, vw, l), lambda bi, i: (bi, 0, 0))],
        out_specs=out_specs, out_shape=out_shape,
        compiler_params=_params("parallel", "arbitrary"),
        name="attn_fast" if fast else "attn",
    )(q, k_c, k_l, v_c, v_l)


def _verified_attention(q, k_c, k_l, v_c, v_l):
    att, den_range = _attention(q, k_c, k_l, v_c, v_l, fast=True)
    in_range = jnp.logical_and(jnp.all(den_range[:, :, 0] >= DEN_MIN), jnp.all(den_range[:, :, 1] <= DEN_MAX))
    return lax.cond(in_range, lambda: att, lambda: _attention(q, k_c, k_l, v_c, v_l, fast=False)[0])


def _s5scan_kernel(n_ctx_chunks, uc_ref, ul_ref, w_ref, a_ref, v_ref, m_ref, dskip_ref, y_ref, hs_ref, st_ref):
    d = pl.program_id(0)
    i = pl.program_id(1)

    @pl.when(i == 0)
    def _():
        st_ref[...] = jnp.zeros_like(st_ref)

    u_bt = jnp.where(i < n_ctx_chunks, uc_ref[...], ul_ref[...])
    u = jnp.swapaxes(u_bt, 0, 1).reshape(N_ROWS // SUBLANES, S5_T, SUBLANES, S5_WIDTH)
    u_pos = [u[:, s].reshape(N_ROWS, S5_WIDTH) for s in range(S5_T)]

    def group_lanes(lt):
        lanes = slice(lt * LANES, (lt + 1) * LANES)
        return jnp.concatenate([p[:, lanes] for p in u_pos], axis=1).astype(BF16)

    def tile_cols(lt):
        base = (lt // TILES_PER_SLAB) * 2 * SLAB + (lt % TILES_PER_SLAB) * TILE_COLS
        return slice(base, base + TILE_COLS), slice(base + SLAB, base + SLAB + TILE_COLS)

    for lt in range(N_LANE_TILES):
        re, im = tile_cols(lt)
        inc = _dot(group_lanes(lt), w_ref[0, lt])
        hs_ref[:, re] = inc[:, :TILE_COLS]
        hs_ref[:, im] = inc[:, TILE_COLS:]

    n_steps = N_ROWS // SUBLANES
    for cs in range(0, 2 * S5_COLS, 2 * SLAB):
        re = slice(cs, cs + SLAB)
        im = slice(cs + SLAB, cs + 2 * SLAB)
        a_re, a_im = a_ref[0, :, re], a_ref[0, :, im]

        def step(tt, carry, re=re, im=im, a_re=a_re, a_im=a_im):
            h_re, h_im = carry
            t = jnp.where(d == 0, tt, n_steps - 1 - tt)
            rows = pl.ds(pl.multiple_of(t * SUBLANES, SUBLANES), SUBLANES)
            n_re = a_re * h_re - a_im * h_im + hs_ref[rows, re]
            n_im = a_re * h_im + a_im * h_re + hs_ref[rows, im]
            hs_ref[rows, re] = h_re
            hs_ref[rows, im] = h_im
            return n_re, n_im

        h_re, h_im = lax.fori_loop(0, n_steps, step, (st_ref[:, re], st_ref[:, im]), unroll=4)
        st_ref[:, re] = h_re
        st_ref[:, im] = h_im

    for lt in range(N_LANE_TILES):
        re, im = tile_cols(lt)
        lanes = slice(lt * LANES, (lt + 1) * LANES)
        y = (_dot(hs_ref[:, re].astype(BF16), v_ref[0, lt, :TILE_COLS])
             + _dot(hs_ref[:, im].astype(BF16), v_ref[0, lt, TILE_COLS:])
             + _dot(group_lanes(lt), m_ref[0, lt]))
        skip = dskip_ref[0, :, lanes]
        y_pos = [(y[:, s * LANES:(s + 1) * LANES] + u_pos[s][:, lanes] * skip)
                 .reshape(N_ROWS // SUBLANES, 1, SUBLANES, LANES) for s in range(S5_T)]
        y_tb = jnp.concatenate(y_pos, axis=1).reshape(TC, SUBLANES, LANES)
        y_ref[0, :, :, lanes] = jnp.swapaxes(y_tb, 0, 1)


def _s5scan(u_c, u_l, w_mat, a_b, v_mat, m_mat, dskip):
    b, l, _ = u_l.shape
    nc, nl = u_c.shape[1] // TC, l // TC

    def ctx_blk(d, i):
        return jnp.clip(jnp.where(d == 0, i, nc - 1 - i), 0, nc - 1)

    def lat_blk(d, i):
        return jnp.clip(jnp.where(d == 0, i - nc, nl - 1 - (i - nc)), 0, nl - 1)

    def out_blk(d, i):
        return jnp.where(i < nc, nl + ctx_blk(d, i), lat_blk(d, i))

    return pl.pallas_call(
        functools.partial(_s5scan_kernel, nc),
        grid=(2, nc + nl),
        in_specs=[pl.BlockSpec((b, TC, S5_WIDTH), lambda d, i: (0, ctx_blk(d, i), 0)),
                  pl.BlockSpec((b, TC, S5_WIDTH), lambda d, i: (0, lat_blk(d, i), 0)),
                  pl.BlockSpec((1, N_LANE_TILES, S5_T * LANES, 2 * TILE_COLS), lambda d, i: (d, 0, 0, 0)),
                  pl.BlockSpec((1, SUBLANES, 2 * S5_COLS), lambda d, i: (d, 0, 0)),
                  pl.BlockSpec((1, N_LANE_TILES, 2 * TILE_COLS, S5_T * LANES), lambda d, i: (d, 0, 0, 0)),
                  pl.BlockSpec((1, N_LANE_TILES, S5_T * LANES, S5_T * LANES), lambda d, i: (d, 0, 0, 0)),
                  pl.BlockSpec((1, 1, S5_WIDTH), lambda d, i: (d, 0, 0))],
        out_specs=pl.BlockSpec((1, b, TC, S5_WIDTH), lambda d, i: (d, 0, out_blk(d, i), 0)),
        out_shape=jax.ShapeDtypeStruct((2, b, (nc + nl) * TC, S5_WIDTH), F32),
        scratch_shapes=[pltpu.VMEM((N_ROWS, 2 * S5_COLS), F32),
                        pltpu.VMEM((SUBLANES, 2 * S5_COLS), F32)],
        compiler_params=_params("arbitrary", "arbitrary"),
        name="s5scan",
    )(u_c, u_l, w_mat, a_b, v_mat, m_mat, dskip)


def _tail_kernel(x_ref, att_ref, yf_ref, yb_ref, g1_ref, sh2_ref, sc2_ref, g2_ref, wglu_ref, bglu_ref, wout_ref,
                 ln1g_ref, ln1b_ref, wgu_ref, wd_ref, ln2g_ref, ln2b_ref, o_ref, acc_ref):
    z = jax.nn.gelu(yf_ref[0, 0] + yb_ref[0, 0])
    s5o = z * jax.nn.sigmoid(_dot(z.astype(BF16), wglu_ref[...]) + bglu_ref[...])
    mix = _dot(att_ref[0], wout_ref[:MLA_WIDTH, :]) + _dot(s5o.astype(BF16), wout_ref[MLA_WIDTH:, :])
    x1 = _layer_norm(DN_ALPHA * x_ref[0] + g1_ref[0] * mix) * ln1g_ref[...] + ln1b_ref[...]

    xm = (_layer_norm(x1) * (1.0 + sc2_ref[0]) + sh2_ref[0]).astype(BF16)
    for j in range(D_FF // FF_CHUNK):
        gate = _dot(xm, wgu_ref[:, j * FF_CHUNK:(j + 1) * FF_CHUNK])
        up = _dot(xm, wgu_ref[:, D_FF + j * FF_CHUNK:D_FF + (j + 1) * FF_CHUNK])
        hidden = (gate * jax.nn.sigmoid(gate) * up).astype(BF16)
        part = _dot(hidden, wd_ref[j * FF_CHUNK:(j + 1) * FF_CHUNK, :])
        if j == 0:
            acc_ref[...] = part
        else:
            acc_ref[...] += part
    o_ref[0] = _layer_norm(DN_ALPHA * x1 + g2_ref[0] * acc_ref[...]) * ln2g_ref[...] + ln2b_ref[...]


def _tail(x, att, y_dir, mod3, w_glu, b_glu, w_out, ln1_g, ln1_b, w_gu, w_down, ln2_g, ln2_b):
    b, l, _ = x.shape
    tok = lambda w: pl.BlockSpec((1, TM, w), lambda bi, i: (bi, i, 0))
    mod = lambda j: pl.BlockSpec((1, 1, D_MODEL), lambda bi, i: (bi, 0, j))
    y_of = lambda d: pl.BlockSpec((1, 1, TM, S5_WIDTH), lambda bi, i: (d, bi, i, 0))
    consts = (w_glu, b_glu, w_out, ln1_g, ln1_b, w_gu, w_down, ln2_g, ln2_b)
    return pl.pallas_call(
        _tail_kernel,
        grid=(b, l // TM),
        in_specs=[tok(D_MODEL), tok(MLA_WIDTH), y_of(0), y_of(1), mod(2), mod(3), mod(4), mod(5)]
                 + [_const_spec(w.shape) for w in consts],
        out_specs=tok(D_MODEL),
        out_shape=jax.ShapeDtypeStruct((b, l, D_MODEL), F32),
        scratch_shapes=[pltpu.VMEM((TM, D_MODEL), F32)],
        compiler_params=_params("parallel", "parallel"),
        name="tail",
    )(x, att, y_dir, y_dir, mod3, mod3, mod3, mod3, *consts)


def _head_lanes(nope, x0, x1, xp=jnp):
    pad = xp.zeros(nope.shape[:-1] + (LANES // 2 - (NOPE - NOPE_LO) - ROPE_HALF,), nope.dtype)
    return xp.concatenate([nope[..., :NOPE_LO], x0, nope[..., NOPE_LO:], pad, x1], axis=-1)


def _rope_tables(seq):
    pos = np.arange(seq)
    row = (pos // GRID_W).astype(np.float32)
    col = (pos % GRID_W).astype(np.float32)
    n_freq = ROPE // 4
    freqs = np.float32(ROPE_THETA) ** (-np.arange(n_freq, dtype=np.float32) / np.float32(n_freq))
    ang = np.concatenate([row[:, None] * freqs, col[:, None] * freqs], axis=-1).astype(np.float32)
    cos, sin = np.cos(ang), np.sin(ang)
    c = _head_lanes(np.ones((seq, NOPE), np.float32), cos, cos, xp=np)
    s = _head_lanes(np.zeros((seq, NOPE), np.float32), -sin, sin, xp=np)
    return jnp.asarray(c), jnp.asarray(s)


def kernel(x, c, ctx, c_ctx, w_ada, b_ada, w_in, q_norm_g, kv_norm_g, w_uq, w_uk, w_uv, s5_lambda_re, s5_lambda_im, s5_log_dt, s5_b_re, s5_b_im, s5_c_re, s5_c_im, s5_d, s5_w_glu, s5_b_glu, w_out, ln1_g, ln1_b, w_gate_up, w_down, ln2_g, ln2_b):
    assert w_ada.shape[0] == DEPTH == 1
    b, l, _ = x.shape
    assert b == SUBLANES, "the S5 scan maps the batch onto the sublanes of one vreg row"
    lc = ctx.shape[1]
    row2 = lambda t: t.reshape(1, -1)

    cond = jnp.concatenate([c, c_ctx[None, :], jnp.zeros((2 * SUBLANES - b - 1, D_MODEL), F32)], axis=0)
    mod = _adaln(cond, w_ada[0], row2(b_ada[0]))
    mod3 = mod.reshape(mod.shape[0], 1, 6 * D_MODEL)

    wi = w_in[0]
    first, second = (lambda w: w[..., 0::2]), (lambda w: w[..., 1::2])
    w_kr = wi[:, KV_END:ROPE_END]
    rope_tile = _head_lanes(jnp.zeros((D_MODEL, NOPE), F32), first(w_kr), second(w_kr))
    w_in_r = jnp.concatenate([wi[:, :KV_END], wi[:, ROPE_END:], rope_tile], axis=1).astype(BF16)
    uq = w_uq[0]
    wuq = _head_lanes(uq[..., :NOPE], first(uq[..., NOPE:]), second(uq[..., NOPE:]))
    wuq = wuq.reshape(Q_RANK, N_HEADS * HEAD_PAD).astype(BF16)
    no_rope = jnp.zeros((KV_RANK, N_HEADS, ROPE_HALF), F32)
    wuk = _head_lanes(w_uk[0], no_rope, no_rope).reshape(KV_RANK, N_HEADS * HEAD_PAD).astype(BF16)
    wuv = w_uv[0].reshape(KV_RANK, MLA_WIDTH).T.astype(BF16)
    qg, kvg = row2(q_norm_g[0]), row2(kv_norm_g[0])

    rope_c, rope_s = _rope_tables(l)
    ones_half = np.ones((lc, ROPE_HALF), np.float32)
    flat_c = jnp.asarray(_head_lanes(np.ones((lc, NOPE), np.float32), ones_half, ones_half, xp=np))
    flat_s = jnp.zeros((lc, LANES), F32)

    q, k_l, v_l, u_l = _inproj(x, mod3, None, w_in_r, qg, kvg, wuq, wuk, wuv, rope_c, rope_s, True)
    k_c, v_c, u_c = _inproj(ctx, mod3, b, w_in_r, qg, kvg, wuq, wuk, wuv, flat_c, flat_s, False)

    att = _verified_attention(q, k_c, k_l, v_c, v_l)

    a_t, ab, ca, kk = _s5prep(s5_lambda_re[0], s5_lambda_im[0], s5_log_dt[0], s5_b_re[0], s5_b_im[0],
                              s5_c_re[0], s5_c_im[0])
    w_mat, v_mat, m_mat = _s5_matrices(ab, ca, kk)
    slabs = lambda t: t.reshape(2, S5_COLS // SLAB, 1, SLAB)
    a_cat = jnp.concatenate([slabs(a_t[0]), slabs(a_t[1])], axis=2).reshape(2, 2 * S5_COLS)
    a_b = jnp.broadcast_to(a_cat[:, None, :], (2, SUBLANES, 2 * S5_COLS))
    dskip = jnp.stack([s5_d[0], jnp.zeros_like(s5_d[0])])[:, None, :]
    y_dir = _s5scan(u_c, u_l, w_mat, a_b, v_mat, m_mat, dskip)

    return _tail(x, att, y_dir, mod3, s5_w_glu[0].astype(BF16), row2(s5_b_glu[0]), w_out[0].astype(BF16),
                 row2(ln1_g[0]), row2(ln1_b[0]), w_gate_up[0].astype(BF16), w_down[0].astype(BF16),
                 row2(ln2_g[0]), row2(ln2_b[0]))
```

```python
import functools
import math

import jax
import jax.numpy as jnp
import numpy as np
from jax import lax
from jax.experimental import pallas as pl
from jax.experimental.pallas import tpu as pltpu

D_MODEL = 1024
GRID_W = 64
N_HEADS = 8
NOPE = 64
ROPE = 32
V_DIM = 64
Q_RANK = 384
KV_RANK = 256
MLA_WIDTH = N_HEADS * V_DIM
S5_WIDTH = D_MODEL - MLA_WIDTH
S5_GROUP = 16
S5_GROUPS = S5_WIDTH // S5_GROUP
S5_STATE = 64
S5_COLS = S5_GROUPS * S5_STATE
KV_END = Q_RANK + KV_RANK
ROPE_END = KV_END + ROPE
D_FF = 2816
ROPE_THETA = 10000.0
NORM_EPS = 1e-6
DEPTH = 1
DN_ALPHA = (2.0 * DEPTH) ** 0.25

LANES = 128
SUBLANES = 8
HEAD_PAD = LANES
ROPE_HALF = ROPE // 2
NOPE_LO = LANES // 2 - ROPE_HALF
P_COLS = Q_RANK + KV_RANK + S5_WIDTH + LANES
VMEM_LIMIT = 56 * 1024 * 1024

TM = 512
TM_IN = 1024
BQ = 512
DEN_ROWS = 16
HEADS_PER_STEP = 8
DEN_MIN, DEN_MAX = 2.0 ** -90, 2.0 ** 100
Q_SCALE = (NOPE + ROPE) ** -0.5 * math.log2(math.e)
TC = 128
S5_T = 4
N_ROWS = TC // S5_T * SUBLANES
N_LANE_TILES = S5_WIDTH // LANES
GROUPS_PER_TILE = LANES // S5_GROUP
TILE_COLS = GROUPS_PER_TILE * S5_STATE
TILES_PER_SLAB = 2
SLAB = TILES_PER_SLAB * TILE_COLS
FF_CHUNK = 256

F32 = jnp.float32
BF16 = jnp.bfloat16


def _params(*sem):
    return pltpu.CompilerParams(dimension_semantics=sem, vmem_limit_bytes=VMEM_LIMIT)


def _const_spec(shape):
    nd = len(shape)
    return pl.BlockSpec(shape, lambda *_: (0,) * nd, pipeline_mode=pl.Buffered(1))


def _layer_norm(x):
    mu = jnp.mean(x, axis=-1, keepdims=True)
    xc = x - mu
    var = jnp.mean(xc * xc, axis=-1, keepdims=True)
    return xc * lax.rsqrt(var + NORM_EPS)


def _rms_norm(x, g):
    return x * lax.rsqrt(jnp.mean(x * x, axis=-1, keepdims=True) + NORM_EPS) * g


def _dot(a, b):
    return jnp.dot(a, b, preferred_element_type=F32)


_NT = (((1,), (1,)), ((), ()))


def _adaln_kernel(cond_ref, w_ref, b_ref, o_ref):
    cnd = cond_ref[...]
    act = cnd * jax.nn.sigmoid(cnd)
    o_ref[...] = _dot(act.astype(BF16), w_ref[...].astype(BF16)) + b_ref[...]


def _adaln(cond, w_ada, b_ada):
    rows, n = cond.shape[0], w_ada.shape[1]
    tn = D_MODEL
    return pl.pallas_call(
        _adaln_kernel,
        grid=(n // tn,),
        in_specs=[pl.BlockSpec((rows, D_MODEL), lambda j: (0, 0)),
                  pl.BlockSpec((D_MODEL, tn), lambda j: (0, j)),
                  pl.BlockSpec((1, tn), lambda j: (0, j))],
        out_specs=pl.BlockSpec((rows, tn), lambda j: (0, j)),
        out_shape=jax.ShapeDtypeStruct((rows, n), F32),
        compiler_params=_params("arbitrary"),
        name="adaln",
    )(cond, w_ada, b_ada)


def _s5prep_kernel(lre_ref, lim_ref, ldt_ref, bre_ref, bim_ref, cre_ref, cim_ref, at_ref, ab_ref, ca_ref, kk_ref):
    lre, lim = lre_ref[...], lim_ref[...]
    dt = jnp.exp(ldt_ref[...])
    mag = jnp.exp(lre * dt)
    a_re, a_im = mag * jnp.cos(lim * dt), mag * jnp.sin(lim * dt)
    den = lre * lre + lim * lim
    f_re = ((a_re - 1) * lre + a_im * lim) / den
    f_im = (a_im * lre - (a_re - 1) * lim) / den
    b_re, b_im = bre_ref[...], bim_ref[...]
    c_re, c_im = cre_ref[...], cim_ref[...]

    def times_a(z_re, z_im):
        return a_re * z_re - a_im * z_im, a_re * z_im + a_im * z_re

    def over_states(x, y):
        return lax.dot_general(x, y, (((2,), (2,)), ((0,), (0,))), precision=lax.Precision.HIGHEST,
                               preferred_element_type=F32)

    p_re, p_im = f_re * b_re - f_im * b_im, f_re * b_im + f_im * b_re
    q_re, q_im = c_re, c_im
    w_re, w_im = jnp.ones_like(a_re), jnp.zeros_like(a_re)
    for k in range(S5_T):
        ab_ref[0, k] = p_re
        ab_ref[1, k] = p_im
        kk_ref[k] = over_states(c_re, p_re) - over_states(c_im, p_im)
        p_re, p_im = times_a(p_re, p_im)
        q_re, q_im = times_a(q_re, q_im)
        w_re, w_im = times_a(w_re, w_im)
        ca_ref[0, k] = q_re
        ca_ref[1, k] = q_im
    at_ref[0] = w_re
    at_ref[1] = w_im


def _s5prep(lam_re, lam_im, log_dt, b_re, b_im, c_re, c_im):
    n = 2 * S5_GROUPS
    full = (n, S5_GROUP, S5_STATE)
    per_state = lambda t: jnp.broadcast_to(t.reshape(n, 1, S5_STATE), full)
    ldt = jnp.broadcast_to(log_dt.reshape(n, 1, 1), full)
    chan_state = lambda t: t.reshape(full)
    shp = lambda *lead: jax.ShapeDtypeStruct(lead + full, F32)
    at, ab, ca, kk = pl.pallas_call(
        _s5prep_kernel,
        out_shape=(shp(2), shp(2, S5_T), shp(2, S5_T), jax.ShapeDtypeStruct((S5_T, n, S5_GROUP, S5_GROUP), F32)),
        compiler_params=pltpu.CompilerParams(vmem_limit_bytes=VMEM_LIMIT),
        name="s5prep",
    )(per_state(lam_re), per_state(lam_im), ldt, chan_state(jnp.swapaxes(b_re, -1, -2)),
      chan_state(jnp.swapaxes(b_im, -1, -2)), chan_state(c_re), chan_state(c_im))
    by_dir = lambda t, lead: t.reshape(lead + (2, S5_GROUPS) + t.shape[-2:])
    return by_dir(at, (2,))[:, :, :, 0], by_dir(ab, (2, S5_T)), by_dir(ca, (2, S5_T)), by_dir(kk, (S5_T,))


def _spread_groups(x, rows_per_group, col_width, n_col_blocks):
    g8 = GROUPS_PER_TILE
    n, rows, k = x.shape
    spread = np.kron(np.eye(n_col_blocks), np.kron(np.ones((1, g8)), np.eye(col_width))).astype(np.float32)
    cols = spread.shape[1]

    def body(x_ref, spread_ref, o_ref):
        tiled = _dot(x_ref[0].astype(BF16), spread_ref[...])
        row_group = (lax.broadcasted_iota(jnp.int32, (rows, cols), 0) // rows_per_group) % g8
        col_group = (lax.broadcasted_iota(jnp.int32, (rows, cols), 1) // col_width) % g8
        o_ref[0] = jnp.where(row_group == col_group, tiled, 0.0).astype(BF16)

    return pl.pallas_call(
        body,
        grid=(n,),
        in_specs=[pl.BlockSpec((1, rows, k), lambda i: (i, 0, 0)), pl.BlockSpec((k, cols), lambda i: (0, 0))],
        out_specs=pl.BlockSpec((1, rows, cols), lambda i: (i, 0, 0)),
        out_shape=jax.ShapeDtypeStruct((n, rows, cols), BF16),
        compiler_params=_params("parallel"),
        name="s5place",
    )(x, jnp.asarray(spread, dtype=BF16))


def _s5_matrices(ab, ca, kk):
    t_, g8, nt = S5_T, GROUPS_PER_TILE, N_LANE_TILES
    w_all, v_all, m_all = [], [], []
    for d in range(2):
        walked = (lambda s: s) if d == 0 else (lambda s: t_ - 1 - s)
        ab_d = jnp.stack([ab[:, t_ - 1 - walked(s), d] for s in range(t_)], axis=1)
        ab_d = ab_d.reshape(2, t_, nt, g8, S5_GROUP, S5_STATE).transpose(2, 1, 3, 4, 0, 5)
        w_all.append(ab_d.reshape(nt, t_ * LANES, 2 * S5_STATE))
        ca_d = jnp.stack([ca[:, walked(t), d] for t in range(t_)], axis=1)
        ca_d = ca_d * jnp.array([1.0, -1.0], F32).reshape(2, 1, 1, 1, 1)
        ca_d = ca_d.reshape(2, t_, nt, g8, S5_GROUP, S5_STATE).transpose(2, 0, 3, 5, 1, 4)
        v_all.append(ca_d.reshape(nt, 2 * TILE_COLS, t_ * S5_GROUP))
        zero = jnp.zeros_like(kk[0, d])
        blocks = jnp.stack([jnp.stack([kk[walked(t) - walked(s), d] if walked(t) >= walked(s) else zero
                                       for t in range(t_)], axis=0) for s in range(t_)], axis=0)
        blocks = blocks.reshape(t_, t_, nt, g8, S5_GROUP, S5_GROUP).transpose(2, 0, 3, 5, 1, 4)
        m_all.append(blocks.reshape(nt, t_ * LANES, t_ * S5_GROUP))

    def spread(parts, *args):
        out = _spread_groups(jnp.concatenate(parts, axis=0), *args)
        return out.reshape((2, nt) + out.shape[1:])

    return (spread(w_all, S5_GROUP, S5_STATE, 2), spread(v_all, S5_STATE, S5_GROUP, t_),
            spread(m_all, S5_GROUP, S5_GROUP, t_))


def _rope_tile(t, c, s):
    return t * c + pltpu.roll(t, LANES // 2, 1) * s


def _inproj_kernel(with_q, x_ref, sh_ref, sc_ref, win_ref, qg_ref, kvg_ref, wuq_ref, wuk_ref, wuv_ref,
                   c_ref, s_ref, *out_refs):
    if with_q:
        q_ref, k_ref, v_ref, u_ref = out_refs
    else:
        k_ref, v_ref, u_ref = out_refs
    x = x_ref[0]
    xm = _layer_norm(x) * (1.0 + sc_ref[0]) + sh_ref[0]
    p = _dot(xm.astype(BF16), win_ref[...])
    c, s = c_ref[...], s_ref[...]

    kv_c = _rms_norm(p[:, Q_RANK:KV_END], kvg_ref[...]).astype(BF16)
    k = _dot(kv_c, wuk_ref[...])
    kr = _rope_tile(p[:, KV_END + S5_WIDTH:], c, s)
    for h in range(N_HEADS):
        k_ref[0, :, h * HEAD_PAD:(h + 1) * HEAD_PAD] = (k[:, h * HEAD_PAD:(h + 1) * HEAD_PAD] + kr).astype(BF16)
    v_ref[0] = lax.dot_general(wuv_ref[...], kv_c, _NT, preferred_element_type=F32).astype(BF16)
    u_ref[0] = p[:, KV_END:KV_END + S5_WIDTH]

    if with_q:
        q_c = _rms_norm(p[:, :Q_RANK], qg_ref[...]).astype(BF16)
        q = _dot(q_c, wuq_ref[...])
        cq, sq = c * Q_SCALE, s * Q_SCALE
        for h in range(N_HEADS):
            sl = slice(h * HEAD_PAD, (h + 1) * HEAD_PAD)
            q_ref[0, :, sl] = _rope_tile(q[:, sl], cq, sq).astype(BF16)


def _inproj(x, mod3, mod_row, w_in_r, qg, kvg, wuq, wuk, wuv, rope_c, rope_s, with_q):
    b, l, _ = x.shape
    tm = min(TM_IN, l)
    row = (lambda bi: bi) if mod_row is None else (lambda bi: mod_row)
    tok = lambda w: pl.BlockSpec((1, tm, w), lambda bi, i: (bi, i, 0))
    in_specs = [
        tok(D_MODEL),
        pl.BlockSpec((1, 1, D_MODEL), lambda bi, i: (row(bi), 0, 0)),
        pl.BlockSpec((1, 1, D_MODEL), lambda bi, i: (row(bi), 0, 1)),
        _const_spec(w_in_r.shape), _const_spec(qg.shape), _const_spec(kvg.shape),
        _const_spec(wuq.shape), _const_spec(wuk.shape), _const_spec(wuv.shape),
        pl.BlockSpec((tm, LANES), lambda bi, i: (i, 0)),
        pl.BlockSpec((tm, LANES), lambda bi, i: (i, 0)),
    ]
    out_specs = [tok(N_HEADS * HEAD_PAD), pl.BlockSpec((1, MLA_WIDTH, tm), lambda bi, i: (bi, 0, i)), tok(S5_WIDTH)]
    out_shape = [jax.ShapeDtypeStruct((b, l, N_HEADS * HEAD_PAD), BF16),
                 jax.ShapeDtypeStruct((b, MLA_WIDTH, l), BF16),
                 jax.ShapeDtypeStruct((b, l, S5_WIDTH), F32)]
    if with_q:
        out_specs = [tok(N_HEADS * HEAD_PAD)] + out_specs
        out_shape = [jax.ShapeDtypeStruct((b, l, N_HEADS * HEAD_PAD), BF16)] + out_shape
    return pl.pallas_call(
        functools.partial(_inproj_kernel, with_q),
        grid=(b, l // tm),
        in_specs=in_specs, out_specs=out_specs, out_shape=out_shape,
        compiler_params=_params("parallel", "parallel"),
        name="inproj_lat" if with_q else "inproj_ctx",
    )(x, mod3, mod3, w_in_r, qg, kvg, wuq, wuk, wuv, rope_c, rope_s)


def _attn_kernel(fast, q_ref, kc_ref, kl_ref, vc_ref, vl_ref, o_ref, *stat_refs):
    def scores(h):
        ks = slice(h * HEAD_PAD, (h + 1) * HEAD_PAD)
        q = q_ref[0, :, ks]
        return (lax.dot_general(kc_ref[0, :, ks], q, _NT, preferred_element_type=F32),
                lax.dot_general(kl_ref[0, :, ks], q, _NT, preferred_element_type=F32))

    def with_ones(vt):
        return jnp.concatenate([vt, jnp.ones((DEN_ROWS, vt.shape[1]), BF16)], axis=0)

    outs, den_lo, den_hi = [], None, None
    s_next = scores(0)
    for h in range(HEADS_PER_STEP):
        s_c, s_l = s_next
        if h + 1 < HEADS_PER_STEP:
            s_next = scores(h + 1)
        vs = slice(h * V_DIM, (h + 1) * V_DIM)
        m = jnp.max(s_c, axis=0, keepdims=True)
        if not fast:
            m = jnp.maximum(m, jnp.max(s_l, axis=0, keepdims=True))
        e_c = jnp.exp2(s_c - m).astype(BF16)
        e_l = jnp.exp2(s_l - m).astype(BF16)
        o_t = _dot(with_ones(vc_ref[0, vs, :]), e_c) + _dot(with_ones(vl_ref[0, vs, :]), e_l)
        den = o_t[V_DIM:V_DIM + 1]
        outs.append(o_t[:V_DIM] / den)
        den_lo = den if den_lo is None else jnp.minimum(den_lo, den)
        den_hi = den if den_hi is None else jnp.maximum(den_hi, den)
    o_ref[0] = jnp.concatenate(outs, axis=0).T.astype(o_ref.dtype)
    if fast:
        stat_refs[0][0, 0] = jnp.concatenate([den_lo, den_hi], axis=0)


def _attention(q, k_c, k_l, v_c, v_l, fast):
    b, l, _ = q.shape
    lc = k_c.shape[1]
    assert HEADS_PER_STEP == N_HEADS
    kw, vw = N_HEADS * HEAD_PAD, N_HEADS * V_DIM
    out_specs = [pl.BlockSpec((1, BQ, vw), lambda bi, i: (bi, i, 0))]
    out_shape = [jax.ShapeDtypeStruct((b, l, MLA_WIDTH), BF16)]
    if fast:
        out_specs.append(pl.BlockSpec((1, 1, 2, BQ), lambda bi, i: (bi, i, 0, 0)))
        out_shape.append(jax.ShapeDtypeStruct((b, l // BQ, 2, BQ), F32))
    return pl.pallas_call(
        functools.partial(_attn_kernel, fast),
        grid=(b, l // BQ),
        in_specs=[pl.BlockSpec((1, BQ, kw), lambda bi, i: (bi, i, 0)),
                  pl.BlockSpec((1, lc, kw), lambda bi, i: (bi, 0, 0)),
                  pl.BlockSpec((1, l, kw), lambda bi, i: (bi, 0, 0)),
                  pl.BlockSpec((1, vw, lc), lambda bi, i: (bi, 0, 0)),
                  pl.BlockSpec((1, vw, l), lambda bi, i: (bi, 0, 0))],
        out_specs=out_specs, out_shape=out_shape,
        compiler_params=_params("parallel", "arbitrary"),
        name="attn_fast" if fast else "attn",
    )(q, k_c, k_l, v_c, v_l)


def _verified_attention(q, k_c, k_l, v_c, v_l):
    att, den_range = _attention(q, k_c, k_l, v_c, v_l, fast=True)
    in_range = jnp.logical_and(jnp.all(den_range[:, :, 0] >= DEN_MIN), jnp.all(den_range[:, :, 1] <= DEN_MAX))
    return lax.cond(in_range, lambda: att, lambda: _attention(q, k_c, k_l, v_c, v_l, fast=False)[0])


def _s5scan_kernel(n_ctx_chunks, uc_ref, ul_ref, w_ref, a_ref, v_ref, m_ref, dskip_ref, y_ref, hs_ref, st_ref):
    d = pl.program_id(0)
    i = pl.program_id(1)

    @pl.when(i == 0)
    def _():
        st_ref[...] = jnp.zeros_like(st_ref)

    u_bt = jnp.where(i < n_ctx_chunks, uc_ref[...], ul_ref[...])
    u = jnp.swapaxes(u_bt, 0, 1).reshape(N_ROWS // SUBLANES, S5_T, SUBLANES, S5_WIDTH)
    u_pos = [u[:, s].reshape(N_ROWS, S5_WIDTH) for s in range(S5_T)]

    def group_lanes(lt):
        lanes = slice(lt * LANES, (lt + 1) * LANES)
        return jnp.concatenate([p[:, lanes] for p in u_pos], axis=1).astype(BF16)

    def tile_cols(lt):
        base = (lt // TILES_PER_SLAB) * 2 * SLAB + (lt % TILES_PER_SLAB) * TILE_COLS
        return slice(base, base + TILE_COLS), slice(base + SLAB, base + SLAB + TILE_COLS)

    for lt in range(N_LANE_TILES):
        re, im = tile_cols(lt)
        inc = _dot(group_lanes(lt), w_ref[0, lt])
        hs_ref[:, re] = inc[:, :TILE_COLS]
        hs_ref[:, im] = inc[:, TILE_COLS:]

    n_steps = N_ROWS // SUBLANES
    for cs in range(0, 2 * S5_COLS, 2 * SLAB):
        re = slice(cs, cs + SLAB)
        im = slice(cs + SLAB, cs + 2 * SLAB)
        a_re, a_im = a_ref[0, :, re], a_ref[0, :, im]

        def step(tt, carry, re=re, im=im, a_re=a_re, a_im=a_im):
            h_re, h_im = carry
            t = jnp.where(d == 0, tt, n_steps - 1 - tt)
            rows = pl.ds(pl.multiple_of(t * SUBLANES, SUBLANES), SUBLANES)
            n_re = a_re * h_re - a_im * h_im + hs_ref[rows, re]
            n_im = a_re * h_im + a_im * h_re + hs_ref[rows, im]
            hs_ref[rows, re] = h_re
            hs_ref[rows, im] = h_im
            return n_re, n_im

        h_re, h_im = lax.fori_loop(0, n_steps, step, (st_ref[:, re], st_ref[:, im]), unroll=4)
        st_ref[:, re] = h_re
        st_ref[:, im] = h_im

    for lt in range(N_LANE_TILES):
        re, im = tile_cols(lt)
        lanes = slice(lt * LANES, (lt + 1) * LANES)
        y = (_dot(hs_ref[:, re].astype(BF16), v_ref[0, lt, :TILE_COLS])
             + _dot(hs_ref[:, im].astype(BF16), v_ref[0, lt, TILE_COLS:])
             + _dot(group_lanes(lt), m_ref[0, lt]))
        skip = dskip_ref[0, :, lanes]
        y_pos = [(y[:, s * LANES:(s + 1) * LANES] + u_pos[s][:, lanes] * skip)
                 .reshape(N_ROWS // SUBLANES, 1, SUBLANES, LANES) for s in range(S5_T)]
        y_tb = jnp.concatenate(y_pos, axis=1).reshape(TC, SUBLANES, LANES)
        y_ref[0, :, :, lanes] = jnp.swapaxes(y_tb, 0, 1)


def _s5scan(u_c, u_l, w_mat, a_b, v_mat, m_mat, dskip):
    b, l, _ = u_l.shape
    nc, nl = u_c.shape[1] // TC, l // TC

    def ctx_blk(d, i):
        return jnp.clip(jnp.where(d == 0, i, nc - 1 - i), 0, nc - 1)

    def lat_blk(d, i):
        return jnp.clip(jnp.where(d == 0, i - nc, nl - 1 - (i - nc)), 0, nl - 1)

    def out_blk(d, i):
        return jnp.where(i < nc, nl + ctx_blk(d, i), lat_blk(d, i))

    return pl.pallas_call(
        functools.partial(_s5scan_kernel, nc),
        grid=(2, nc + nl),
        in_specs=[pl.BlockSpec((b, TC, S5_WIDTH), lambda d, i: (0, ctx_blk(d, i), 0)),
                  pl.BlockSpec((b, TC, S5_WIDTH), lambda d, i: (0, lat_blk(d, i), 0)),
                  pl.BlockSpec((1, N_LANE_TILES, S5_T * LANES, 2 * TILE_COLS), lambda d, i: (d, 0, 0, 0)),
                  pl.BlockSpec((1, SUBLANES, 2 * S5_COLS), lambda d, i: (d, 0, 0)),
                  pl.BlockSpec((1, N_LANE_TILES, 2 * TILE_COLS, S5_T * LANES), lambda d, i: (d, 0, 0, 0)),
                  pl.BlockSpec((1, N_LANE_TILES, S5_T * LANES, S5_T * LANES), lambda d, i: (d, 0, 0, 0)),
                  pl.BlockSpec((1, 1, S5_WIDTH), lambda d, i: (d, 0, 0))],
        out_specs=pl.BlockSpec((1, b, TC, S5_WIDTH), lambda d, i: (d, 0, out_blk(d, i), 0)),
        out_shape=jax.ShapeDtypeStruct((2, b, (nc + nl) * TC, S5_WIDTH), F32),
        scratch_shapes=[pltpu.VMEM((N_ROWS, 2 * S5_COLS), F32),
                        pltpu.VMEM((SUBLANES, 2 * S5_COLS), F32)],
        compiler_params=_params("arbitrary", "arbitrary"),
        name="s5scan",
    )(u_c, u_l, w_mat, a_b, v_mat, m_mat, dskip)


def _tail_kernel(x_ref, att_ref, yf_ref, yb_ref, g1_ref, sh2_ref, sc2_ref, g2_ref, wglu_ref, bglu_ref, wout_ref,
                 ln1g_ref, ln1b_ref, wgu_ref, wd_ref, ln2g_ref, ln2b_ref, o_ref, acc_ref, x1_ref, xm_ref):
    z = jax.nn.gelu(yf_ref[0, 0] + yb_ref[0, 0])
    s5o = z * jax.nn.sigmoid(_dot(z.astype(BF16), wglu_ref[...]) + bglu_ref[...])
    mix = _dot(att_ref[0], wout_ref[:MLA_WIDTH, :]) + _dot(s5o.astype(BF16), wout_ref[MLA_WIDTH:, :])
    x1 = _layer_norm(DN_ALPHA * x_ref[0] + g1_ref[0] * mix) * ln1g_ref[...] + ln1b_ref[...]

    x1_ref[...] = x1
    xm_ref[...] = (_layer_norm(x1) * (1.0 + sc2_ref[0]) + sh2_ref[0]).astype(BF16)
    for j in range(D_FF // FF_CHUNK):
        gate = _dot(xm_ref[...], wgu_ref[:, j * FF_CHUNK:(j + 1) * FF_CHUNK])
        up = _dot(xm_ref[...], wgu_ref[:, D_FF + j * FF_CHUNK:D_FF + (j + 1) * FF_CHUNK])
        hidden = (gate * jax.nn.sigmoid(gate) * up).astype(BF16)
        part = _dot(hidden, wd_ref[j * FF_CHUNK:(j + 1) * FF_CHUNK, :])
        if j == 0:
            acc_ref[...] = part
        else:
            acc_ref[...] += part
    o_ref[0] = _layer_norm(DN_ALPHA * x1_ref[...] + g2_ref[0] * acc_ref[...]) * ln2g_ref[...] + ln2b_ref[...]


def _tail(x, att, y_dir, mod3, w_glu, b_glu, w_out, ln1_g, ln1_b, w_gu, w_down, ln2_g, ln2_b):
    b, l, _ = x.shape
    tok = lambda w: pl.BlockSpec((1, TM, w), lambda bi, i: (bi, i, 0))
    mod = lambda j: pl.BlockSpec((1, 1, D_MODEL), lambda bi, i: (bi, 0, j))
    y_of = lambda d: pl.BlockSpec((1, 1, TM, S5_WIDTH), lambda bi, i: (d, bi, i, 0))
    consts = (w_glu, b_glu, w_out, ln1_g, ln1_b, w_gu, w_down, ln2_g, ln2_b)
    return pl.pallas_call(
        _tail_kernel,
        grid=(b, l // TM),
        in_specs=[tok(D_MODEL), tok(MLA_WIDTH), y_of(0), y_of(1), mod(2), mod(3), mod(4), mod(5)]
                 + [_const_spec(w.shape) for w in consts],
        out_specs=tok(D_MODEL),
        out_shape=jax.ShapeDtypeStruct((b, l, D_MODEL), F32),
        scratch_shapes=[pltpu.VMEM((TM, D_MODEL), F32),
                        pltpu.VMEM((TM, D_MODEL), F32),
                        pltpu.VMEM((TM, D_MODEL), BF16)],
        compiler_params=_params("parallel", "parallel"),
        name="tail",
    )(x, att, y_dir, y_dir, mod3, mod3, mod3, mod3, *consts)


def _head_lanes(nope, x0, x1, xp=jnp):
    pad = xp.zeros(nope.shape[:-1] + (LANES // 2 - (NOPE - NOPE_LO) - ROPE_HALF,), nope.dtype)
    return xp.concatenate([nope[..., :NOPE_LO], x0, nope[..., NOPE_LO:], pad, x1], axis=-1)


def _rope_tables(seq):
    pos = np.arange(seq)
    row = (pos // GRID_W).astype(np.float32)
    col = (pos % GRID_W).astype(np.float32)
    n_freq = ROPE // 4
    freqs = np.float32(ROPE_THETA) ** (-np.arange(n_freq, dtype=np.float32) / np.float32(n_freq))
    ang = np.concatenate([row[:, None] * freqs, col[:, None] * freqs], axis=-1).astype(np.float32)
    cos, sin = np.cos(ang), np.sin(ang)
    c = _head_lanes(np.ones((seq, NOPE), np.float32), cos, cos, xp=np)
    s = _head_lanes(np.zeros((seq, NOPE), np.float32), -sin, sin, xp=np)
    return jnp.asarray(c), jnp.asarray(s)


def kernel(x, c, ctx, c_ctx, w_ada, b_ada, w_in, q_norm_g, kv_norm_g, w_uq, w_uk, w_uv, s5_lambda_re, s5_lambda_im, s5_log_dt, s5_b_re, s5_b_im, s5_c_re, s5_c_im, s5_d, s5_w_glu, s5_b_glu, w_out, ln1_g, ln1_b, w_gate_up, w_down, ln2_g, ln2_b):
    assert w_ada.shape[0] == DEPTH == 1
    b, l, _ = x.shape
    assert b == SUBLANES, "the S5 scan maps the batch onto the sublanes of one vreg row"
    lc = ctx.shape[1]
    row2 = lambda t: t.reshape(1, -1)

    cond = jnp.concatenate([c, c_ctx[None, :], jnp.zeros((2 * SUBLANES - b - 1, D_MODEL), F32)], axis=0)
    mod = _adaln(cond, w_ada[0], row2(b_ada[0]))
    mod3 = mod.reshape(mod.shape[0], 1, 6 * D_MODEL)

    wi = w_in[0]
    first, second = (lambda w: w[..., 0::2]), (lambda w: w[..., 1::2])
    w_kr = wi[:, KV_END:ROPE_END]
    rope_tile = _head_lanes(jnp.zeros((D_MODEL, NOPE), F32), first(w_kr), second(w_kr))
    w_in_r = jnp.concatenate([wi[:, :KV_END], wi[:, ROPE_END:], rope_tile], axis=1).astype(BF16)
    uq = w_uq[0]
    wuq = _head_lanes(uq[..., :NOPE], first(uq[..., NOPE:]), second(uq[..., NOPE:]))
    wuq = wuq.reshape(Q_RANK, N_HEADS * HEAD_PAD).astype(BF16)
    no_rope = jnp.zeros((KV_RANK, N_HEADS, ROPE_HALF), F32)
    wuk = _head_lanes(w_uk[0], no_rope, no_rope).reshape(KV_RANK, N_HEADS * HEAD_PAD).astype(BF16)
    wuv = w_uv[0].reshape(KV_RANK, MLA_WIDTH).T.astype(BF16)
    qg, kvg = row2(q_norm_g[0]), row2(kv_norm_g[0])

    rope_c, rope_s = _rope_tables(l)
    ones_half = np.ones((lc, ROPE_HALF), np.float32)
    flat_c = jnp.asarray(_head_lanes(np.ones((lc, NOPE), np.float32), ones_half, ones_half, xp=np))
    flat_s = jnp.zeros((lc, LANES), F32)

    q, k_l, v_l, u_l = _inproj(x, mod3, None, w_in_r, qg, kvg, wuq, wuk, wuv, rope_c, rope_s, True)
    k_c, v_c, u_c = _inproj(ctx, mod3, b, w_in_r, qg, kvg, wuq, wuk, wuv, flat_c, flat_s, False)

    att = _verified_attention(q, k_c, k_l, v_c, v_l)

    a_t, ab, ca, kk = _s5prep(s5_lambda_re[0], s5_lambda_im[0], s5_log_dt[0], s5_b_re[0], s5_b_im[0],
                              s5_c_re[0], s5_c_im[0])
    w_mat, v_mat, m_mat = _s5_matrices(ab, ca, kk)
    slabs = lambda t: t.reshape(2, S5_COLS // SLAB, 1, SLAB)
    a_cat = jnp.concatenate([slabs(a_t[0]), slabs(a_t[1])], axis=2).reshape(2, 2 * S5_COLS)
    a_b = jnp.broadcast_to(a_cat[:, None, :], (2, SUBLANES, 2 * S5_COLS))
    dskip = jnp.stack([s5_d[0], jnp.zeros_like(s5_d[0])])[:, None, :]
    y_dir = _s5scan(u_c, u_l, w_mat, a_b, v_mat, m_mat, dskip)

    return _tail(x, att, y_dir, mod3, s5_w_glu[0].astype(BF16), row2(s5_b_glu[0]), w_out[0].astype(BF16),
                 row2(ln1_g[0]), row2(ln1_b[0]), w_gate_up[0].astype(BF16), w_down[0].astype(BF16),
                 row2(ln2_g[0]), row2(ln2_b[0]))
```
